```python
import math
import jax, jax.numpy as jnp
from jax import lax
import numpy as np

D_MODEL = 1024
BATCH = 32
SEQ = 2048
DEPTH = 1
DEC_BATCH = 4
DEC_SEQ = 4096
PAST_LEN = 128

GRID_W = 64
HEAD_DIM = 64
NA_HEADS = 8
GQA_HEADS = 8
GQA_KV_HEADS = 2
GQA_GROUP = GQA_HEADS // GQA_KV_HEADS
NA_WIDTH = NA_HEADS * HEAD_DIM
GQA_WIDTH = GQA_HEADS * HEAD_DIM
KV_WIDTH = GQA_KV_HEADS * HEAD_DIM
MIX_WIDTH = NA_WIDTH + GQA_WIDTH
IN_COLS = 3 * NA_WIDTH + GQA_WIDTH + 2 * KV_WIDTH
SPLITS = [NA_WIDTH, 2 * NA_WIDTH, 3 * NA_WIDTH, 3 * NA_WIDTH + GQA_WIDTH, 3 * NA_WIDTH + GQA_WIDTH + KV_WIDTH]
NA_WIN_R = 8
NA_WIN_C = 16
ROPE_AXIS_DIM = HEAD_DIM // 2
ROPE_THETA = 10000.0
Q_BLOCK = 128
N_EXPERTS = 32
TOP_K = 4
D_FF = D_MODEL
SWIGLU_ALPHA = 1.702
SWIGLU_LIMIT = 7.0
MOE_BLOCK = 256
DN_ALPHA = (2.0 * DEPTH) ** 0.25
DN_BETA = (8.0 * DEPTH) ** -0.25
NEG_INF = -1e30

kernel_name = 'hybrid_na_gqa_moe_encoder'


def _rms_norm(x, g, eps=1e-6):
    xf = x.astype(jnp.float32)
    y = xf * lax.rsqrt(jnp.mean(xf * xf, axis=-1, keepdims=True) + eps)
    return (y * g.astype(jnp.float32)).astype(x.dtype)


def _layer_norm(x, g, b, eps=1e-5):
    xf = x.astype(jnp.float32)
    mu = jnp.mean(xf, axis=-1, keepdims=True)
    xc = xf - mu
    var = jnp.mean(xc * xc, axis=-1, keepdims=True)
    y = xc * lax.rsqrt(var + eps) * g.astype(jnp.float32) + b.astype(jnp.float32)
    return y.astype(x.dtype)


def _neighbourhood_attention(q, k, v, rpb):
    B, T, H, Dh = q.shape
    rows = T // GRID_W
    wr = min(NA_WIN_R, rows)
    qg = q.reshape(B, rows, GRID_W, H, Dh)
    kg = k.reshape(B, rows, GRID_W, H, Dh)
    vg = v.reshape(B, rows, GRID_W, H, Dh)
    r = jnp.arange(rows)
    rs = jnp.clip(r - wr // 2, 0, rows - wr)
    key_rows = rs[:, None] + jnp.arange(wr)[None, :]
    k_blk = kg[:, key_rows]
    v_blk = vg[:, key_rows]
    c = jnp.arange(GRID_W)
    cs = jnp.clip(c - NA_WIN_C // 2, 0, GRID_W - NA_WIN_C)
    col_ok = (c[None, :] >= cs[:, None]) & (c[None, :] < cs[:, None] + NA_WIN_C)
    dr = key_rows - r[:, None] + (NA_WIN_R - 1)
    dc = jnp.clip(c[None, :] - c[:, None], -(NA_WIN_C - 1), NA_WIN_C - 1) + (NA_WIN_C - 1)
    bias = rpb.astype(jnp.float32)[:, dr[:, None, :, None], dc[None, :, None, :]]
    bias = jnp.moveaxis(bias, 0, 1)
    bias = jnp.where(col_ok[:, None, :], bias, NEG_INF)
    s = jnp.einsum('brqhd,brjkhd->brhqjk', qg, k_blk).astype(jnp.float32) * (Dh ** -0.5)
    s = s + bias[None]
    p = jax.nn.softmax(s.reshape(B, rows, H, GRID_W, wr * GRID_W), axis=-1)
    p = p.reshape(B, rows, H, GRID_W, wr, GRID_W).astype(v.dtype)
    o = jnp.einsum('brhqjk,brjkhd->brqhd', p, v_blk)
    return o.reshape(B, T, H * Dh)


def _axial_rope_tables(T):
    t = jnp.arange(T)
    row = (t // GRID_W).astype(jnp.float32)
    col = (t % GRID_W).astype(jnp.float32)
    inv = ROPE_THETA ** (-jnp.arange(0, ROPE_AXIS_DIM, 2, dtype=jnp.float32) / ROPE_AXIS_DIM)
    ang = jnp.stack([row[:, None] * inv[None, :], col[:, None] * inv[None, :]], axis=1)
    return jnp.cos(ang), jnp.sin(ang)


def _apply_axial_rope(x, cos, sin):
    B, T, H, Dh = x.shape
    xf = x.astype(jnp.float32).reshape(B, T, H, 2, 2, ROPE_AXIS_DIM // 2)
    x1 = xf[..., 0, :]
    x2 = xf[..., 1, :]
    c = cos[None, :, None]
    s = sin[None, :, None]
    y = jnp.stack([x1 * c - x2 * s, x2 * c + x1 * s], axis=-2)
    return y.reshape(B, T, H, Dh).astype(x.dtype)


def _gqa_blocked(q, k, v):
    B, T, Hq, Dh = q.shape
    nb = T // Q_BLOCK
    qb = q.reshape(B, nb, Q_BLOCK, GQA_KV_HEADS, GQA_GROUP, Dh).transpose(1, 0, 2, 3, 4, 5)

    def one_block(qblk):
        s = jnp.einsum('bqkgd,btkd->bkgqt', qblk, k).astype(jnp.float32) * (Dh ** -0.5)
        p = jax.nn.softmax(s, axis=-1).astype(v.dtype)
        return jnp.einsum('bkgqt,btkd->bqkgd', p, v)

    o = lax.map(one_block, qb)
    return o.transpose(1, 0, 2, 3, 4, 5).reshape(B, T, Hq * Dh)


def _clamped_swiglu(h):
    x_glu = jnp.minimum(h[..., ::2], SWIGLU_LIMIT)
    x_lin = jnp.clip(h[..., 1::2], -SWIGLU_LIMIT, SWIGLU_LIMIT)
    return x_glu * jax.nn.sigmoid(SWIGLU_ALPHA * x_glu) * (x_lin + 1.0)


def _moe(x2d, router_w, router_b, w1, b1, w2, b2):
    N, D = x2d.shape
    logits = (x2d @ router_w + router_b).astype(jnp.float32)
    top_v, top_i = lax.top_k(logits, TOP_K)
    gates = jax.nn.softmax(top_v, axis=-1)
    A = N * TOP_K
    flat_e = top_i.reshape(-1).astype(jnp.int32)
    flat_tok = jnp.repeat(jnp.arange(N, dtype=jnp.int32), TOP_K)
    flat_g = gates.reshape(-1)
    order = jnp.argsort(flat_e)
    e_sorted = flat_e[order]
    counts = jnp.bincount(flat_e, length=N_EXPERTS).astype(jnp.int32)
    padded = ((counts + MOE_BLOCK - 1) // MOE_BLOCK) * MOE_BLOCK
    pad_end = jnp.cumsum(padded)
    pad_start = pad_end - padded
    start = jnp.cumsum(counts) - counts
    rank = jnp.arange(A, dtype=jnp.int32) - start[e_sorted]
    dest = pad_start[e_sorted] + rank
    n_blocks = -(-A // MOE_BLOCK) + N_EXPERTS
    P = n_blocks * MOE_BLOCK
    slot_tok = jnp.zeros((P,), jnp.int32).at[dest].set(flat_tok[order])
    slot_gate = jnp.zeros((P,), jnp.float32).at[dest].set(flat_g[order])
    block_expert = jnp.clip(jnp.searchsorted(pad_end, jnp.arange(n_blocks, dtype=jnp.int32) * MOE_BLOCK, side='right'), 0, N_EXPERTS - 1)
    xs = x2d[slot_tok].reshape(n_blocks, MOE_BLOCK, D)

    def run_block(args):
        xb, e = args
        h = xb @ w1[e] + b1[e]
        return _clamped_swiglu(h) @ w2[e] + b2[e]

    out = lax.map(run_block, (xs, block_expert)).reshape(P, D)
    contrib = (out.astype(jnp.float32) * slot_gate[:, None]).astype(x2d.dtype)
    return jnp.zeros((N, D), x2d.dtype).at[slot_tok].add(contrib)


def _trunk(x, w_in, rpb, q_norm_g, k_norm_g, g_out_na, g_out_gqa, w_o, ln1_g, ln1_b,
           router_w, router_b, w1, b1, w2, b2, ln2_g, ln2_b):
    B, T, _ = x.shape
    cos, sin = _axial_rope_tables(T)
    for l in range(DEPTH):
        proj = jnp.einsum('btd,de->bte', x, w_in[l])
        q_a, k_a, v_a, q_b, k_b, v_b = jnp.split(proj, SPLITS, axis=-1)
        heads = lambda z: z.reshape(B, T, -1, HEAD_DIM)
        o_a = _neighbourhood_attention(heads(q_a), heads(k_a), heads(v_a), rpb[l])
        qb = _apply_axial_rope(_rms_norm(heads(q_b), q_norm_g[l]), cos, sin)
        kb = _apply_axial_rope(_rms_norm(heads(k_b), k_norm_g[l]), cos, sin)
        o_b = _gqa_blocked(qb, kb, heads(v_b))
        mixed = jnp.concatenate([_rms_norm(o_a, g_out_na[l]), _rms_norm(o_b, g_out_gqa[l])], axis=-1)
        x = _layer_norm(DN_ALPHA * x + mixed @ w_o[l], ln1_g[l], ln1_b[l])
        y = _moe(x.reshape(B * T, D_MODEL), router_w[l], router_b[l], w1[l], b1[l], w2[l], b2[l])
        x = _layer_norm(DN_ALPHA * x + y.reshape(B, T, D_MODEL), ln2_g[l], ln2_b[l])
    return x


def setup_inputs(seed: int = 0) -> dict:
    key = jax.random.key(seed)
    ks = jax.random.split(key, 20)
    f32 = jnp.float32
    L = DEPTH

    def nrm(k, shape, scale):
        return scale * jax.random.normal(k, shape, f32)

    col_scale = np.ones((IN_COLS,), np.float32)
    col_scale[2 * NA_WIDTH:3 * NA_WIDTH] = DN_BETA
    col_scale[3 * NA_WIDTH + GQA_WIDTH + KV_WIDTH:] = DN_BETA
    w_in = nrm(ks[2], (L, D_MODEL, IN_COLS), D_MODEL ** -0.5) * jnp.asarray(col_scale)
    return {
        'x_prompt': jax.random.normal(ks[0], (BATCH, SEQ, D_MODEL), f32),
        'x_sample': jax.random.normal(ks[1], (DEC_BATCH, DEC_SEQ, D_MODEL), f32),
        'w_in': w_in,
        'rpb': nrm(ks[3], (L, NA_HEADS, 2 * NA_WIN_R - 1, 2 * NA_WIN_C - 1), 0.02),
        'q_norm_g': 1.0 + nrm(ks[4], (L, HEAD_DIM), 0.01),
        'k_norm_g': 1.0 + nrm(ks[5], (L, HEAD_DIM), 0.01),
        'g_out_na': 1.0 + nrm(ks[6], (L, NA_WIDTH), 0.01),
        'g_out_gqa': 1.0 + nrm(ks[7], (L, GQA_WIDTH), 0.01),
        'w_o': nrm(ks[8], (L, MIX_WIDTH, D_MODEL), DN_BETA * MIX_WIDTH ** -0.5),
        'ln1_g': 1.0 + nrm(ks[9], (L, D_MODEL), 0.01),
        'ln1_b': nrm(ks[10], (L, D_MODEL), 0.01),
        'router_w': nrm(ks[11], (L, D_MODEL, N_EXPERTS), D_MODEL ** -0.5),
        'router_b': nrm(ks[12], (L, N_EXPERTS), 0.01),
        'w1': nrm(ks[13], (L, N_EXPERTS, D_MODEL, 2 * D_FF), D_MODEL ** -0.5),
        'b1': nrm(ks[14], (L, N_EXPERTS, 2 * D_FF), 0.01),
        'w2': nrm(ks[15], (L, N_EXPERTS, D_FF, D_MODEL), DN_BETA * D_FF ** -0.5),
        'b2': nrm(ks[16], (L, N_EXPERTS, D_MODEL), 0.01),
        'ln2_g': 1.0 + nrm(ks[17], (L, D_MODEL), 0.01),
        'ln2_b': nrm(ks[18], (L, D_MODEL), 0.01),
    }


def reference(x_prompt, x_sample, w_in, rpb, q_norm_g, k_norm_g, g_out_na, g_out_gqa, w_o,
              ln1_g, ln1_b, router_w, router_b, w1, b1, w2, b2, ln2_g, ln2_b):
    y_prompt = _trunk(x_prompt, w_in, rpb, q_norm_g, k_norm_g, g_out_na, g_out_gqa, w_o, ln1_g, ln1_b,
                      router_w, router_b, w1, b1, w2, b2, ln2_g, ln2_b)
    y_sample = _trunk(x_sample, w_in, rpb, q_norm_g, k_norm_g, g_out_na, g_out_gqa, w_o, ln1_g, ln1_b,
                      router_w, router_b, w1, b1, w2, b2, ln2_g, ln2_b)
    return (y_prompt, y_sample)
```

```python
import functools

import jax
import jax.numpy as jnp
from jax import lax
from jax.experimental import pallas as pl
from jax.experimental.pallas import tpu as pltpu

D_MODEL = 1024
GRID_W = 64
HEAD_DIM = 64
NA_HEADS = 8
GQA_HEADS = 8
GQA_KV_HEADS = 2
GQA_GROUP = GQA_HEADS // GQA_KV_HEADS
NA_WIDTH = NA_HEADS * HEAD_DIM
GQA_WIDTH = GQA_HEADS * HEAD_DIM
KV_WIDTH = GQA_KV_HEADS * HEAD_DIM
NA_WIN_R = 8
NA_WIN_C = 16
ROPE_AXIS_DIM = HEAD_DIM // 2
ROPE_THETA = 10000.0
N_EXPERTS = 32
TOP_K = 4
D_FF = D_MODEL
SWIGLU_ALPHA = 1.702
SWIGLU_LIMIT = 7.0
DEPTH = 1
DN_ALPHA = (2.0 * DEPTH) ** 0.25
NEG_INF = -1e30
RMS_EPS = 1e-6
LN_EPS = 1e-5

LANES = 128
QEXP_WIDTH = GQA_HEADS * LANES
PROJ_COLS = 3 * NA_WIDTH + QEXP_WIDTH + 2 * KV_WIDTH

PROJ_TM = 512
NA_ROWS = 8
GQA_TQ = 256
MIX_TM = 512
DISP_TM = 256
COMB_TM = 256
EXP_BLK = 512
VMEM_LIMIT = 56 * 1024 * 1024

F32 = jnp.float32
BF16 = jnp.bfloat16

TOK_ROWS = D_MODEL // LANES


def _tok_load(ref, n, lead=()):
    return jnp.concatenate([ref[lead + (pl.ds(s, n, stride=TOK_ROWS), slice(None))] for s in range(TOK_ROWS)],
                           axis=1)


def _tok_store(ref, val):
    n = val.shape[0]
    for s in range(TOK_ROWS):
        ref[pl.ds(s, n, stride=TOK_ROWS), :] = val[:, s * LANES:(s + 1) * LANES]


def _cparams(sem):
    return pltpu.CompilerParams(dimension_semantics=sem, vmem_limit_bytes=VMEM_LIMIT)


def _rope(y, cos, sin, first_half):
    partner = jnp.where(first_half, pltpu.roll(y, LANES - 16, 1), pltpu.roll(y, 16, 1))
    return y * cos + partner * sin


def _proj_kernel(x_ref, w_ref, cos_ref, sin_ref, gq_ref, gk_ref,
                 qa_ref, ka_ref, va_ref, qb_ref, kb_ref, vb_ref):
    x = x_ref[...].astype(BF16)
    acc = jnp.dot(x, w_ref[...], preferred_element_type=F32)
    qa_ref[...] = acc[:, 0:NA_WIDTH].astype(BF16)
    ka_ref[...] = acc[:, NA_WIDTH:2 * NA_WIDTH].astype(BF16)
    va_ref[...] = acc[:, 2 * NA_WIDTH:3 * NA_WIDTH].astype(BF16)
    cos = cos_ref[...]
    sin = sin_ref[...]
    lane = lax.broadcasted_iota(jnp.int32, cos.shape, 1)
    first_half = (lane % 32) < 16
    base = 3 * NA_WIDTH
    gq = gq_ref[...]
    for h in range(GQA_HEADS):
        y = acc[:, base + h * LANES: base + (h + 1) * LANES]
        ms = jnp.sum(y * y, axis=-1, keepdims=True) * (1.0 / HEAD_DIM)
        y = y * lax.rsqrt(ms + RMS_EPS) * gq
        qb_ref[:, h * LANES:(h + 1) * LANES] = _rope(y, cos, sin, first_half).astype(BF16)
    base += QEXP_WIDTH
    y = acc[:, base: base + KV_WIDTH]
    y2 = y * y
    lo = lane < HEAD_DIM
    ms_lo = jnp.sum(jnp.where(lo, y2, 0.0), axis=-1, keepdims=True)
    ms_hi = jnp.sum(jnp.where(lo, 0.0, y2), axis=-1, keepdims=True)
    ms = jnp.where(lo, ms_lo, ms_hi) * (1.0 / HEAD_DIM)
    y = y * lax.rsqrt(ms + RMS_EPS) * gk_ref[...]
    kb_ref[...] = _rope(y, cos, sin, first_half).astype(BF16)
    base += KV_WIDTH
    vb_ref[...] = acc[:, base: base + KV_WIDTH].astype(BF16)


def _proj(x2d, seq_len, w, cos, sin, gq, gk):
    n = x2d.shape[0]
    tm = PROJ_TM
    assert n % tm == 0 and seq_len % tm == 0
    pos_blocks = seq_len // tm
    row = lambda i: (i, 0)
    const = lambda i: (0, 0)
    outs = [(NA_WIDTH, BF16)] * 3 + [(QEXP_WIDTH, BF16), (KV_WIDTH, BF16), (KV_WIDTH, BF16)]
    return pl.pallas_call(
        _proj_kernel,
        grid=(n // tm,),
        in_specs=[
            pl.BlockSpec((tm, D_MODEL), row),
            pl.BlockSpec((D_MODEL, PROJ_COLS), const),
            pl.BlockSpec((tm, LANES), lambda i: (i % pos_blocks, 0)),
            pl.BlockSpec((tm, LANES), lambda i: (i % pos_blocks, 0)),
            pl.BlockSpec((1, LANES), const),
            pl.BlockSpec((1, LANES), const),
        ],
        out_specs=[pl.BlockSpec((tm, c), row) for c, _ in outs],
        out_shape=[jax.ShapeDtypeStruct((n, c), dt) for c, dt in outs],
        compiler_params=_cparams(("parallel",)),
        name="proj",
    )(x2d, w, cos, sin, gq, gk)


def _na_kernel(q_ref, k_ref, v_ref, bias_ref, o_ref, *, rows):
    j = pl.program_id(1)
    lane = lax.broadcasted_iota(jnp.int32, (GRID_W, LANES), 1)
    lo = lane < HEAD_DIM

    def one_row(rr, carry):
        r = j * NA_ROWS + rr
        rs = jnp.clip(r - NA_WIN_R // 2, 0, rows - NA_WIN_R)
        var = r - rs
        k0 = pl.multiple_of(rs * GRID_W, GRID_W)
        q0 = pl.multiple_of(rr * GRID_W, GRID_W)
        for g in range(NA_HEADS // 2):
            cols = slice(g * LANES, (g + 1) * LANES)
            q = q_ref[0, pl.ds(q0, GRID_W), cols]
            zero = jnp.zeros_like(q)
            q2 = jnp.concatenate([jnp.where(lo, q, zero), jnp.where(lo, zero, q)], axis=0)
            k = k_ref[0, pl.ds(k0, NA_WIN_R * GRID_W), cols]
            v = v_ref[0, pl.ds(k0, NA_WIN_R * GRID_W), cols]
            s = lax.dot_general(q2, k, (((1,), (1,)), ((), ())), preferred_element_type=F32)
            s = s + jnp.concatenate([bias_ref[var, 2 * g], bias_ref[var, 2 * g + 1]], axis=0)
            m = jnp.max(s, axis=-1, keepdims=True)
            p = jnp.exp(s - m)
            l = jnp.sum(p, axis=-1, keepdims=True)
            o2 = jnp.dot(p.astype(BF16), v, preferred_element_type=F32) / l
            o = jnp.where(lo, o2[:GRID_W], o2[GRID_W:])
            o_ref[0, pl.ds(q0, GRID_W), cols] = o.astype(BF16)
        return carry

    lax.fori_loop(0, NA_ROWS, one_row, 0)


def _na(qa, ka, va, bias):
    b, t, _ = qa.shape
    rows = t // GRID_W
    assert rows >= NA_WIN_R and rows % NA_ROWS == 0
    tq = NA_ROWS * GRID_W
    return pl.pallas_call(
        functools.partial(_na_kernel, rows=rows),
        grid=(b, rows // NA_ROWS),
        in_specs=[
            pl.BlockSpec((1, tq, NA_WIDTH), lambda i, j: (i, j, 0)),
            pl.BlockSpec((1, t, NA_WIDTH), lambda i, j: (i, 0, 0)),
            pl.BlockSpec((1, t, NA_WIDTH), lambda i, j: (i, 0, 0)),
            pl.BlockSpec(bias.shape, lambda i, j: (0, 0, 0, 0), pipeline_mode=pl.Buffered(1)),
        ],
        out_specs=pl.BlockSpec((1, tq, NA_WIDTH), lambda i, j: (i, j, 0)),
        out_shape=jax.ShapeDtypeStruct((b, t, NA_WIDTH), BF16),
        compiler_params=_cparams(("parallel", "parallel")),
        name="na",
    )(qa, ka, va, bias)


def _na_bias_tables(rpb):
    off = jnp.arange(NA_WIN_R)
    jrow = jnp.arange(NA_WIN_R)
    dr = jrow[None, :] - off[:, None] + (NA_WIN_R - 1)
    c = jnp.arange(GRID_W)
    cs = jnp.clip(c - NA_WIN_C // 2, 0, GRID_W - NA_WIN_C)
    col_ok = (c[None, :] >= cs[:, None]) & (c[None, :] < cs[:, None] + NA_WIN_C)
    dc = jnp.clip(c[None, :] - c[:, None], -(NA_WIN_C - 1), NA_WIN_C - 1) + (NA_WIN_C - 1)
    bias = rpb.astype(F32)[:, dr[:, None, :, None], dc[None, :, None, :]]
    bias = jnp.where(col_ok[None, None, :, None, :], bias, NEG_INF)
    return jnp.moveaxis(bias, 0, 1).reshape(NA_WIN_R, NA_HEADS, GRID_W, NA_WIN_R * GRID_W)


def _gqa_kernel(q_ref, k_ref, v_ref, o_ref):
    k = k_ref[0]
    v = v_ref[0]
    lane = lax.broadcasted_iota(jnp.int32, (q_ref.shape[1], LANES), 1)
    lo = lane < HEAD_DIM

    def head(h):
        q = q_ref[0, :, h * LANES:(h + 1) * LANES]
        s = lax.dot_general(q, k, (((1,), (1,)), ((), ())), preferred_element_type=F32)
        m = jnp.max(s, axis=-1, keepdims=True)
        p = jnp.exp(s - m)
        l = jnp.sum(p, axis=-1, keepdims=True)
        return jnp.dot(p.astype(BF16), v, preferred_element_type=F32) / l

    for g in range(GQA_GROUP):
        o = jnp.where(lo, head(g), head(GQA_GROUP + g))
        o_ref[0, :, g * LANES:(g + 1) * LANES] = o.astype(BF16)


def _gqa(qb, kb, vb):
    b, t, _ = qb.shape
    tq = GQA_TQ
    assert t % tq == 0
    return pl.pallas_call(
        _gqa_kernel,
        grid=(b, t // tq),
        in_specs=[
            pl.BlockSpec((1, tq, QEXP_WIDTH), lambda i, j: (i, j, 0)),
            pl.BlockSpec((1, t, KV_WIDTH), lambda i, j: (i, 0, 0)),
            pl.BlockSpec((1, t, KV_WIDTH), lambda i, j: (i, 0, 0)),
        ],
        out_specs=pl.BlockSpec((1, tq, GQA_WIDTH), lambda i, j: (i, j, 0)),
        out_shape=jax.ShapeDtypeStruct((b, t, GQA_WIDTH), BF16),
        compiler_params=_cparams(("parallel", "parallel")),
        name="gqa",
    )(qb, kb, vb)


def _layer_norm(z, g, b):
    mu = jnp.mean(z, axis=-1, keepdims=True)
    zc = z - mu
    var = jnp.mean(zc * zc, axis=-1, keepdims=True)
    return zc * lax.rsqrt(var + LN_EPS) * g + b


def _rms(o, g):
    return o * lax.rsqrt(jnp.mean(o * o, axis=-1, keepdims=True) + RMS_EPS) * g


def _mix_kernel(oa_ref, ob_ref, x_ref, woa_ref, wob_ref, ga_ref, gb_ref, lng_ref, lnb_ref,
                rw_ref, rb_ref, tri_ref, cnt_in_ref,
                x1_ref, idx_ref, gate_ref, rank_ref, cnt_ref, carry_ref):
    @pl.when(pl.program_id(0) == 0)
    def _():
        carry_ref[...] = cnt_in_ref[...]

    na = _rms(oa_ref[...].astype(F32), ga_ref[...]).astype(BF16)
    nb = _rms(ob_ref[...].astype(F32), gb_ref[...]).astype(BF16)
    mixed = (jnp.dot(na, woa_ref[...], preferred_element_type=F32)
             + jnp.dot(nb, wob_ref[...], preferred_element_type=F32))
    x1 = _layer_norm(DN_ALPHA * x_ref[...] + mixed, lng_ref[...], lnb_ref[...])
    _tok_store(x1_ref, x1)

    logits = lax.dot_general(rw_ref[...], x1.astype(BF16), (((1,), (1,)), ((), ())),
                             preferred_element_type=F32) + rb_ref[...]
    tm = logits.shape[1]
    eidx = lax.broadcasted_iota(jnp.int32, (N_EXPERTS, tm), 0).astype(F32)
    work = logits
    vals, idxs, hots = [], [], []
    for _ in range(TOP_K):
        m = jnp.max(work, axis=0, keepdims=True)
        sel = jnp.min(jnp.where(work == m, eidx, float(N_EXPERTS)), axis=0, keepdims=True)
        hot = eidx == sel
        vals.append(m)
        idxs.append(sel)
        hots.append(hot)
        work = jnp.where(hot, -jnp.inf, work)
    es = [jnp.exp(v - vals[0]) for v in vals]
    den = es[0] + es[1] + es[2] + es[3]
    gate_ref[...] = jnp.concatenate([e / den for e in es], axis=0)
    idx_ref[...] = jnp.concatenate(idxs, axis=0).astype(jnp.int32)

    hot_all = hots[0] | hots[1] | hots[2] | hots[3]
    onehot = jnp.where(hot_all, 1.0, 0.0)
    before = jnp.dot(onehot.astype(BF16), tri_ref[...], preferred_element_type=F32)
    before = before + carry_ref[:, 0:1]
    ranks = [jnp.sum(jnp.where(hot, before, 0.0), axis=0, keepdims=True) for hot in hots]
    rank_ref[...] = jnp.concatenate(ranks, axis=0).astype(jnp.int32)
    carry_ref[...] = carry_ref[...] + jnp.sum(onehot, axis=1, keepdims=True)
    cnt_ref[...] = carry_ref[...]


def _mix(oa, ob, x2d, woa, wob, ga, gb, lng, lnb, rw_t, rb, tri, cnt_in):
    n = x2d.shape[0]
    tm = MIX_TM
    assert n % tm == 0
    row = lambda i: (i, 0)
    col = lambda i: (0, i)
    const = lambda i: (0, 0)
    full = lambda a: pl.BlockSpec(a.shape, const)
    return pl.pallas_call(
        _mix_kernel,
        grid=(n // tm,),
        in_specs=[
            pl.BlockSpec((tm, NA_WIDTH), row),
            pl.BlockSpec((tm, GQA_WIDTH), row),
            pl.BlockSpec((tm, D_MODEL), row),
            full(woa), full(wob), full(ga), full(gb), full(lng), full(lnb),
            full(rw_t), full(rb), full(tri), full(cnt_in),
        ],
        out_specs=[
            pl.BlockSpec((tm * TOK_ROWS, LANES), row),
            pl.BlockSpec((TOP_K, tm), col),
            pl.BlockSpec((TOP_K, tm), col),
            pl.BlockSpec((TOP_K, tm), col),
            pl.BlockSpec((N_EXPERTS, LANES), const),
        ],
        out_shape=[
            jax.ShapeDtypeStruct((n * TOK_ROWS, LANES), F32),
            jax.ShapeDtypeStruct((TOP_K, n), jnp.int32),
            jax.ShapeDtypeStruct((TOP_K, n), F32),
            jax.ShapeDtypeStruct((TOP_K, n), jnp.int32),
            jax.ShapeDtypeStruct((N_EXPERTS, LANES), F32),
        ],
        scratch_shapes=[pltpu.VMEM((N_EXPERTS, LANES), F32)],
        compiler_params=_cparams(("arbitrary",)),
        name="mix",
    )(oa, ob, x2d, woa, wob, ga, gb, lng, lnb, rw_t, rb, tri, cnt_in)


def _dispatch_kernel(tail_ref, has_tail_ref, dest_ref, x_ref, *rest, zero_tails):
    if zero_tails:
        xs_ref, zeros_ref, sem = rest
    else:
        _, xs_ref, sem = rest
    tm = x_ref.shape[0] // TOK_ROWS

    if zero_tails:
        @pl.when(pl.program_id(0) == 0)
        def _():
            zeros_ref[...] = jnp.zeros_like(zeros_ref)

            def tail_copy(e):
                start = pl.multiple_of(tail_ref[e], EXP_BLK * TOK_ROWS)
                return pltpu.make_async_copy(zeros_ref, xs_ref.at[pl.ds(start, EXP_BLK * TOK_ROWS), :], sem)

            for e in range(N_EXPERTS):
                @pl.when(has_tail_ref[e] != 0)
                def _():
                    tail_copy(e).start()
            for e in range(N_EXPERTS):
                @pl.when(has_tail_ref[e] != 0)
                def _():
                    tail_copy(e).wait()

    def row_copy(i, k):
        src = pl.multiple_of(i * TOK_ROWS, TOK_ROWS)
        dst = pl.multiple_of(dest_ref[0, 0, k * tm + i], TOK_ROWS)
        return pltpu.make_async_copy(x_ref.at[pl.ds(src, TOK_ROWS), :], xs_ref.at[pl.ds(dst, TOK_ROWS), :], sem)

    def issue(i, c):
        for k in range(TOP_K):
            row_copy(i, k).start()
        return c

    def drain(i, c):
        for k in range(TOP_K):
            row_copy(i, k).wait()
        return c

    lax.fori_loop(0, tm, issue, 0)
    lax.fori_loop(0, tm, drain, 0)


def _dispatch(tail, has_tail, dest_tiles, x1, xs_prev, n_slots):
    n = x1.shape[0] // TOK_ROWS
    tm = DISP_TM
    assert n % tm == 0
    zero_tails = xs_prev is None
    in_specs = [
        pl.BlockSpec((1, 1, TOP_K * tm), lambda i, *_: (i, 0, 0), memory_space=pltpu.SMEM),
        pl.BlockSpec((tm * TOK_ROWS, LANES), lambda i, *_: (i, 0)),
    ]
    args = [dest_tiles, x1]
    scratch = []
    aliases = {}
    if zero_tails:
        scratch.append(pltpu.VMEM((EXP_BLK * TOK_ROWS, LANES), F32))
    else:
        in_specs.append(pl.BlockSpec(memory_space=pl.ANY))
        args.append(xs_prev)
        aliases = {4: 0}
    scratch.append(pltpu.SemaphoreType.DMA(()))
    return pl.pallas_call(
        functools.partial(_dispatch_kernel, zero_tails=zero_tails),
        grid_spec=pltpu.PrefetchScalarGridSpec(
            num_scalar_prefetch=2,
            grid=(n // tm,),
            in_specs=in_specs,
            out_specs=pl.BlockSpec(memory_space=pl.ANY),
            scratch_shapes=scratch,
        ),
        out_shape=jax.ShapeDtypeStruct((n_slots * TOK_ROWS, LANES), F32),
        input_output_aliases=aliases,
        compiler_params=pltpu.CompilerParams(dimension_semantics=("arbitrary",), vmem_limit_bytes=VMEM_LIMIT,
                                             has_side_effects=True),
        name="dispatch",
    )(tail, has_tail, *args)


def _expert_kernel(be_ref, nused_ref, xs_ref, w1g_ref, w1l_ref, w2_ref, b1g_ref, b1l_ref, b2_ref, o_ref):
    @pl.when(pl.program_id(0) < nused_ref[0])
    def _():
        x = _tok_load(xs_ref, EXP_BLK).astype(BF16)
        glu = jnp.dot(x, w1g_ref[0], preferred_element_type=F32) + b1g_ref[0]
        lin = jnp.dot(x, w1l_ref[0], preferred_element_type=F32) + b1l_ref[0]
        glu = jnp.minimum(glu, SWIGLU_LIMIT)
        lin = jnp.clip(lin, -SWIGLU_LIMIT, SWIGLU_LIMIT)
        act = glu * jax.nn.sigmoid(SWIGLU_ALPHA * glu) * (lin + 1.0)
        _tok_store(o_ref, jnp.dot(act.astype(BF16), w2_ref[0], preferred_element_type=F32) + b2_ref[0])


def _experts(block_expert, n_used, xs, w1g, w1l, w2, b1g, b1l, b2):
    n_slots = xs.shape[0] // TOK_ROWS
    n_blocks = n_slots // EXP_BLK
    slot = lambda i, be, nu: (jnp.minimum(i, nu[0] - 1), 0)
    wsel = lambda i, be, nu: (be[i], 0, 0)
    return pl.pallas_call(
        _expert_kernel,
        grid_spec=pltpu.PrefetchScalarGridSpec(
            num_scalar_prefetch=2,
            grid=(n_blocks,),
            in_specs=[
                pl.BlockSpec((EXP_BLK * TOK_ROWS, LANES), slot),
                pl.BlockSpec((1, D_MODEL, D_FF), wsel),
                pl.BlockSpec((1, D_MODEL, D_FF), wsel),
                pl.BlockSpec((1, D_FF, D_MODEL), wsel),
                pl.BlockSpec((1, 1, D_FF), wsel),
                pl.BlockSpec((1, 1, D_FF), wsel),
                pl.BlockSpec((1, 1, D_MODEL), wsel),
            ],
            out_specs=pl.BlockSpec((EXP_BLK * TOK_ROWS, LANES), slot),
        ),
        out_shape=jax.ShapeDtypeStruct((n_slots * TOK_ROWS, LANES), F32),
        compiler_params=_cparams(("arbitrary",)),
        name="experts",
    )(block_expert, n_used, xs, w1g, w1l, w2, b1g, b1l, b2)


def _combine_kernel(dest_ref, x1_ref, gate_ref, lng_ref, lnb_ref, ys_ref, o_ref, buf_ref, sem):
    tm = x1_ref.shape[0] // TOK_ROWS

    def row_copy(i, k):
        src = pl.multiple_of(dest_ref[0, 0, k * tm + i], TOK_ROWS)
        dst = pl.multiple_of(i * TOK_ROWS, TOK_ROWS)
        return pltpu.make_async_copy(ys_ref.at[pl.ds(src, TOK_ROWS), :], buf_ref.at[k, pl.ds(dst, TOK_ROWS), :], sem)

    def issue(i, c):
        for k in range(TOP_K):
            row_copy(i, k).start()
        return c

    def drain(i, c):
        for k in range(TOP_K):
            row_copy(i, k).wait()
        return c

    lax.fori_loop(0, tm, issue, 0)
    lax.fori_loop(0, tm, drain, 0)
    gate = gate_ref[...]
    y = _tok_load(buf_ref, tm, (0,)) * gate[:, 0:1]
    for k in range(1, TOP_K):
        y = y + _tok_load(buf_ref, tm, (k,)) * gate[:, k:k + 1]
    o_ref[...] = _layer_norm(DN_ALPHA * _tok_load(x1_ref, tm) + y, lng_ref[...], lnb_ref[...])


def _combine(dest_tiles, x1, gate, lng, lnb, ys):
    n = x1.shape[0] // TOK_ROWS
    tm = COMB_TM
    assert n % tm == 0
    return pl.pallas_call(
        _combine_kernel,
        grid=(n // tm,),
        in_specs=[
            pl.BlockSpec((1, 1, TOP_K * tm), lambda i: (i, 0, 0), memory_space=pltpu.SMEM),
            pl.BlockSpec((tm * TOK_ROWS, LANES), lambda i: (i, 0)),
            pl.BlockSpec((tm, TOP_K), lambda i: (i, 0)),
            pl.BlockSpec((1, D_MODEL), lambda i: (0, 0)),
            pl.BlockSpec((1, D_MODEL), lambda i: (0, 0)),
            pl.BlockSpec(memory_space=pl.ANY),
        ],
        out_specs=pl.BlockSpec((tm, D_MODEL), lambda i: (i, 0)),
        out_shape=jax.ShapeDtypeStruct((n, D_MODEL), F32),
        scratch_shapes=[pltpu.VMEM((TOP_K, tm * TOK_ROWS, LANES), F32), pltpu.SemaphoreType.DMA(())],
        compiler_params=_cparams(("arbitrary",)),
        name="combine",
    )(dest_tiles, x1, gate, lng, lnb, ys)


def _rope_tables(t):
    pos = jnp.arange(t)
    row = (pos // GRID_W).astype(F32)
    col = (pos % GRID_W).astype(F32)
    inv = ROPE_THETA ** (-jnp.arange(0, ROPE_AXIS_DIM, 2, dtype=F32) / ROPE_AXIS_DIM)
    ar = row[:, None] * inv[None, :]
    ac = col[:, None] * inv[None, :]
    cos = jnp.concatenate([jnp.cos(ar), jnp.cos(ar), jnp.cos(ac), jnp.cos(ac)], axis=1)
    sin = jnp.concatenate([-jnp.sin(ar), jnp.sin(ar), -jnp.sin(ac), jnp.sin(ac)], axis=1)
    return jnp.tile(cos, (1, 2)), jnp.tile(sin, (1, 2))


def _prep_w_in(w):
    scale = HEAD_DIM ** -0.5
    qa = w[:, :NA_WIDTH] * scale
    kva = w[:, NA_WIDTH:3 * NA_WIDTH]
    qb = w[:, 3 * NA_WIDTH:3 * NA_WIDTH + GQA_WIDTH].reshape(D_MODEL, GQA_HEADS, HEAD_DIM)
    rest = w[:, 3 * NA_WIDTH + GQA_WIDTH:]
    kv_of_head = (jnp.arange(GQA_HEADS) // GQA_GROUP)[None, :, None]
    zero = jnp.zeros_like(qb)
    qexp = jnp.concatenate([jnp.where(kv_of_head == 0, qb, zero), jnp.where(kv_of_head == 1, qb, zero)], axis=-1)
    return jnp.concatenate([qa, kva, qexp.reshape(D_MODEL, QEXP_WIDTH), rest], axis=1).astype(BF16)


def _gqa_out_perm():
    heads = []
    for g in range(GQA_GROUP):
        heads += [g, GQA_GROUP + g]
    return (jnp.asarray(heads)[:, None] * HEAD_DIM + jnp.arange(HEAD_DIM)[None, :]).reshape(-1)


def _dest_tiles(dest_t, tm):
    n = dest_t.shape[1]
    return dest_t.reshape(TOP_K, n // tm, tm).transpose(1, 0, 2).reshape(n // tm, 1, TOP_K * tm)


def kernel(x_prompt, x_sample, w_in, rpb, q_norm_g, k_norm_g, g_out_na, g_out_gqa, w_o, ln1_g, ln1_b,
           router_w, router_b, w1, b1, w2, b2, ln2_g, ln2_b):
    assert GQA_KV_HEADS == 2 and KV_WIDTH == LANES
    xs_in = [x_prompt, x_sample]
    l = 0
    w_proj = _prep_w_in(w_in[l])
    scale = HEAD_DIM ** -0.5
    gq = jnp.tile(q_norm_g[l] * scale, 2).reshape(1, LANES)
    gk = jnp.tile(k_norm_g[l], 2).reshape(1, LANES)
    bias = _na_bias_tables(rpb[l])
    perm = _gqa_out_perm()
    woa = w_o[l][:NA_WIDTH].astype(BF16)
    wob = w_o[l][NA_WIDTH:][perm].astype(BF16)
    ga = g_out_na[l].reshape(1, NA_WIDTH)
    gb = g_out_gqa[l][perm].reshape(1, GQA_WIDTH)
    ln1g, ln1b = ln1_g[l].reshape(1, D_MODEL), ln1_b[l].reshape(1, D_MODEL)
    ln2g, ln2b = ln2_g[l].reshape(1, D_MODEL), ln2_b[l].reshape(1, D_MODEL)
    rw_t = router_w[l].T.astype(BF16)
    rb = router_b[l].reshape(N_EXPERTS, 1)
    tri = (jnp.arange(MIX_TM)[:, None] < jnp.arange(MIX_TM)[None, :]).astype(BF16)
    w1g = w1[l][:, :, 0::2].astype(BF16)
    w1l = w1[l][:, :, 1::2].astype(BF16)
    w2b = w2[l].astype(BF16)
    b1g = b1[l][:, 0::2].reshape(N_EXPERTS, 1, D_FF)
    b1l = b1[l][:, 1::2].reshape(N_EXPERTS, 1, D_FF)
    b2r = b2[l].reshape(N_EXPERTS, 1, D_MODEL)

    x1s, idxs, gates, ranks = [], [], [], []
    cnt = jnp.zeros((N_EXPERTS, LANES), F32)
    for x in xs_in:
        b, t, _ = x.shape
        x2d = x.reshape(b * t, D_MODEL)
        cos, sin = _rope_tables(t)
        qa, ka, va, qb, kb, vb = _proj(x2d, t, w_proj, cos, sin, gq, gk)
        sh = lambda a: a.reshape(b, t, a.shape[-1])
        oa = _na(sh(qa), sh(ka), sh(va), bias).reshape(b * t, NA_WIDTH)
        ob = _gqa(sh(qb), sh(kb), sh(vb)).reshape(b * t, GQA_WIDTH)
        x1, idx_t, gate_t, rank_t, cnt = _mix(oa, ob, x2d, woa, wob, ga, gb, ln1g, ln1b, rw_t, rb, tri, cnt)
        x1s.append(x1)
        idxs.append(idx_t)
        gates.append(gate_t)
        ranks.append(rank_t)

    counts = cnt[:, 0].astype(jnp.int32)
    nblk = (counts + EXP_BLK - 1) // EXP_BLK
    blk_end = jnp.cumsum(nblk)
    pad_start = (blk_end - nblk) * EXP_BLK
    n_assign = TOP_K * sum(x.shape[0] * x.shape[1] for x in xs_in)
    n_blocks = -(-n_assign // EXP_BLK) + N_EXPERTS
    n_slots = n_blocks * EXP_BLK
    block_expert = jnp.clip(jnp.searchsorted(blk_end, jnp.arange(n_blocks, dtype=jnp.int32), side='right'),
                            0, N_EXPERTS - 1).astype(jnp.int32)
    n_used = blk_end[-1:].astype(jnp.int32)
    tail = ((blk_end - 1) * (EXP_BLK * TOK_ROWS)).astype(jnp.int32)
    has_tail = (counts % EXP_BLK != 0).astype(jnp.int32)
    dests = [(jnp.take(pad_start, idx_t, axis=0).astype(jnp.int32) + rank_t) * TOK_ROWS
             for idx_t, rank_t in zip(idxs, ranks)]

    slots = None
    for x1, dest_t in zip(x1s, dests):
        slots = _dispatch(tail, has_tail, _dest_tiles(dest_t, DISP_TM), x1, slots, n_slots)
    ys = _experts(block_expert, n_used, slots, w1g, w1l, w2b, b1g, b1l, b2r)
    outs = []
    for x, x1, dest_t, gate_t in zip(xs_in, x1s, dests, gates):
        y = _combine(_dest_tiles(dest_t, COMB_TM), x1, gate_t.T, ln2g, ln2b, ys)
        outs.append(y.reshape(x.shape))
    return tuple(outs)
```

```python
import functools

import jax
import jax.numpy as jnp
from jax import lax
from jax.experimental import pallas as pl
from jax.experimental.pallas import tpu as pltpu

D_MODEL = 1024
GRID_W = 64
HEAD_DIM = 64
NA_HEADS = 8
GQA_HEADS = 8
GQA_KV_HEADS = 2
GQA_GROUP = GQA_HEADS // GQA_KV_HEADS
NA_WIDTH = NA_HEADS * HEAD_DIM
GQA_WIDTH = GQA_HEADS * HEAD_DIM
KV_WIDTH = GQA_KV_HEADS * HEAD_DIM
NA_WIN_R = 8
NA_WIN_C = 16
ROPE_AXIS_DIM = HEAD_DIM // 2
ROPE_THETA = 10000.0
N_EXPERTS = 32
TOP_K = 4
D_FF = D_MODEL
SWIGLU_ALPHA = 1.702
SWIGLU_LIMIT = 7.0
DEPTH = 1
DN_ALPHA = (2.0 * DEPTH) ** 0.25
NEG_INF = -1e30
RMS_EPS = 1e-6
LN_EPS = 1e-5

LANES = 128
QEXP_WIDTH = GQA_HEADS * LANES
PROJ_COLS = 3 * NA_WIDTH + QEXP_WIDTH + 2 * KV_WIDTH

PROJ_TM = 512
NA_ROWS = 8
GQA_TQ = 256
MIX_TM = 512
DISP_TM = 256
COMB_TM = 256
EXP_BLK = 512
W1_GROUP = 2 * LANES
SLOT_TM = 2048
VMEM_LIMIT = 56 * 1024 * 1024

F32 = jnp.float32
BF16 = jnp.bfloat16

TOK_ROWS = D_MODEL // LANES


def _tok_load(ref, n, lead=()):
    return jnp.concatenate([ref[lead + (pl.ds(s, n, stride=TOK_ROWS), slice(None))] for s in range(TOK_ROWS)],
                           axis=1)


def _tok_store(ref, val):
    n = val.shape[0]
    for s in range(TOK_ROWS):
        ref[pl.ds(s, n, stride=TOK_ROWS), :] = val[:, s * LANES:(s + 1) * LANES]


def _cparams(sem):
    return pltpu.CompilerParams(dimension_semantics=sem, vmem_limit_bytes=VMEM_LIMIT)


def _rope(y, cos, sin, first_half):
    partner = jnp.where(first_half, pltpu.roll(y, LANES - 16, 1), pltpu.roll(y, 16, 1))
    return y * cos + partner * sin


def _proj_kernel(x_ref, w_ref, cos_ref, sin_ref, gq_ref, gk_ref,
                 qa_ref, ka_ref, va_ref, qb_ref, kb_ref, vb_ref):
    x = x_ref[...].astype(BF16)
    acc = jnp.dot(x, w_ref[...], preferred_element_type=F32)
    qa_ref[...] = acc[:, 0:NA_WIDTH].astype(BF16)
    ka_ref[...] = acc[:, NA_WIDTH:2 * NA_WIDTH].astype(BF16)
    va_ref[...] = acc[:, 2 * NA_WIDTH:3 * NA_WIDTH].astype(BF16)
    cos = cos_ref[...]
    sin = sin_ref[...]
    lane = lax.broadcasted_iota(jnp.int32, cos.shape, 1)
    first_half = (lane % 32) < 16
    base = 3 * NA_WIDTH
    gq = gq_ref[...]
    for h in range(GQA_HEADS):
        y = acc[:, base + h * LANES: base + (h + 1) * LANES]
        ms = jnp.sum(y * y, axis=-1, keepdims=True) * (1.0 / HEAD_DIM)
        y = y * lax.rsqrt(ms + RMS_EPS) * gq
        qb_ref[:, h * LANES:(h + 1) * LANES] = _rope(y, cos, sin, first_half).astype(BF16)
    base += QEXP_WIDTH
    y = acc[:, base: base + KV_WIDTH]
    y2 = y * y
    lo = lane < HEAD_DIM
    ms_lo = jnp.sum(jnp.where(lo, y2, 0.0), axis=-1, keepdims=True)
    ms_hi = jnp.sum(jnp.where(lo, 0.0, y2), axis=-1, keepdims=True)
    ms = jnp.where(lo, ms_lo, ms_hi) * (1.0 / HEAD_DIM)
    y = y * lax.rsqrt(ms + RMS_EPS) * gk_ref[...]
    kb_ref[...] = _rope(y, cos, sin, first_half).astype(BF16)
    base += KV_WIDTH
    vb_ref[...] = acc[:, base: base + KV_WIDTH].astype(BF16)


def _proj(x2d, seq_len, w, cos, sin, gq, gk):
    n = x2d.shape[0]
    tm = PROJ_TM
    assert n % tm == 0 and seq_len % tm == 0
    pos_blocks = seq_len // tm
    row = lambda i: (i, 0)
    const = lambda i: (0, 0)
    outs = [(NA_WIDTH, BF16)] * 3 + [(QEXP_WIDTH, BF16), (KV_WIDTH, BF16), (KV_WIDTH, BF16)]
    return pl.pallas_call(
        _proj_kernel,
        grid=(n // tm,),
        in_specs=[
            pl.BlockSpec((tm, D_MODEL), row),
            pl.BlockSpec((D_MODEL, PROJ_COLS), const),
            pl.BlockSpec((tm, LANES), lambda i: (i % pos_blocks, 0)),
            pl.BlockSpec((tm, LANES), lambda i: (i % pos_blocks, 0)),
            pl.BlockSpec((1, LANES), const),
            pl.BlockSpec((1, LANES), const),
        ],
        out_specs=[pl.BlockSpec((tm, c), row) for c, _ in outs],
        out_shape=[jax.ShapeDtypeStruct((n, c), dt) for c, dt in outs],
        compiler_params=_cparams(("parallel",)),
        name="proj",
    )(x2d, w, cos, sin, gq, gk)


def _na_kernel(q_ref, k_ref, v_ref, bias_ref, o_ref, *, rows):
    j = pl.program_id(1)
    lane = lax.broadcasted_iota(jnp.int32, (GRID_W, LANES), 1)
    lo = lane < HEAD_DIM

    def one_row(rr, carry):
        r = j * NA_ROWS + rr
        rs = jnp.clip(r - NA_WIN_R // 2, 0, rows - NA_WIN_R)
        var = r - rs
        k0 = pl.multiple_of(rs * GRID_W, GRID_W)
        q0 = pl.multiple_of(rr * GRID_W, GRID_W)
        for g in range(NA_HEADS // 2):
            cols = slice(g * LANES, (g + 1) * LANES)
            q = q_ref[0, pl.ds(q0, GRID_W), cols]
            zero = jnp.zeros_like(q)
            q2 = jnp.concatenate([jnp.where(lo, q, zero), jnp.where(lo, zero, q)], axis=0)
            k = k_ref[0, pl.ds(k0, NA_WIN_R * GRID_W), cols]
            v = v_ref[0, pl.ds(k0, NA_WIN_R * GRID_W), cols]
            s = lax.dot_general(q2, k, (((1,), (1,)), ((), ())), preferred_element_type=F32)
            s = s + jnp.concatenate([bias_ref[var, 2 * g], bias_ref[var, 2 * g + 1]], axis=0)
            m = jnp.max(s, axis=-1, keepdims=True)
            p = jnp.exp(s - m)
            l = jnp.sum(p, axis=-1, keepdims=True)
            o2 = jnp.dot(p.astype(BF16), v, preferred_element_type=F32) / l
            o = jnp.where(lo, o2[:GRID_W], o2[GRID_W:])
            o_ref[0, pl.ds(q0, GRID_W), cols] = o.astype(BF16)
        return carry

    lax.fori_loop(0, NA_ROWS, one_row, 0)


def _na(qa, ka, va, bias):
    b, t, _ = qa.shape
    rows = t // GRID_W
    assert rows >= NA_WIN_R and rows % NA_ROWS == 0
    tq = NA_ROWS * GRID_W
    return pl.pallas_call(
        functools.partial(_na_kernel, rows=rows),
        grid=(b, rows // NA_ROWS),
        in_specs=[
            pl.BlockSpec((1, tq, NA_WIDTH), lambda i, j: (i, j, 0)),
            pl.BlockSpec((1, t, NA_WIDTH), lambda i, j: (i, 0, 0)),
            pl.BlockSpec((1, t, NA_WIDTH), lambda i, j: (i, 0, 0)),
            pl.BlockSpec(bias.shape, lambda i, j: (0, 0, 0, 0), pipeline_mode=pl.Buffered(1)),
        ],
        out_specs=pl.BlockSpec((1, tq, NA_WIDTH), lambda i, j: (i, j, 0)),
        out_shape=jax.ShapeDtypeStruct((b, t, NA_WIDTH), BF16),
        compiler_params=_cparams(("parallel", "parallel")),
        name="na",
    )(qa, ka, va, bias)


def _na_bias_tables(rpb):
    off = jnp.arange(NA_WIN_R)
    jrow = jnp.arange(NA_WIN_R)
    dr = jrow[None, :] - off[:, None] + (NA_WIN_R - 1)
    c = jnp.arange(GRID_W)
    cs = jnp.clip(c - NA_WIN_C // 2, 0, GRID_W - NA_WIN_C)
    col_ok = (c[None, :] >= cs[:, None]) & (c[None, :] < cs[:, None] + NA_WIN_C)
    dc = jnp.clip(c[None, :] - c[:, None], -(NA_WIN_C - 1), NA_WIN_C - 1) + (NA_WIN_C - 1)
    sel_r = (dr[:, :, None] == jnp.arange(2 * NA_WIN_R - 1)[None, None, :]).astype(F32)
    sel_c = (dc[:, :, None] == jnp.arange(2 * NA_WIN_C - 1)[None, None, :]).astype(F32)
    bias = jnp.einsum('vja,hab,qkb->hvqjk', sel_r, rpb.astype(F32), sel_c, precision=lax.Precision.HIGHEST)
    bias = jnp.where(col_ok[None, None, :, None, :], bias, NEG_INF)
    return jnp.moveaxis(bias, 0, 1).reshape(NA_WIN_R, NA_HEADS, GRID_W, NA_WIN_R * GRID_W)


def _gqa_kernel(q_ref, k_ref, v_ref, o_ref):
    k = k_ref[0]
    v = v_ref[0]
    lane = lax.broadcasted_iota(jnp.int32, (q_ref.shape[1], LANES), 1)
    lo = lane < HEAD_DIM

    def head(h):
        q = q_ref[0, :, h * LANES:(h + 1) * LANES]
        s = lax.dot_general(q, k, (((1,), (1,)), ((), ())), preferred_element_type=F32)
        m = jnp.max(s, axis=-1, keepdims=True)
        p = jnp.exp(s - m)
        l = jnp.sum(p, axis=-1, keepdims=True)
        return jnp.dot(p.astype(BF16), v, preferred_element_type=F32) / l

    for g in range(GQA_GROUP):
        o = jnp.where(lo, head(g), head(GQA_GROUP + g))
        o_ref[0, :, g * LANES:(g + 1) * LANES] = o.astype(BF16)


def _gqa(qb, kb, vb):
    b, t, _ = qb.shape
    tq = GQA_TQ
    assert t % tq == 0
    return pl.pallas_call(
        _gqa_kernel,
        grid=(b, t // tq),
        in_specs=[
            pl.BlockSpec((1, tq, QEXP_WIDTH), lambda i, j: (i, j, 0)),
            pl.BlockSpec((1, t, KV_WIDTH), lambda i, j: (i, 0, 0)),
            pl.BlockSpec((1, t, KV_WIDTH), lambda i, j: (i, 0, 0)),
        ],
        out_specs=pl.BlockSpec((1, tq, GQA_WIDTH), lambda i, j: (i, j, 0)),
        out_shape=jax.ShapeDtypeStruct((b, t, GQA_WIDTH), BF16),
        compiler_params=_cparams(("parallel", "parallel")),
        name="gqa",
    )(qb, kb, vb)


def _layer_norm(z, g, b):
    mu = jnp.mean(z, axis=-1, keepdims=True)
    zc = z - mu
    var = jnp.mean(zc * zc, axis=-1, keepdims=True)
    return zc * lax.rsqrt(var + LN_EPS) * g + b


def _rms(o, g):
    return o * lax.rsqrt(jnp.mean(o * o, axis=-1, keepdims=True) + RMS_EPS) * g


def _mix_kernel(oa_ref, ob_ref, x_ref, woa_ref, wob_ref, ga_ref, gb_ref, lng_ref, lnb_ref,
                rw_ref, rb_ref, tri_ref, cnt_in_ref,
                x1_ref, idx_ref, gate_ref, rank_ref, cnt_ref, carry_ref):
    @pl.when(pl.program_id(0) == 0)
    def _():
        carry_ref[...] = cnt_in_ref[...]

    na = _rms(oa_ref[...].astype(F32), ga_ref[...]).astype(BF16)
    nb = _rms(ob_ref[...].astype(F32), gb_ref[...]).astype(BF16)
    mixed = (jnp.dot(na, woa_ref[...], preferred_element_type=F32)
             + jnp.dot(nb, wob_ref[...], preferred_element_type=F32))
    x1 = _layer_norm(DN_ALPHA * x_ref[...] + mixed, lng_ref[...], lnb_ref[...])
    _tok_store(x1_ref, x1)

    logits = lax.dot_general(rw_ref[...], x1.astype(BF16), (((1,), (1,)), ((), ())),
                             preferred_element_type=F32) + rb_ref[...]
    tm = logits.shape[1]
    eidx = lax.broadcasted_iota(jnp.int32, (N_EXPERTS, tm), 0).astype(F32)
    work = logits
    vals, idxs, hots = [], [], []
    for _ in range(TOP_K):
        m = jnp.max(work, axis=0, keepdims=True)
        sel = jnp.min(jnp.where(work == m, eidx, float(N_EXPERTS)), axis=0, keepdims=True)
        hot = eidx == sel
        vals.append(m)
        idxs.append(sel)
        hots.append(hot)
        work = jnp.where(hot, -jnp.inf, work)
    es = [jnp.exp(v - vals[0]) for v in vals]
    den = es[0] + es[1] + es[2] + es[3]
    gate_ref[...] = jnp.concatenate([e / den for e in es], axis=0)
    idx_ref[...] = jnp.concatenate(idxs, axis=0).astype(jnp.int32)

    hot_all = hots[0] | hots[1] | hots[2] | hots[3]
    onehot = jnp.where(hot_all, 1.0, 0.0)
    before = jnp.dot(onehot.astype(BF16), tri_ref[...], preferred_element_type=F32)
    before = before + carry_ref[:, 0:1]
    ranks = [jnp.sum(jnp.where(hot, before, 0.0), axis=0, keepdims=True) for hot in hots]
    rank_ref[...] = jnp.concatenate(ranks, axis=0).astype(jnp.int32)
    carry_ref[...] = carry_ref[...] + jnp.sum(onehot, axis=1, keepdims=True)
    cnt_ref[...] = carry_ref[...]


def _mix(oa, ob, x2d, woa, wob, ga, gb, lng, lnb, rw_t, rb, tri, cnt_in):
    n = x2d.shape[0]
    tm = MIX_TM
    assert n % tm == 0
    row = lambda i: (i, 0)
    col = lambda i: (0, i)
    const = lambda i: (0, 0)
    full = lambda a: pl.BlockSpec(a.shape, const)
    return pl.pallas_call(
        _mix_kernel,
        grid=(n // tm,),
        in_specs=[
            pl.BlockSpec((tm, NA_WIDTH), row),
            pl.BlockSpec((tm, GQA_WIDTH), row),
            pl.BlockSpec((tm, D_MODEL), row),
            full(woa), full(wob), full(ga), full(gb), full(lng), full(lnb),
            full(rw_t), full(rb), full(tri), full(cnt_in),
        ],
        out_specs=[
            pl.BlockSpec((tm * TOK_ROWS, LANES), row),
            pl.BlockSpec((TOP_K, tm), col),
            pl.BlockSpec((TOP_K, tm), col),
            pl.BlockSpec((TOP_K, tm), col),
            pl.BlockSpec((N_EXPERTS, LANES), const),
        ],
        out_shape=[
            jax.ShapeDtypeStruct((n * TOK_ROWS, LANES), F32),
            jax.ShapeDtypeStruct((TOP_K, n), jnp.int32),
            jax.ShapeDtypeStruct((TOP_K, n), F32),
            jax.ShapeDtypeStruct((TOP_K, n), jnp.int32),
            jax.ShapeDtypeStruct((N_EXPERTS, LANES), F32),
        ],
        scratch_shapes=[pltpu.VMEM((N_EXPERTS, LANES), F32)],
        compiler_params=_cparams(("arbitrary",)),
        name="mix",
    )(oa, ob, x2d, woa, wob, ga, gb, lng, lnb, rw_t, rb, tri, cnt_in)


def _slot_kernel(pad_ref, idx_ref, rank_ref, o_ref):
    idx = idx_ref[...]
    start = jnp.zeros_like(idx)
    for e in range(N_EXPERTS):
        start = jnp.where(idx == e, pad_ref[e], start)
    o_ref[...] = (start + rank_ref[...]) * TOK_ROWS


def _slots(pad_start, idx_t, rank_t):
    n = idx_t.shape[1]
    tm = min(SLOT_TM, n)
    assert n % tm == 0
    col = lambda i, *_: (0, i)
    return pl.pallas_call(
        _slot_kernel,
        grid_spec=pltpu.PrefetchScalarGridSpec(
            num_scalar_prefetch=1,
            grid=(n // tm,),
            in_specs=[pl.BlockSpec((TOP_K, tm), col), pl.BlockSpec((TOP_K, tm), col)],
            out_specs=pl.BlockSpec((TOP_K, tm), col),
        ),
        out_shape=jax.ShapeDtypeStruct((TOP_K, n), jnp.int32),
        compiler_params=_cparams(("parallel",)),
        name="slots",
    )(pad_start, idx_t, rank_t)


def _dispatch_kernel(tail_ref, has_tail_ref, dest_ref, x_ref, *rest, zero_tails):
    if zero_tails:
        xs_ref, zeros_ref, sem = rest
    else:
        _, xs_ref, sem = rest
    tm = x_ref.shape[0] // TOK_ROWS

    if zero_tails:
        @pl.when(pl.program_id(0) == 0)
        def _():
            zeros_ref[...] = jnp.zeros_like(zeros_ref)

            def tail_copy(e):
                start = pl.multiple_of(tail_ref[e], EXP_BLK * TOK_ROWS)
                return pltpu.make_async_copy(zeros_ref, xs_ref.at[pl.ds(start, EXP_BLK * TOK_ROWS), :], sem)

            for e in range(N_EXPERTS):
                @pl.when(has_tail_ref[e] != 0)
                def _():
                    tail_copy(e).start()
            for e in range(N_EXPERTS):
                @pl.when(has_tail_ref[e] != 0)
                def _():
                    tail_copy(e).wait()

    def row_copy(i, k):
        src = pl.multiple_of(i * TOK_ROWS, TOK_ROWS)
        dst = pl.multiple_of(dest_ref[0, 0, k * tm + i], TOK_ROWS)
        return pltpu.make_async_copy(x_ref.at[pl.ds(src, TOK_ROWS), :], xs_ref.at[pl.ds(dst, TOK_ROWS), :], sem)

    def issue(i, c):
        for k in range(TOP_K):
            row_copy(i, k).start()
        return c

    def drain(i, c):
        for k in range(TOP_K):
            row_copy(i, k).wait()
        return c

    lax.fori_loop(0, tm, issue, 0)
    lax.fori_loop(0, tm, drain, 0)


def _dispatch(tail, has_tail, dest_tiles, x1, xs_prev, n_slots):
    n = x1.shape[0] // TOK_ROWS
    tm = DISP_TM
    assert n % tm == 0
    zero_tails = xs_prev is None
    in_specs = [
        pl.BlockSpec((1, 1, TOP_K * tm), lambda i, *_: (i, 0, 0), memory_space=pltpu.SMEM),
        pl.BlockSpec((tm * TOK_ROWS, LANES), lambda i, *_: (i, 0)),
    ]
    args = [dest_tiles, x1]
    scratch = []
    aliases = {}
    if zero_tails:
        scratch.append(pltpu.VMEM((EXP_BLK * TOK_ROWS, LANES), F32))
    else:
        in_specs.append(pl.BlockSpec(memory_space=pl.ANY))
        args.append(xs_prev)
        aliases = {4: 0}
    scratch.append(pltpu.SemaphoreType.DMA(()))
    return pl.pallas_call(
        functools.partial(_dispatch_kernel, zero_tails=zero_tails),
        grid_spec=pltpu.PrefetchScalarGridSpec(
            num_scalar_prefetch=2,
            grid=(n // tm,),
            in_specs=in_specs,
            out_specs=pl.BlockSpec(memory_space=pl.ANY),
            scratch_shapes=scratch,
        ),
        out_shape=jax.ShapeDtypeStruct((n_slots * TOK_ROWS, LANES), F32),
        input_output_aliases=aliases,
        compiler_params=pltpu.CompilerParams(dimension_semantics=("arbitrary",), vmem_limit_bytes=VMEM_LIMIT,
                                             has_side_effects=True),
        name="dispatch",
    )(tail, has_tail, *args)


def _w1_prep_kernel(w_ref, perm_ref, o_ref):
    for c in range(w_ref.shape[2] // W1_GROUP):
        cols = slice(c * W1_GROUP, (c + 1) * W1_GROUP)
        w = w_ref[0, :, cols].astype(BF16)
        o_ref[0, :, cols] = jnp.dot(w, perm_ref[...], preferred_element_type=F32).astype(BF16)


def _w1_prep(w1):
    e, d, f2 = w1.shape
    half = f2 // 2
    j = jnp.arange(W1_GROUP)
    dst = jnp.where(j % 2 == 0, j // 2, LANES + j // 2)
    perm = (dst[:, None] == jnp.arange(W1_GROUP)[None, :]).astype(BF16)
    return pl.pallas_call(
        _w1_prep_kernel,
        grid=(e, 2),
        in_specs=[pl.BlockSpec((1, d, half), lambda i, j: (i, 0, j)),
                  pl.BlockSpec((W1_GROUP, W1_GROUP), lambda i, j: (0, 0))],
        out_specs=pl.BlockSpec((1, d, half), lambda i, j: (i, 0, j)),
        out_shape=jax.ShapeDtypeStruct((e, d, f2), BF16),
        compiler_params=_cparams(("parallel", "parallel")),
        name="w1prep",
    )(w1, perm)


def _expert_kernel(be_ref, nused_ref, xs_ref, w1_ref, w2_ref, b1_ref, b2_ref, o_ref):
    @pl.when(pl.program_id(0) < nused_ref[0])
    def _():
        x = _tok_load(xs_ref, EXP_BLK).astype(BF16)
        h = jnp.dot(x, w1_ref[0], preferred_element_type=F32) + b1_ref[0]
        groups = range(h.shape[1] // W1_GROUP)
        glu = jnp.concatenate([h[:, c * W1_GROUP: c * W1_GROUP + LANES] for c in groups], axis=1)
        lin = jnp.concatenate([h[:, c * W1_GROUP + LANES: (c + 1) * W1_GROUP] for c in groups], axis=1)
        glu = jnp.minimum(glu, SWIGLU_LIMIT)
        lin = jnp.clip(lin, -SWIGLU_LIMIT, SWIGLU_LIMIT)
        act = glu * jax.nn.sigmoid(SWIGLU_ALPHA * glu) * (lin + 1.0)
        _tok_store(o_ref, jnp.dot(act.astype(BF16), w2_ref[0], preferred_element_type=F32) + b2_ref[0])


def _experts(block_expert, n_used, xs, w1p, w2, b1p, b2):
    n_slots = xs.shape[0] // TOK_ROWS
    n_blocks = n_slots // EXP_BLK
    slot = lambda i, be, nu: (jnp.minimum(i, nu[0] - 1), 0)
    wsel = lambda i, be, nu: (be[i], 0, 0)
    return pl.pallas_call(
        _expert_kernel,
        grid_spec=pltpu.PrefetchScalarGridSpec(
            num_scalar_prefetch=2,
            grid=(n_blocks,),
            in_specs=[
                pl.BlockSpec((EXP_BLK * TOK_ROWS, LANES), slot),
                pl.BlockSpec((1, D_MODEL, 2 * D_FF), wsel),
                pl.BlockSpec((1, D_FF, D_MODEL), wsel),
                pl.BlockSpec((1, 1, 2 * D_FF), wsel),
                pl.BlockSpec((1, 1, D_MODEL), wsel),
            ],
            out_specs=pl.BlockSpec((EXP_BLK * TOK_ROWS, LANES), slot),
        ),
        out_shape=jax.ShapeDtypeStruct((n_slots * TOK_ROWS, LANES), F32),
        compiler_params=_cparams(("arbitrary",)),
        name="experts",
    )(block_expert, n_used, xs, w1p, w2, b1p, b2)


def _combine_kernel(dest_ref, x1_ref, gate_ref, lng_ref, lnb_ref, ys_ref, o_ref, buf_ref, sem):
    tm = x1_ref.shape[0] // TOK_ROWS

    def row_copy(i, k):
        src = pl.multiple_of(dest_ref[0, 0, k * tm + i], TOK_ROWS)
        dst = pl.multiple_of(i * TOK_ROWS, TOK_ROWS)
        return pltpu.make_async_copy(ys_ref.at[pl.ds(src, TOK_ROWS), :], buf_ref.at[k, pl.ds(dst, TOK_ROWS), :], sem)

    def issue(i, c):
        for k in range(TOP_K):
            row_copy(i, k).start()
        return c

    def drain(i, c):
        for k in range(TOP_K):
            row_copy(i, k).wait()
        return c

    lax.fori_loop(0, tm, issue, 0)
    lax.fori_loop(0, tm, drain, 0)
    gate = gate_ref[...]
    y = _tok_load(buf_ref, tm, (0,)) * gate[:, 0:1]
    for k in range(1, TOP_K):
        y = y + _tok_load(buf_ref, tm, (k,)) * gate[:, k:k + 1]
    o_ref[...] = _layer_norm(DN_ALPHA * _tok_load(x1_ref, tm) + y, lng_ref[...], lnb_ref[...])


def _combine(dest_tiles, x1, gate, lng, lnb, ys):
    n = x1.shape[0] // TOK_ROWS
    tm = COMB_TM
    assert n % tm == 0
    return pl.pallas_call(
        _combine_kernel,
        grid=(n // tm,),
        in_specs=[
            pl.BlockSpec((1, 1, TOP_K * tm), lambda i: (i, 0, 0), memory_space=pltpu.SMEM),
            pl.BlockSpec((tm * TOK_ROWS, LANES), lambda i: (i, 0)),
            pl.BlockSpec((tm, TOP_K), lambda i: (i, 0)),
            pl.BlockSpec((1, D_MODEL), lambda i: (0, 0)),
            pl.BlockSpec((1, D_MODEL), lambda i: (0, 0)),
            pl.BlockSpec(memory_space=pl.ANY),
        ],
        out_specs=pl.BlockSpec((tm, D_MODEL), lambda i: (i, 0)),
        out_shape=jax.ShapeDtypeStruct((n, D_MODEL), F32),
        scratch_shapes=[pltpu.VMEM((TOP_K, tm * TOK_ROWS, LANES), F32), pltpu.SemaphoreType.DMA(())],
        compiler_params=_cparams(("arbitrary",)),
        name="combine",
    )(dest_tiles, x1, gate, lng, lnb, ys)


def _rope_tables(t):
    pos = jnp.arange(t)
    row = (pos // GRID_W).astype(F32)
    col = (pos % GRID_W).astype(F32)
    inv = ROPE_THETA ** (-jnp.arange(0, ROPE_AXIS_DIM, 2, dtype=F32) / ROPE_AXIS_DIM)
    ar = row[:, None] * inv[None, :]
    ac = col[:, None] * inv[None, :]
    cos = jnp.concatenate([jnp.cos(ar), jnp.cos(ar), jnp.cos(ac), jnp.cos(ac)], axis=1)
    sin = jnp.concatenate([-jnp.sin(ar), jnp.sin(ar), -jnp.sin(ac), jnp.sin(ac)], axis=1)
    return jnp.tile(cos, (1, 2)), jnp.tile(sin, (1, 2))


def _prep_w_in(w):
    scale = HEAD_DIM ** -0.5
    qa = w[:, :NA_WIDTH] * scale
    kva = w[:, NA_WIDTH:3 * NA_WIDTH]
    qb = w[:, 3 * NA_WIDTH:3 * NA_WIDTH + GQA_WIDTH].reshape(D_MODEL, GQA_HEADS, HEAD_DIM)
    rest = w[:, 3 * NA_WIDTH + GQA_WIDTH:]
    kv_of_head = (jnp.arange(GQA_HEADS) // GQA_GROUP)[None, :, None]
    zero = jnp.zeros_like(qb)
    qexp = jnp.concatenate([jnp.where(kv_of_head == 0, qb, zero), jnp.where(kv_of_head == 1, qb, zero)], axis=-1)
    return jnp.concatenate([qa, kva, qexp.reshape(D_MODEL, QEXP_WIDTH), rest], axis=1).astype(BF16)


def _gqa_out_order(a):
    rest = a.shape[1:]
    a = a.reshape((GQA_KV_HEADS, GQA_GROUP, HEAD_DIM) + rest)
    return jnp.swapaxes(a, 0, 1).reshape((GQA_WIDTH,) + rest)


def _dest_tiles(dest_t, tm):
    n = dest_t.shape[1]
    return dest_t.reshape(TOP_K, n // tm, tm).transpose(1, 0, 2).reshape(n // tm, 1, TOP_K * tm)


def kernel(x_prompt, x_sample, w_in, rpb, q_norm_g, k_norm_g, g_out_na, g_out_gqa, w_o, ln1_g, ln1_b,
           router_w, router_b, w1, b1, w2, b2, ln2_g, ln2_b):
    assert GQA_KV_HEADS == 2 and KV_WIDTH == LANES
    xs_in = [x_prompt, x_sample]
    l = 0
    w_proj = _prep_w_in(w_in[l])
    scale = HEAD_DIM ** -0.5
    gq = jnp.tile(q_norm_g[l] * scale, 2).reshape(1, LANES)
    gk = jnp.tile(k_norm_g[l], 2).reshape(1, LANES)
    bias = _na_bias_tables(rpb[l])
    woa = w_o[l][:NA_WIDTH].astype(BF16)
    wob = _gqa_out_order(w_o[l][NA_WIDTH:]).astype(BF16)
    ga = g_out_na[l].reshape(1, NA_WIDTH)
    gb = _gqa_out_order(g_out_gqa[l]).reshape(1, GQA_WIDTH)
    ln1g, ln1b = ln1_g[l].reshape(1, D_MODEL), ln1_b[l].reshape(1, D_MODEL)
    ln2g, ln2b = ln2_g[l].reshape(1, D_MODEL), ln2_b[l].reshape(1, D_MODEL)
    rw_t = router_w[l].T.astype(BF16)
    rb = router_b[l].reshape(N_EXPERTS, 1)
    tri = (jnp.arange(MIX_TM)[:, None] < jnp.arange(MIX_TM)[None, :]).astype(BF16)
    w1p = _w1_prep(w1[l])
    w2b = w2[l].astype(BF16)
    b1p = jnp.swapaxes(b1[l].reshape(N_EXPERTS, 2 * D_FF // W1_GROUP, LANES, 2), 2, 3).reshape(N_EXPERTS, 1, 2 * D_FF)
    b2r = b2[l].reshape(N_EXPERTS, 1, D_MODEL)

    x1s, idxs, gates, ranks = [], [], [], []
    cnt = jnp.zeros((N_EXPERTS, LANES), F32)
    for x in xs_in:
        b, t, _ = x.shape
        x2d = x.reshape(b * t, D_MODEL)
        cos, sin = _rope_tables(t)
        qa, ka, va, qb, kb, vb = _proj(x2d, t, w_proj, cos, sin, gq, gk)
        sh = lambda a: a.reshape(b, t, a.shape[-1])
        oa = _na(sh(qa), sh(ka), sh(va), bias).reshape(b * t, NA_WIDTH)
        ob = _gqa(sh(qb), sh(kb), sh(vb)).reshape(b * t, GQA_WIDTH)
        x1, idx_t, gate_t, rank_t, cnt = _mix(oa, ob, x2d, woa, wob, ga, gb, ln1g, ln1b, rw_t, rb, tri, cnt)
        x1s.append(x1)
        idxs.append(idx_t)
        gates.append(gate_t)
        ranks.append(rank_t)

    counts = cnt[:, 0].astype(jnp.int32)
    nblk = (counts + EXP_BLK - 1) // EXP_BLK
    blk_end = jnp.cumsum(nblk)
    pad_start = (blk_end - nblk) * EXP_BLK
    n_assign = TOP_K * sum(x.shape[0] * x.shape[1] for x in xs_in)
    n_blocks = -(-n_assign // EXP_BLK) + N_EXPERTS
    n_slots = n_blocks * EXP_BLK
    blocks = jnp.arange(n_blocks, dtype=jnp.int32)
    block_expert = jnp.minimum(jnp.sum((blk_end[None, :] <= blocks[:, None]).astype(jnp.int32), axis=1),
                               N_EXPERTS - 1)
    n_used = blk_end[-1:].astype(jnp.int32)
    tail = ((blk_end - 1) * (EXP_BLK * TOK_ROWS)).astype(jnp.int32)
    has_tail = (counts % EXP_BLK != 0).astype(jnp.int32)
    dests = [_slots(pad_start.astype(jnp.int32), idx_t, rank_t) for idx_t, rank_t in zip(idxs, ranks)]

    slots = None
    for x1, dest_t in zip(x1s, dests):
        slots = _dispatch(tail, has_tail, _dest_tiles(dest_t, DISP_TM), x1, slots, n_slots)
    ys = _experts(block_expert, n_used, slots, w1p, w2b, b1p, b2r)
    outs = []
    for x, x1, dest_t, gate_t in zip(xs_in, x1s, dests, gates):
        y = _combine(_dest_tiles(dest_t, COMB_TM), x1, gate_t.T, ln2g, ln2b, ys)
        outs.append(y.reshape(x.shape))
    return tuple(outs)
```

```python
import functools

import jax
import jax.numpy as jnp
from jax import lax
from jax.experimental import pallas as pl
from jax.experimental.pallas import tpu as pltpu

D_MODEL = 1024
GRID_W = 64
HEAD_DIM = 64
NA_HEADS = 8
GQA_HEADS = 8
GQA_KV_HEADS = 2
GQA_GROUP = GQA_HEADS // GQA_KV_HEADS
NA_WIDTH = NA_HEADS * HEAD_DIM
GQA_WIDTH = GQA_HEADS * HEAD_DIM
KV_WIDTH = GQA_KV_HEADS * HEAD_DIM
NA_WIN_R = 8
NA_WIN_C = 16
ROPE_AXIS_DIM = HEAD_DIM // 2
ROPE_THETA = 10000.0
N_EXPERTS = 32
TOP_K = 4
D_FF = D_MODEL
SWIGLU_ALPHA = 1.702
SWIGLU_LIMIT = 7.0
DEPTH = 1
DN_ALPHA = (2.0 * DEPTH) ** 0.25
NEG_INF = -1e30
RMS_EPS = 1e-6
LN_EPS = 1e-5

LANES = 128
PROJ_COLS = 3 * NA_WIDTH + GQA_WIDTH + 2 * KV_WIDTH

PROJ_TM = 512
NA_ROWS = 8
NA_PASS_HEADS = 4
GQA_TQ = 256
MIX_TM = 512
DISP_TM = 256
COMB_TM = 256
EXP_BLK = 512
W1_GROUP = 2 * LANES
SLOT_TM = 2048
VMEM_LIMIT = 56 * 1024 * 1024

F32 = jnp.float32
BF16 = jnp.bfloat16

TOK_ROWS = D_MODEL // LANES


def _tok_load(ref, n, lead=()):
    return jnp.concatenate([ref[lead + (pl.ds(s, n, stride=TOK_ROWS), slice(None))] for s in range(TOK_ROWS)],
                           axis=1)


def _tok_store(ref, val):
    n = val.shape[0]
    for s in range(TOK_ROWS):
        ref[pl.ds(s, n, stride=TOK_ROWS), :] = val[:, s * LANES:(s + 1) * LANES]


def _cparams(sem):
    return pltpu.CompilerParams(dimension_semantics=sem, vmem_limit_bytes=VMEM_LIMIT)


def _rope(y, cos, sin, first_half):
    partner = jnp.where(first_half, pltpu.roll(y, LANES - 16, 1), pltpu.roll(y, 16, 1))
    return y * cos + partner * sin


def _proj_kernel(x_ref, w_ref, cos_ref, sin_ref, gq_ref, gk_ref,
                 qa_ref, ka_ref, va_ref, qb_ref, kb_ref, vb_ref):
    x = x_ref[...].astype(BF16)

    def cols(c0, width):
        return jnp.dot(x, w_ref[:, c0:c0 + width], preferred_element_type=F32)

    qa_ref[...] = cols(0, NA_WIDTH).astype(BF16)
    ka_ref[...] = cols(NA_WIDTH, NA_WIDTH).astype(BF16)
    va_ref[...] = cols(2 * NA_WIDTH, NA_WIDTH).astype(BF16)
    cos = cos_ref[...]
    sin = sin_ref[...]
    lane = lax.broadcasted_iota(jnp.int32, cos.shape, 1)
    first_half = (lane % 32) < 16
    lo = lane < HEAD_DIM

    def norm_rope(y, g):
        y2 = y * y
        ms_lo = jnp.sum(jnp.where(lo, y2, 0.0), axis=-1, keepdims=True)
        ms_hi = jnp.sum(jnp.where(lo, 0.0, y2), axis=-1, keepdims=True)
        ms = jnp.where(lo, ms_lo, ms_hi) * (1.0 / HEAD_DIM)
        return _rope(y * lax.rsqrt(ms + RMS_EPS) * g, cos, sin, first_half).astype(BF16)

    base = 3 * NA_WIDTH
    q = cols(base, GQA_WIDTH)
    gq = gq_ref[...]
    for g in range(GQA_GROUP):
        qb_ref[:, g * LANES:(g + 1) * LANES] = norm_rope(q[:, g * LANES:(g + 1) * LANES], gq)
    base += GQA_WIDTH
    kv = cols(base, 2 * KV_WIDTH)
    kb_ref[...] = norm_rope(kv[:, :KV_WIDTH], gk_ref[...])
    vb_ref[...] = kv[:, KV_WIDTH:].astype(BF16)


def _proj(x2d, seq_len, w, cos, sin, gq, gk):
    n = x2d.shape[0]
    tm = PROJ_TM
    assert n % tm == 0 and seq_len % tm == 0
    pos_blocks = seq_len // tm
    row = lambda i: (i, 0)
    const = lambda i: (0, 0)
    outs = [(NA_WIDTH, BF16)] * 3 + [(GQA_WIDTH, BF16), (KV_WIDTH, BF16), (KV_WIDTH, BF16)]
    return pl.pallas_call(
        _proj_kernel,
        grid=(n // tm,),
        in_specs=[
            pl.BlockSpec((tm, D_MODEL), row),
            pl.BlockSpec((D_MODEL, PROJ_COLS), const),
            pl.BlockSpec((tm, LANES), lambda i: (i % pos_blocks, 0)),
            pl.BlockSpec((tm, LANES), lambda i: (i % pos_blocks, 0)),
            pl.BlockSpec((1, LANES), const),
            pl.BlockSpec((1, LANES), const),
        ],
        out_specs=[pl.BlockSpec((tm, c), row) for c, _ in outs],
        out_shape=[jax.ShapeDtypeStruct((n, c), dt) for c, dt in outs],
        compiler_params=_cparams(("parallel",)),
        name="proj",
    )(x2d, w, cos, sin, gq, gk)


def _na_kernel(q_ref, k_ref, v_ref, bias_ref, o_ref, *, rows):
    j = pl.program_id(1)
    width = NA_PASS_HEADS * HEAD_DIM
    lane_head = lax.broadcasted_iota(jnp.int32, (GRID_W, width), 1) // HEAD_DIM

    def one_row(rr, carry):
        r = j * NA_ROWS + rr
        rs = jnp.clip(r - NA_WIN_R // 2, 0, rows - NA_WIN_R)
        var = r - rs
        k0 = pl.multiple_of(rs * GRID_W, GRID_W)
        q0 = pl.multiple_of(rr * GRID_W, GRID_W)
        for g in range(NA_HEADS // NA_PASS_HEADS):
            cols = slice(g * width, (g + 1) * width)
            heads = range(g * NA_PASS_HEADS, (g + 1) * NA_PASS_HEADS)
            q = q_ref[0, pl.ds(q0, GRID_W), cols]
            zero = jnp.zeros_like(q)
            qs = jnp.concatenate([jnp.where(lane_head == a, q, zero) for a in range(NA_PASS_HEADS)], axis=0)
            k = k_ref[0, pl.ds(k0, NA_WIN_R * GRID_W), cols]
            v = v_ref[0, pl.ds(k0, NA_WIN_R * GRID_W), cols]
            s = lax.dot_general(qs, k, (((1,), (1,)), ((), ())), preferred_element_type=F32)
            s = s + jnp.concatenate([bias_ref[var, h] for h in heads], axis=0)
            m = jnp.max(s, axis=-1, keepdims=True)
            p = jnp.exp(s - m)
            l = jnp.sum(p, axis=-1, keepdims=True)
            os = jnp.dot(p.astype(BF16), v, preferred_element_type=F32) / l
            o = os[:GRID_W]
            for a in range(1, NA_PASS_HEADS):
                o = jnp.where(lane_head == a, os[a * GRID_W:(a + 1) * GRID_W], o)
            o_ref[0, pl.ds(q0, GRID_W), cols] = o.astype(BF16)
        return carry

    lax.fori_loop(0, NA_ROWS, one_row, 0, unroll=2)


def _na(qa, ka, va, bias):
    b, t, _ = qa.shape
    rows = t // GRID_W
    assert rows >= NA_WIN_R and rows % NA_ROWS == 0
    tq = NA_ROWS * GRID_W
    return pl.pallas_call(
        functools.partial(_na_kernel, rows=rows),
        grid=(b, rows // NA_ROWS),
        in_specs=[
            pl.BlockSpec((1, tq, NA_WIDTH), lambda i, j: (i, j, 0)),
            pl.BlockSpec((1, t, NA_WIDTH), lambda i, j: (i, 0, 0)),
            pl.BlockSpec((1, t, NA_WIDTH), lambda i, j: (i, 0, 0)),
            pl.BlockSpec(bias.shape, lambda i, j: (0, 0, 0, 0), pipeline_mode=pl.Buffered(1)),
        ],
        out_specs=pl.BlockSpec((1, tq, NA_WIDTH), lambda i, j: (i, j, 0)),
        out_shape=jax.ShapeDtypeStruct((b, t, NA_WIDTH), BF16),
        compiler_params=_cparams(("parallel", "parallel")),
        name="na",
    )(qa, ka, va, bias)


def _na_bias_tables(rpb):
    off = jnp.arange(NA_WIN_R)
    jrow = jnp.arange(NA_WIN_R)
    dr = jrow[None, :] - off[:, None] + (NA_WIN_R - 1)
    c = jnp.arange(GRID_W)
    cs = jnp.clip(c - NA_WIN_C // 2, 0, GRID_W - NA_WIN_C)
    col_ok = (c[None, :] >= cs[:, None]) & (c[None, :] < cs[:, None] + NA_WIN_C)
    dc = jnp.clip(c[None, :] - c[:, None], -(NA_WIN_C - 1), NA_WIN_C - 1) + (NA_WIN_C - 1)
    sel_r = (dr[:, :, None] == jnp.arange(2 * NA_WIN_R - 1)[None, None, :]).astype(F32)
    sel_c = (dc[:, :, None] == jnp.arange(2 * NA_WIN_C - 1)[None, None, :]).astype(F32)
    bias = jnp.einsum('vja,hab,qkb->hvqjk', sel_r, rpb.astype(F32), sel_c, precision=lax.Precision.HIGHEST)
    bias = jnp.where(col_ok[None, None, :, None, :], bias, NEG_INF)
    return jnp.moveaxis(bias, 0, 1).reshape(NA_WIN_R, NA_HEADS, GRID_W, NA_WIN_R * GRID_W)


def _gqa_kernel(q_ref, k_ref, v_ref, o_ref):
    k = k_ref[0]
    v = v_ref[0]
    lane = lax.broadcasted_iota(jnp.int32, (q_ref.shape[1], LANES), 1)
    lo = lane < HEAD_DIM

    def head(q):
        s = lax.dot_general(q, k, (((1,), (1,)), ((), ())), preferred_element_type=F32)
        m = jnp.max(s, axis=-1, keepdims=True)
        p = jnp.exp(s - m)
        l = jnp.sum(p, axis=-1, keepdims=True)
        return jnp.dot(p.astype(BF16), v, preferred_element_type=F32) / l

    for g in range(GQA_GROUP):
        q = q_ref[0, :, g * LANES:(g + 1) * LANES]
        zero = jnp.zeros_like(q)
        o = jnp.where(lo, head(jnp.where(lo, q, zero)), head(jnp.where(lo, zero, q)))
        o_ref[0, :, g * LANES:(g + 1) * LANES] = o.astype(BF16)


def _gqa(qb, kb, vb):
    b, t, _ = qb.shape
    tq = GQA_TQ
    assert t % tq == 0
    return pl.pallas_call(
        _gqa_kernel,
        grid=(b, t // tq),
        in_specs=[
            pl.BlockSpec((1, tq, GQA_WIDTH), lambda i, j: (i, j, 0)),
            pl.BlockSpec((1, t, KV_WIDTH), lambda i, j: (i, 0, 0)),
            pl.BlockSpec((1, t, KV_WIDTH), lambda i, j: (i, 0, 0)),
        ],
        out_specs=pl.BlockSpec((1, tq, GQA_WIDTH), lambda i, j: (i, j, 0)),
        out_shape=jax.ShapeDtypeStruct((b, t, GQA_WIDTH), BF16),
        compiler_params=_cparams(("parallel", "parallel")),
        name="gqa",
    )(qb, kb, vb)


def _layer_norm(z, g, b):
    mu = jnp.mean(z, axis=-1, keepdims=True)
    zc = z - mu
    var = jnp.mean(zc * zc, axis=-1, keepdims=True)
    return zc * lax.rsqrt(var + LN_EPS) * g + b


def _rms(o, g):
    return o * lax.rsqrt(jnp.mean(o * o, axis=-1, keepdims=True) + RMS_EPS) * g


def _mix_kernel(oa_ref, ob_ref, x_ref, woa_ref, wob_ref, ga_ref, gb_ref, lng_ref, lnb_ref,
                rw_ref, rb_ref, tri_ref, cnt_in_ref,
                x1_ref, idx_ref, gate_ref, rank_ref, cnt_ref, carry_ref):
    @pl.when(pl.program_id(0) == 0)
    def _():
        carry_ref[...] = cnt_in_ref[...]

    na = _rms(oa_ref[...].astype(F32), ga_ref[...]).astype(BF16)
    nb = _rms(ob_ref[...].astype(F32), gb_ref[...]).astype(BF16)
    mixed = (jnp.dot(na, woa_ref[...], preferred_element_type=F32)
             + jnp.dot(nb, wob_ref[...], preferred_element_type=F32))
    x1 = _layer_norm(DN_ALPHA * x_ref[...] + mixed, lng_ref[...], lnb_ref[...])
    _tok_store(x1_ref, x1)

    logits = lax.dot_general(rw_ref[...], x1.astype(BF16), (((1,), (1,)), ((), ())),
                             preferred_element_type=F32) + rb_ref[...]
    tm = logits.shape[1]
    eidx = lax.broadcasted_iota(jnp.int32, (N_EXPERTS, tm), 0).astype(F32)
    work = logits
    vals, idxs, hots = [], [], []
    for _ in range(TOP_K):
        m = jnp.max(work, axis=0, keepdims=True)
        sel = jnp.min(jnp.where(work == m, eidx, float(N_EXPERTS)), axis=0, keepdims=True)
        hot = eidx == sel
        vals.append(m)
        idxs.append(sel)
        hots.append(hot)
        work = jnp.where(hot, -jnp.inf, work)
    es = [jnp.exp(v - vals[0]) for v in vals]
    den = es[0] + es[1] + es[2] + es[3]
    gate_ref[...] = jnp.concatenate([e / den for e in es], axis=0)
    idx_ref[...] = jnp.concatenate(idxs, axis=0).astype(jnp.int32)

    hot_all = hots[0] | hots[1] | hots[2] | hots[3]
    onehot = jnp.where(hot_all, 1.0, 0.0)
    before = jnp.dot(onehot.astype(BF16), tri_ref[...], preferred_element_type=F32)
    before = before + carry_ref[:, 0:1]
    ranks = [jnp.sum(jnp.where(hot, before, 0.0), axis=0, keepdims=True) for hot in hots]
    rank_ref[...] = jnp.concatenate(ranks, axis=0).astype(jnp.int32)
    carry_ref[...] = carry_ref[...] + jnp.sum(onehot, axis=1, keepdims=True)
    cnt_ref[...] = carry_ref[...]


def _mix(oa, ob, x2d, woa, wob, ga, gb, lng, lnb, rw_t, rb, tri, cnt_in):
    n = x2d.shape[0]
    tm = MIX_TM
    assert n % tm == 0
    row = lambda i: (i, 0)
    col = lambda i: (0, i)
    const = lambda i: (0, 0)
    full = lambda a: pl.BlockSpec(a.shape, const)
    return pl.pallas_call(
        _mix_kernel,
        grid=(n // tm,),
        in_specs=[
            pl.BlockSpec((tm, NA_WIDTH), row),
            pl.BlockSpec((tm, GQA_WIDTH), row),
            pl.BlockSpec((tm, D_MODEL), row),
            full(woa), full(wob), full(ga), full(gb), full(lng), full(lnb),
            full(rw_t), full(rb), full(tri), full(cnt_in),
        ],
        out_specs=[
            pl.BlockSpec((tm * TOK_ROWS, LANES), row),
            pl.BlockSpec((TOP_K, tm), col),
            pl.BlockSpec((TOP_K, tm), col),
            pl.BlockSpec((TOP_K, tm), col),
            pl.BlockSpec((N_EXPERTS, LANES), const),
        ],
        out_shape=[
            jax.ShapeDtypeStruct((n * TOK_ROWS, LANES), F32),
            jax.ShapeDtypeStruct((TOP_K, n), jnp.int32),
            jax.ShapeDtypeStruct((TOP_K, n), F32),
            jax.ShapeDtypeStruct((TOP_K, n), jnp.int32),
            jax.ShapeDtypeStruct((N_EXPERTS, LANES), F32),
        ],
        scratch_shapes=[pltpu.VMEM((N_EXPERTS, LANES), F32)],
        compiler_params=_cparams(("arbitrary",)),
        name="mix",
    )(oa, ob, x2d, woa, wob, ga, gb, lng, lnb, rw_t, rb, tri, cnt_in)


def _slot_kernel(pad_ref, idx_ref, rank_ref, o_ref):
    idx = idx_ref[...]
    start = jnp.zeros_like(idx)
    for e in range(N_EXPERTS):
        start = jnp.where(idx == e, pad_ref[e], start)
    o_ref[...] = (start + rank_ref[...]) * TOK_ROWS


def _slots(pad_start, idx_t, rank_t):
    n = idx_t.shape[1]
    tm = min(SLOT_TM, n)
    assert n % tm == 0
    col = lambda i, *_: (0, i)
    return pl.pallas_call(
        _slot_kernel,
        grid_spec=pltpu.PrefetchScalarGridSpec(
            num_scalar_prefetch=1,
            grid=(n // tm,),
            in_specs=[pl.BlockSpec((TOP_K, tm), col), pl.BlockSpec((TOP_K, tm), col)],
            out_specs=pl.BlockSpec((TOP_K, tm), col),
        ),
        out_shape=jax.ShapeDtypeStruct((TOP_K, n), jnp.int32),
        compiler_params=_cparams(("parallel",)),
        name="slots",
    )(pad_start, idx_t, rank_t)


def _dispatch_kernel(tail_ref, has_tail_ref, dest_ref, x_ref, *rest, zero_tails):
    if zero_tails:
        xs_ref, zeros_ref, sem = rest
    else:
        _, xs_ref, sem = rest
    tm = x_ref.shape[0] // TOK_ROWS

    if zero_tails:
        @pl.when(pl.program_id(0) == 0)
        def _():
            zeros_ref[...] = jnp.zeros_like(zeros_ref)

            def tail_copy(e):
                start = pl.multiple_of(tail_ref[e], EXP_BLK * TOK_ROWS)
                return pltpu.make_async_copy(zeros_ref, xs_ref.at[pl.ds(start, EXP_BLK * TOK_ROWS), :], sem)

            for e in range(N_EXPERTS):
                @pl.when(has_tail_ref[e] != 0)
                def _():
                    tail_copy(e).start()
            for e in range(N_EXPERTS):
                @pl.when(has_tail_ref[e] != 0)
                def _():
                    tail_copy(e).wait()

    def row_copy(i, k):
        src = pl.multiple_of(i * TOK_ROWS, TOK_ROWS)
        dst = pl.multiple_of(dest_ref[0, 0, k * tm + i], TOK_ROWS)
        return pltpu.make_async_copy(x_ref.at[pl.ds(src, TOK_ROWS), :], xs_ref.at[pl.ds(dst, TOK_ROWS), :], sem)

    def issue(i, c):
        for k in range(TOP_K):
            row_copy(i, k).start()
        return c

    lax.fori_loop(0, tm, issue, 0, unroll=4)
    for k in range(TOP_K):
        pltpu.make_async_copy(x_ref, xs_ref.at[pl.ds(0, tm * TOK_ROWS), :], sem).wait()


def _dispatch(tail, has_tail, dest_tiles, x1, xs_prev, n_slots):
    n = x1.shape[0] // TOK_ROWS
    tm = DISP_TM
    assert n % tm == 0
    zero_tails = xs_prev is None
    in_specs = [
        pl.BlockSpec((1, 1, TOP_K * tm), lambda i, *_: (i, 0, 0), memory_space=pltpu.SMEM),
        pl.BlockSpec((tm * TOK_ROWS, LANES), lambda i, *_: (i, 0)),
    ]
    args = [dest_tiles, x1]
    scratch = []
    aliases = {}
    if zero_tails:
        scratch.append(pltpu.VMEM((EXP_BLK * TOK_ROWS, LANES), F32))
    else:
        in_specs.append(pl.BlockSpec(memory_space=pl.ANY))
        args.append(xs_prev)
        aliases = {4: 0}
    scratch.append(pltpu.SemaphoreType.DMA(()))
    return pl.pallas_call(
        functools.partial(_dispatch_kernel, zero_tails=zero_tails),
        grid_spec=pltpu.PrefetchScalarGridSpec(
            num_scalar_prefetch=2,
            grid=(n // tm,),
            in_specs=in_specs,
            out_specs=pl.BlockSpec(memory_space=pl.ANY),
            scratch_shapes=scratch,
        ),
        out_shape=jax.ShapeDtypeStruct((n_slots * TOK_ROWS, LANES), F32),
        input_output_aliases=aliases,
        compiler_params=pltpu.CompilerParams(dimension_semantics=("arbitrary",), vmem_limit_bytes=VMEM_LIMIT,
                                             has_side_effects=True),
        name="dispatch",
    )(tail, has_tail, *args)


def _w1_prep_kernel(w_ref, perm_ref, o_ref):
    for c in range(w_ref.shape[2] // W1_GROUP):
        cols = slice(c * W1_GROUP, (c + 1) * W1_GROUP)
        w = w_ref[0, :, cols].astype(BF16)
        o_ref[0, :, cols] = jnp.dot(w, perm_ref[...], preferred_element_type=F32).astype(BF16)


def _w1_prep(w1):
    e, d, f2 = w1.shape
    half = f2 // 2
    j = jnp.arange(W1_GROUP)
    dst = jnp.where(j % 2 == 0, j // 2, LANES + j // 2)
    perm = (dst[:, None] == jnp.arange(W1_GROUP)[None, :]).astype(BF16)
    return pl.pallas_call(
        _w1_prep_kernel,
        grid=(e, 2),
        in_specs=[pl.BlockSpec((1, d, half), lambda i, j: (i, 0, j)),
                  pl.BlockSpec((W1_GROUP, W1_GROUP), lambda i, j: (0, 0))],
        out_specs=pl.BlockSpec((1, d, half), lambda i, j: (i, 0, j)),
        out_shape=jax.ShapeDtypeStruct((e, d, f2), BF16),
        compiler_params=_cparams(("parallel", "parallel")),
        name="w1prep",
    )(w1, perm)


def _expert_kernel(be_ref, nused_ref, xs_ref, w1_ref, w2_ref, b1_ref, b2_ref, o_ref):
    @pl.when(pl.program_id(0) < nused_ref[0])
    def _():
        x = _tok_load(xs_ref, EXP_BLK).astype(BF16)
        h = jnp.dot(x, w1_ref[0], preferred_element_type=F32) + b1_ref[0]
        groups = range(h.shape[1] // W1_GROUP)
        glu = jnp.concatenate([h[:, c * W1_GROUP: c * W1_GROUP + LANES] for c in groups], axis=1)
        lin = jnp.concatenate([h[:, c * W1_GROUP + LANES: (c + 1) * W1_GROUP] for c in groups], axis=1)
        glu = jnp.minimum(glu, SWIGLU_LIMIT)
        lin = jnp.clip(lin, -SWIGLU_LIMIT, SWIGLU_LIMIT)
        act = glu * jax.nn.sigmoid(SWIGLU_ALPHA * glu) * (lin + 1.0)
        _tok_store(o_ref, jnp.dot(act.astype(BF16), w2_ref[0], preferred_element_type=F32) + b2_ref[0])


def _experts(block_expert, n_used, xs, w1p, w2, b1p, b2):
    n_slots = xs.shape[0] // TOK_ROWS
    n_blocks = n_slots // EXP_BLK
    slot = lambda i, be, nu: (jnp.minimum(i, nu[0] - 1), 0)
    wsel = lambda i, be, nu: (be[i], 0, 0)
    return pl.pallas_call(
        _expert_kernel,
        grid_spec=pltpu.PrefetchScalarGridSpec(
            num_scalar_prefetch=2,
            grid=(n_blocks,),
            in_specs=[
                pl.BlockSpec((EXP_BLK * TOK_ROWS, LANES), slot),
                pl.BlockSpec((1, D_MODEL, 2 * D_FF), wsel),
                pl.BlockSpec((1, D_FF, D_MODEL), wsel),
                pl.BlockSpec((1, 1, 2 * D_FF), wsel),
                pl.BlockSpec((1, 1, D_MODEL), wsel),
            ],
            out_specs=pl.BlockSpec((EXP_BLK * TOK_ROWS, LANES), slot),
        ),
        out_shape=jax.ShapeDtypeStruct((n_slots * TOK_ROWS, LANES), F32),
        compiler_params=_cparams(("arbitrary",)),
        name="experts",
    )(block_expert, n_used, xs, w1p, w2, b1p, b2)


def _combine_kernel(dest_ref, dest_next_ref, x1_ref, gate_ref, lng_ref, lnb_ref, ys_ref, o_ref, buf_ref, sems):
    tm = x1_ref.shape[0] // TOK_ROWS
    step = pl.program_id(0)
    slot = step % 2

    def gather(idx_ref, into):
        def issue(i, c):
            for k in range(TOP_K):
                src = pl.multiple_of(idx_ref[0, 0, k * tm + i], TOK_ROWS)
                dst = pl.multiple_of(i * TOK_ROWS, TOK_ROWS)
                pltpu.make_async_copy(ys_ref.at[pl.ds(src, TOK_ROWS), :],
                                      buf_ref.at[into, k, pl.ds(dst, TOK_ROWS), :], sems.at[into]).start()
            return c

        lax.fori_loop(0, tm, issue, 0, unroll=4)

    @pl.when(step == 0)
    def _():
        gather(dest_ref, slot)

    @pl.when(step + 1 < pl.num_programs(0))
    def _():
        gather(dest_next_ref, 1 - slot)

    for k in range(TOP_K):
        pltpu.make_async_copy(ys_ref.at[pl.ds(0, tm * TOK_ROWS), :], buf_ref.at[slot, k], sems.at[slot]).wait()
    gate = gate_ref[...]
    y = _tok_load(buf_ref, tm, (slot, 0)) * gate[:, 0:1]
    for k in range(1, TOP_K):
        y = y + _tok_load(buf_ref, tm, (slot, k)) * gate[:, k:k + 1]
    o_ref[...] = _layer_norm(DN_ALPHA * _tok_load(x1_ref, tm) + y, lng_ref[...], lnb_ref[...])


def _combine(dest_tiles, x1, gate, lng, lnb, ys):
    n = x1.shape[0] // TOK_ROWS
    tm = COMB_TM
    assert n % tm == 0
    n_tiles = n // tm
    return pl.pallas_call(
        _combine_kernel,
        grid=(n_tiles,),
        in_specs=[
            pl.BlockSpec((1, 1, TOP_K * tm), lambda i: (i, 0, 0), memory_space=pltpu.SMEM),
            pl.BlockSpec((1, 1, TOP_K * tm), lambda i: (jnp.minimum(i + 1, n_tiles - 1), 0, 0),
                         memory_space=pltpu.SMEM),
            pl.BlockSpec((tm * TOK_ROWS, LANES), lambda i: (i, 0)),
            pl.BlockSpec((tm, TOP_K), lambda i: (i, 0)),
            pl.BlockSpec((1, D_MODEL), lambda i: (0, 0)),
            pl.BlockSpec((1, D_MODEL), lambda i: (0, 0)),
            pl.BlockSpec(memory_space=pl.ANY),
        ],
        out_specs=pl.BlockSpec((tm, D_MODEL), lambda i: (i, 0)),
        out_shape=jax.ShapeDtypeStruct((n, D_MODEL), F32),
        scratch_shapes=[pltpu.VMEM((2, TOP_K, tm * TOK_ROWS, LANES), F32), pltpu.SemaphoreType.DMA((2,))],
        compiler_params=_cparams(("arbitrary",)),
        name="combine",
    )(dest_tiles, dest_tiles, x1, gate, lng, lnb, ys)


def _rope_tables(t):
    pos = jnp.arange(t)
    row = (pos // GRID_W).astype(F32)
    col = (pos % GRID_W).astype(F32)
    inv = ROPE_THETA ** (-jnp.arange(0, ROPE_AXIS_DIM, 2, dtype=F32) / ROPE_AXIS_DIM)
    ar = row[:, None] * inv[None, :]
    ac = col[:, None] * inv[None, :]
    cos = jnp.concatenate([jnp.cos(ar), jnp.cos(ar), jnp.cos(ac), jnp.cos(ac)], axis=1)
    sin = jnp.concatenate([-jnp.sin(ar), jnp.sin(ar), -jnp.sin(ac), jnp.sin(ac)], axis=1)
    return jnp.tile(cos, (1, 2)), jnp.tile(sin, (1, 2))


def _prep_w_in(w):
    scale = HEAD_DIM ** -0.5
    qa = w[:, :NA_WIDTH] * scale
    kva = w[:, NA_WIDTH:3 * NA_WIDTH]
    qb = _gqa_out_order(w[:, 3 * NA_WIDTH:3 * NA_WIDTH + GQA_WIDTH].T).T
    rest = w[:, 3 * NA_WIDTH + GQA_WIDTH:]
    return jnp.concatenate([qa, kva, qb, rest], axis=1).astype(BF16)


def _gqa_out_order(a):
    rest = a.shape[1:]
    a = a.reshape((GQA_KV_HEADS, GQA_GROUP, HEAD_DIM) + rest)
    return jnp.swapaxes(a, 0, 1).reshape((GQA_WIDTH,) + rest)


def _dest_tiles(dest_t, tm):
    n = dest_t.shape[1]
    return dest_t.reshape(TOP_K, n // tm, tm).transpose(1, 0, 2).reshape(n // tm, 1, TOP_K * tm)


def kernel(x_prompt, x_sample, w_in, rpb, q_norm_g, k_norm_g, g_out_na, g_out_gqa, w_o, ln1_g, ln1_b,
           router_w, router_b, w1, b1, w2, b2, ln2_g, ln2_b):
    assert GQA_KV_HEADS == 2 and KV_WIDTH == LANES
    xs_in = [x_prompt, x_sample]
    l = 0
    w_proj = _prep_w_in(w_in[l])
    scale = HEAD_DIM ** -0.5
    gq = jnp.tile(q_norm_g[l] * scale, 2).reshape(1, LANES)
    gk = jnp.tile(k_norm_g[l], 2).reshape(1, LANES)
    bias = _na_bias_tables(rpb[l])
    woa = w_o[l][:NA_WIDTH].astype(BF16)
    wob = _gqa_out_order(w_o[l][NA_WIDTH:]).astype(BF16)
    ga = g_out_na[l].reshape(1, NA_WIDTH)
    gb = _gqa_out_order(g_out_gqa[l]).reshape(1, GQA_WIDTH)
    ln1g, ln1b = ln1_g[l].reshape(1, D_MODEL), ln1_b[l].reshape(1, D_MODEL)
    ln2g, ln2b = ln2_g[l].reshape(1, D_MODEL), ln2_b[l].reshape(1, D_MODEL)
    rw_t = router_w[l].T.astype(BF16)
    rb = router_b[l].reshape(N_EXPERTS, 1)
    tri = (jnp.arange(MIX_TM)[:, None] < jnp.arange(MIX_TM)[None, :]).astype(BF16)
    w1p = _w1_prep(w1[l])
    w2b = w2[l].astype(BF16)
    b1p = jnp.swapaxes(b1[l].reshape(N_EXPERTS, 2 * D_FF // W1_GROUP, LANES, 2), 2, 3).reshape(N_EXPERTS, 1, 2 * D_FF)
    b2r = b2[l].reshape(N_EXPERTS, 1, D_MODEL)

    x1s, idxs, gates, ranks = [], [], [], []
    cnt = jnp.zeros((N_EXPERTS, LANES), F32)
    for x in xs_in:
        b, t, _ = x.shape
        x2d = x.reshape(b * t, D_MODEL)
        cos, sin = _rope_tables(t)
        qa, ka, va, qb, kb, vb = _proj(x2d, t, w_proj, cos, sin, gq, gk)
        sh = lambda a: a.reshape(b, t, a.shape[-1])
        oa = _na(sh(qa), sh(ka), sh(va), bias).reshape(b * t, NA_WIDTH)
        ob = _gqa(sh(qb), sh(kb), sh(vb)).reshape(b * t, GQA_WIDTH)
        x1, idx_t, gate_t, rank_t, cnt = _mix(oa, ob, x2d, woa, wob, ga, gb, ln1g, ln1b, rw_t, rb, tri, cnt)
        x1s.append(x1)
        idxs.append(idx_t)
        gates.append(gate_t)
        ranks.append(rank_t)

    counts = cnt[:, 0].astype(jnp.int32)
    nblk = (counts + EXP_BLK - 1) // EXP_BLK
    blk_end = jnp.cumsum(nblk)
    pad_start = (blk_end - nblk) * EXP_BLK
    n_assign = TOP_K * sum(x.shape[0] * x.shape[1] for x in xs_in)
    n_blocks = -(-n_assign // EXP_BLK) + N_EXPERTS
    n_slots = n_blocks * EXP_BLK
    blocks = jnp.arange(n_blocks, dtype=jnp.int32)
    block_expert = jnp.minimum(jnp.sum((blk_end[None, :] <= blocks[:, None]).astype(jnp.int32), axis=1),
                               N_EXPERTS - 1)
    n_used = blk_end[-1:].astype(jnp.int32)
    tail = ((blk_end - 1) * (EXP_BLK * TOK_ROWS)).astype(jnp.int32)
    has_tail = (counts % EXP_BLK != 0).astype(jnp.int32)
    dests = [_slots(pad_start.astype(jnp.int32), idx_t, rank_t) for idx_t, rank_t in zip(idxs, ranks)]

    slots = None
    for x1, dest_t in zip(x1s, dests):
        slots = _dispatch(tail, has_tail, _dest_tiles(dest_t, DISP_TM), x1, slots, n_slots)
    ys = _experts(block_expert, n_used, slots, w1p, w2b, b1p, b2r)
    outs = []
    for x, x1, dest_t, gate_t in zip(xs_in, x1s, dests, gates):
        y = _combine(_dest_tiles(dest_t, COMB_TM), x1, gate_t.T, ln2g, ln2b, ys)
        outs.append(y.reshape(x.shape))
    return tuple(outs)
```

```python
import functools

import jax
import jax.numpy as jnp
from jax import lax
from jax.experimental import pallas as pl
from jax.experimental.pallas import tpu as pltpu

D_MODEL = 1024
GRID_W = 64
HEAD_DIM = 64
NA_HEADS = 8
GQA_HEADS = 8
GQA_KV_HEADS = 2
GQA_GROUP = GQA_HEADS // GQA_KV_HEADS
NA_WIDTH = NA_HEADS * HEAD_DIM
GQA_WIDTH = GQA_HEADS * HEAD_DIM
KV_WIDTH = GQA_KV_HEADS * HEAD_DIM
NA_WIN_R = 8
NA_WIN_C = 16
ROPE_AXIS_DIM = HEAD_DIM // 2
ROPE_THETA = 10000.0
N_EXPERTS = 32
TOP_K = 4
D_FF = D_MODEL
SWIGLU_ALPHA = 1.702
SWIGLU_LIMIT = 7.0
DEPTH = 1
DN_ALPHA = (2.0 * DEPTH) ** 0.25
NEG_INF = -1e30
RMS_EPS = 1e-6
LN_EPS = 1e-5

LANES = 128
PROJ_COLS = 3 * NA_WIDTH + GQA_WIDTH + 2 * KV_WIDTH

PROJ_TM = 512
NA_ROWS = 8
NA_PASS_HEADS = 4
GQA_SCORE_ELEMS = 512 * 2048
MIX_TM = 512
DISP_TM = 256
COMB_TM = 256
EXP_BLK = 512
W1_GROUP = 2 * LANES
SLOT_TM = 2048
VMEM_LIMIT = 56 * 1024 * 1024

F32 = jnp.float32
BF16 = jnp.bfloat16

TOK_ROWS = D_MODEL // LANES


def _tok_load(ref, n, lead=()):
    return jnp.concatenate([ref[lead + (pl.ds(s, n, stride=TOK_ROWS), slice(None))] for s in range(TOK_ROWS)],
                           axis=1)


def _tok_store(ref, val):
    n = val.shape[0]
    for s in range(TOK_ROWS):
        ref[pl.ds(s, n, stride=TOK_ROWS), :] = val[:, s * LANES:(s + 1) * LANES]


def _cparams(sem):
    return pltpu.CompilerParams(dimension_semantics=sem, vmem_limit_bytes=VMEM_LIMIT)


def _rope(y, cos, sin, first_half):
    partner = jnp.where(first_half, pltpu.roll(y, LANES - 16, 1), pltpu.roll(y, 16, 1))
    return y * cos + partner * sin


def _proj_kernel(x_ref, w_ref, cos_ref, sin_ref, gq_ref, gk_ref,
                 qa_ref, ka_ref, va_ref, qb_ref, kb_ref, vb_ref):
    x = x_ref[...].astype(BF16)

    def cols(c0, width):
        return jnp.dot(x, w_ref[:, c0:c0 + width], preferred_element_type=F32)

    qa_ref[...] = cols(0, NA_WIDTH).astype(BF16)
    ka_ref[...] = cols(NA_WIDTH, NA_WIDTH).astype(BF16)
    va_ref[...] = cols(2 * NA_WIDTH, NA_WIDTH).astype(BF16)
    cos = cos_ref[...]
    sin = sin_ref[...]
    lane = lax.broadcasted_iota(jnp.int32, cos.shape, 1)
    first_half = (lane % 32) < 16
    lo = lane < HEAD_DIM

    def norm_rope(y, g):
        y2 = y * y
        ms_lo = jnp.sum(jnp.where(lo, y2, 0.0), axis=-1, keepdims=True)
        ms_hi = jnp.sum(jnp.where(lo, 0.0, y2), axis=-1, keepdims=True)
        ms = jnp.where(lo, ms_lo, ms_hi) * (1.0 / HEAD_DIM)
        return _rope(y * lax.rsqrt(ms + RMS_EPS) * g, cos, sin, first_half).astype(BF16)

    base = 3 * NA_WIDTH
    q = cols(base, GQA_WIDTH)
    gq = gq_ref[...]
    for g in range(GQA_GROUP):
        qb_ref[:, g * LANES:(g + 1) * LANES] = norm_rope(q[:, g * LANES:(g + 1) * LANES], gq)
    base += GQA_WIDTH
    kv = cols(base, 2 * KV_WIDTH)
    kb_ref[...] = norm_rope(kv[:, :KV_WIDTH], gk_ref[...])
    vb_ref[...] = kv[:, KV_WIDTH:].astype(BF16)


def _proj(x2d, seq_len, w, cos, sin, gq, gk):
    n = x2d.shape[0]
    tm = PROJ_TM
    assert n % tm == 0 and seq_len % tm == 0
    pos_blocks = seq_len // tm
    row = lambda i: (i, 0)
    const = lambda i: (0, 0)
    outs = [(NA_WIDTH, BF16)] * 3 + [(GQA_WIDTH, BF16), (KV_WIDTH, BF16), (KV_WIDTH, BF16)]
    return pl.pallas_call(
        _proj_kernel,
        grid=(n // tm,),
        in_specs=[
            pl.BlockSpec((tm, D_MODEL), row),
            pl.BlockSpec((D_MODEL, PROJ_COLS), const),
            pl.BlockSpec((tm, LANES), lambda i: (i % pos_blocks, 0)),
            pl.BlockSpec((tm, LANES), lambda i: (i % pos_blocks, 0)),
            pl.BlockSpec((1, LANES), const),
            pl.BlockSpec((1, LANES), const),
        ],
        out_specs=[pl.BlockSpec((tm, c), row) for c, _ in outs],
        out_shape=[jax.ShapeDtypeStruct((n, c), dt) for c, dt in outs],
        compiler_params=_cparams(("parallel",)),
        name="proj",
    )(x2d, w, cos, sin, gq, gk)


def _na_kernel(q_ref, k_ref, v_ref, bias_ref, o_ref, *, rows):
    j = pl.program_id(1)
    width = NA_PASS_HEADS * HEAD_DIM
    lane_head = lax.broadcasted_iota(jnp.int32, (GRID_W, width), 1) // HEAD_DIM

    def one_row(rr, carry):
        r = j * NA_ROWS + rr
        rs = jnp.clip(r - NA_WIN_R // 2, 0, rows - NA_WIN_R)
        var = r - rs
        k0 = pl.multiple_of(rs * GRID_W, GRID_W)
        q0 = pl.multiple_of(rr * GRID_W, GRID_W)
        for g in range(NA_HEADS // NA_PASS_HEADS):
            cols = slice(g * width, (g + 1) * width)
            heads = range(g * NA_PASS_HEADS, (g + 1) * NA_PASS_HEADS)
            q = q_ref[0, pl.ds(q0, GRID_W), cols]
            zero = jnp.zeros_like(q)
            qs = jnp.concatenate([jnp.where(lane_head == a, q, zero) for a in range(NA_PASS_HEADS)], axis=0)
            k = k_ref[0, pl.ds(k0, NA_WIN_R * GRID_W), cols]
            v = v_ref[0, pl.ds(k0, NA_WIN_R * GRID_W), cols]
            s = lax.dot_general(qs, k, (((1,), (1,)), ((), ())), preferred_element_type=F32)
            s = s + jnp.concatenate([bias_ref[var, h] for h in heads], axis=0)
            m = jnp.max(s, axis=-1, keepdims=True)
            p = jnp.exp(s - m)
            l = jnp.sum(p, axis=-1, keepdims=True)
            os = jnp.dot(p.astype(BF16), v, preferred_element_type=F32) / l
            o = os[:GRID_W]
            for a in range(1, NA_PASS_HEADS):
                o = jnp.where(lane_head == a, os[a * GRID_W:(a + 1) * GRID_W], o)
            o_ref[0, pl.ds(q0, GRID_W), cols] = o.astype(BF16)
        return carry

    lax.fori_loop(0, NA_ROWS, one_row, 0, unroll=True)


def _na(qa, ka, va, bias):
    b, t, _ = qa.shape
    rows = t // GRID_W
    assert rows >= NA_WIN_R and rows % NA_ROWS == 0
    tq = NA_ROWS * GRID_W
    return pl.pallas_call(
        functools.partial(_na_kernel, rows=rows),
        grid=(b, rows // NA_ROWS),
        in_specs=[
            pl.BlockSpec((1, tq, NA_WIDTH), lambda i, j: (i, j, 0)),
            pl.BlockSpec((1, t, NA_WIDTH), lambda i, j: (i, 0, 0)),
            pl.BlockSpec((1, t, NA_WIDTH), lambda i, j: (i, 0, 0)),
            pl.BlockSpec(bias.shape, lambda i, j: (0, 0, 0, 0), pipeline_mode=pl.Buffered(1)),
        ],
        out_specs=pl.BlockSpec((1, tq, NA_WIDTH), lambda i, j: (i, j, 0)),
        out_shape=jax.ShapeDtypeStruct((b, t, NA_WIDTH), BF16),
        compiler_params=_cparams(("parallel", "parallel")),
        name="na",
    )(qa, ka, va, bias)


def _na_bias_tables(rpb):
    off = jnp.arange(NA_WIN_R)
    jrow = jnp.arange(NA_WIN_R)
    dr = jrow[None, :] - off[:, None] + (NA_WIN_R - 1)
    c = jnp.arange(GRID_W)
    cs = jnp.clip(c - NA_WIN_C // 2, 0, GRID_W - NA_WIN_C)
    col_ok = (c[None, :] >= cs[:, None]) & (c[None, :] < cs[:, None] + NA_WIN_C)
    dc = jnp.clip(c[None, :] - c[:, None], -(NA_WIN_C - 1), NA_WIN_C - 1) + (NA_WIN_C - 1)
    sel_r = (dr[:, :, None] == jnp.arange(2 * NA_WIN_R - 1)[None, None, :]).astype(F32)
    sel_c = (dc[:, :, None] == jnp.arange(2 * NA_WIN_C - 1)[None, None, :]).astype(F32)
    bias = jnp.einsum('vja,hab,qkb->hvqjk', sel_r, rpb.astype(F32), sel_c, precision=lax.Precision.HIGHEST)
    bias = jnp.where(col_ok[None, None, :, None, :], bias, NEG_INF)
    return jnp.moveaxis(bias, 0, 1).reshape(NA_WIN_R, NA_HEADS, GRID_W, NA_WIN_R * GRID_W)


def _gqa_kernel(q_ref, k_ref, v_ref, o_ref):
    k = k_ref[0]
    v = v_ref[0]
    lane = lax.broadcasted_iota(jnp.int32, (q_ref.shape[1], LANES), 1)
    lo = lane < HEAD_DIM

    def head(q):
        s = lax.dot_general(q, k, (((1,), (1,)), ((), ())), preferred_element_type=F32)
        m = jnp.max(s, axis=-1, keepdims=True)
        p = jnp.exp(s - m)
        l = jnp.sum(p, axis=-1, keepdims=True)
        return jnp.dot(p.astype(BF16), v, preferred_element_type=F32) / l

    for g in range(GQA_GROUP):
        q = q_ref[0, :, g * LANES:(g + 1) * LANES]
        zero = jnp.zeros_like(q)
        o = jnp.where(lo, head(jnp.where(lo, q, zero)), head(jnp.where(lo, zero, q)))
        o_ref[0, :, g * LANES:(g + 1) * LANES] = o.astype(BF16)


def _gqa(qb, kb, vb):
    b, t, _ = qb.shape
    tq = min(t, GQA_SCORE_ELEMS // t)
    assert t % tq == 0 and tq % 8 == 0
    return pl.pallas_call(
        _gqa_kernel,
        grid=(b, t // tq),
        in_specs=[
            pl.BlockSpec((1, tq, GQA_WIDTH), lambda i, j: (i, j, 0)),
            pl.BlockSpec((1, t, KV_WIDTH), lambda i, j: (i, 0, 0)),
            pl.BlockSpec((1, t, KV_WIDTH), lambda i, j: (i, 0, 0)),
        ],
        out_specs=pl.BlockSpec((1, tq, GQA_WIDTH), lambda i, j: (i, j, 0)),
        out_shape=jax.ShapeDtypeStruct((b, t, GQA_WIDTH), BF16),
        compiler_params=_cparams(("parallel", "parallel")),
        name="gqa",
    )(qb, kb, vb)


def _layer_norm(z, g, b):
    mu = jnp.mean(z, axis=-1, keepdims=True)
    zc = z - mu
    var = jnp.mean(zc * zc, axis=-1, keepdims=True)
    return zc * lax.rsqrt(var + LN_EPS) * g + b


def _rms(o, g):
    return o * lax.rsqrt(jnp.mean(o * o, axis=-1, keepdims=True) + RMS_EPS) * g


def _mix_kernel(oa_ref, ob_ref, x_ref, woa_ref, wob_ref, ga_ref, gb_ref, lng_ref, lnb_ref,
                rw_ref, rb_ref, tri_ref, cnt_in_ref,
                x1_ref, idx_ref, gate_ref, rank_ref, cnt_ref, carry_ref):
    @pl.when(pl.program_id(0) == 0)
    def _():
        carry_ref[...] = cnt_in_ref[...]

    na = _rms(oa_ref[...].astype(F32), ga_ref[...]).astype(BF16)
    nb = _rms(ob_ref[...].astype(F32), gb_ref[...]).astype(BF16)
    mixed = (jnp.dot(na, woa_ref[...], preferred_element_type=F32)
             + jnp.dot(nb, wob_ref[...], preferred_element_type=F32))
    x1 = _layer_norm(DN_ALPHA * x_ref[...] + mixed, lng_ref[...], lnb_ref[...])
    _tok_store(x1_ref, x1)

    logits = lax.dot_general(rw_ref[...], x1.astype(BF16), (((1,), (1,)), ((), ())),
                             preferred_element_type=F32) + rb_ref[...]
    tm = logits.shape[1]
    eidx = lax.broadcasted_iota(jnp.int32, (N_EXPERTS, tm), 0).astype(F32)
    work = logits
    vals, idxs, hots = [], [], []
    for _ in range(TOP_K):
        m = jnp.max(work, axis=0, keepdims=True)
        sel = jnp.min(jnp.where(work == m, eidx, float(N_EXPERTS)), axis=0, keepdims=True)
        hot = eidx == sel
        vals.append(m)
        idxs.append(sel)
        hots.append(hot)
        work = jnp.where(hot, -jnp.inf, work)
    es = [jnp.exp(v - vals[0]) for v in vals]
    den = es[0] + es[1] + es[2] + es[3]
    gate_ref[...] = jnp.concatenate([e / den for e in es], axis=0)
    idx_ref[...] = jnp.concatenate(idxs, axis=0).astype(jnp.int32)

    hot_all = hots[0] | hots[1] | hots[2] | hots[3]
    onehot = jnp.where(hot_all, 1.0, 0.0)
    before = jnp.dot(onehot.astype(BF16), tri_ref[...], preferred_element_type=F32)
    before = before + carry_ref[:, 0:1]
    ranks = [jnp.sum(jnp.where(hot, before, 0.0), axis=0, keepdims=True) for hot in hots]
    rank_ref[...] = jnp.concatenate(ranks, axis=0).astype(jnp.int32)
    carry_ref[...] = carry_ref[...] + jnp.sum(onehot, axis=1, keepdims=True)
    cnt_ref[...] = carry_ref[...]


def _mix(oa, ob, x2d, woa, wob, ga, gb, lng, lnb, rw_t, rb, tri, cnt_in):
    n = x2d.shape[0]
    tm = MIX_TM
    assert n % tm == 0
    row = lambda i: (i, 0)
    col = lambda i: (0, i)
    const = lambda i: (0, 0)
    full = lambda a: pl.BlockSpec(a.shape, const)
    return pl.pallas_call(
        _mix_kernel,
        grid=(n // tm,),
        in_specs=[
            pl.BlockSpec((tm, NA_WIDTH), row),
            pl.BlockSpec((tm, GQA_WIDTH), row),
            pl.BlockSpec((tm, D_MODEL), row),
            full(woa), full(wob), full(ga), full(gb), full(lng), full(lnb),
            full(rw_t), full(rb), full(tri), full(cnt_in),
        ],
        out_specs=[
            pl.BlockSpec((tm * TOK_ROWS, LANES), row),
            pl.BlockSpec((TOP_K, tm), col),
            pl.BlockSpec((TOP_K, tm), col),
            pl.BlockSpec((TOP_K, tm), col),
            pl.BlockSpec((N_EXPERTS, LANES), const),
        ],
        out_shape=[
            jax.ShapeDtypeStruct((n * TOK_ROWS, LANES), F32),
            jax.ShapeDtypeStruct((TOP_K, n), jnp.int32),
            jax.ShapeDtypeStruct((TOP_K, n), F32),
            jax.ShapeDtypeStruct((TOP_K, n), jnp.int32),
            jax.ShapeDtypeStruct((N_EXPERTS, LANES), F32),
        ],
        scratch_shapes=[pltpu.VMEM((N_EXPERTS, LANES), F32)],
        compiler_params=_cparams(("arbitrary",)),
        name="mix",
    )(oa, ob, x2d, woa, wob, ga, gb, lng, lnb, rw_t, rb, tri, cnt_in)


def _slot_kernel(pad_ref, idx_ref, rank_ref, o_ref):
    idx = idx_ref[...]
    start = jnp.zeros_like(idx)
    for e in range(N_EXPERTS):
        start = jnp.where(idx == e, pad_ref[e], start)
    o_ref[...] = (start + rank_ref[...]) * TOK_ROWS


def _slots(pad_start, idx_t, rank_t):
    n = idx_t.shape[1]
    tm = min(SLOT_TM, n)
    assert n % tm == 0
    col = lambda i, *_: (0, i)
    return pl.pallas_call(
        _slot_kernel,
        grid_spec=pltpu.PrefetchScalarGridSpec(
            num_scalar_prefetch=1,
            grid=(n // tm,),
            in_specs=[pl.BlockSpec((TOP_K, tm), col), pl.BlockSpec((TOP_K, tm), col)],
            out_specs=pl.BlockSpec((TOP_K, tm), col),
        ),
        out_shape=jax.ShapeDtypeStruct((TOP_K, n), jnp.int32),
        compiler_params=_cparams(("parallel",)),
        name="slots",
    )(pad_start, idx_t, rank_t)


def _dispatch_kernel(tail_ref, has_tail_ref, dest_ref, x_ref, *rest, zero_tails):
    if zero_tails:
        xs_ref, zeros_ref, sem = rest
    else:
        _, xs_ref, sem = rest
    tm = x_ref.shape[0] // TOK_ROWS

    if zero_tails:
        @pl.when(pl.program_id(0) == 0)
        def _():
            zeros_ref[...] = jnp.zeros_like(zeros_ref)

            def tail_copy(e):
                start = pl.multiple_of(tail_ref[e], EXP_BLK * TOK_ROWS)
                return pltpu.make_async_copy(zeros_ref, xs_ref.at[pl.ds(start, EXP_BLK * TOK_ROWS), :], sem)

            for e in range(N_EXPERTS):
                @pl.when(has_tail_ref[e] != 0)
                def _():
                    tail_copy(e).start()
            for e in range(N_EXPERTS):
                @pl.when(has_tail_ref[e] != 0)
                def _():
                    tail_copy(e).wait()

    def row_copy(i, k):
        src = pl.multiple_of(i * TOK_ROWS, TOK_ROWS)
        dst = pl.multiple_of(dest_ref[0, 0, k * tm + i], TOK_ROWS)
        return pltpu.make_async_copy(x_ref.at[pl.ds(src, TOK_ROWS), :], xs_ref.at[pl.ds(dst, TOK_ROWS), :], sem)

    def issue(i, c):
        for k in range(TOP_K):
            row_copy(i, k).start(priority=k % 2)
        return c

    lax.fori_loop(0, tm, issue, 0, unroll=4)
    for k in range(TOP_K):
        pltpu.make_async_copy(x_ref, xs_ref.at[pl.ds(0, tm * TOK_ROWS), :], sem).wait()


def _dispatch(tail, has_tail, dest_tiles, x1, xs_prev, n_slots):
    n = x1.shape[0] // TOK_ROWS
    tm = DISP_TM
    assert n % tm == 0
    zero_tails = xs_prev is None
    in_specs = [
        pl.BlockSpec((1, 1, TOP_K * tm), lambda i, *_: (i, 0, 0), memory_space=pltpu.SMEM),
        pl.BlockSpec((tm * TOK_ROWS, LANES), lambda i, *_: (i, 0)),
    ]
    args = [dest_tiles, x1]
    scratch = []
    aliases = {}
    if zero_tails:
        scratch.append(pltpu.VMEM((EXP_BLK * TOK_ROWS, LANES), F32))
    else:
        in_specs.append(pl.BlockSpec(memory_space=pl.ANY))
        args.append(xs_prev)
        aliases = {4: 0}
    scratch.append(pltpu.SemaphoreType.DMA(()))
    return pl.pallas_call(
        functools.partial(_dispatch_kernel, zero_tails=zero_tails),
        grid_spec=pltpu.PrefetchScalarGridSpec(
            num_scalar_prefetch=2,
            grid=(n // tm,),
            in_specs=in_specs,
            out_specs=pl.BlockSpec(memory_space=pl.ANY),
            scratch_shapes=scratch,
        ),
        out_shape=jax.ShapeDtypeStruct((n_slots * TOK_ROWS, LANES), F32),
        input_output_aliases=aliases,
        compiler_params=pltpu.CompilerParams(dimension_semantics=("arbitrary",), vmem_limit_bytes=VMEM_LIMIT,
                                             has_side_effects=True),
        name="dispatch",
    )(tail, has_tail, *args)


def _w1_prep_kernel(w_ref, perm_ref, o_ref):
    for c in range(w_ref.shape[2] // W1_GROUP):
        cols = slice(c * W1_GROUP, (c + 1) * W1_GROUP)
        w = w_ref[0, :, cols].astype(BF16)
        o_ref[0, :, cols] = jnp.dot(w, perm_ref[...], preferred_element_type=F32).astype(BF16)


def _w1_prep(w1):
    e, d, f2 = w1.shape
    half = f2 // 2
    j = jnp.arange(W1_GROUP)
    dst = jnp.where(j % 2 == 0, j // 2, LANES + j // 2)
    perm = (dst[:, None] == jnp.arange(W1_GROUP)[None, :]).astype(BF16)
    return pl.pallas_call(
        _w1_prep_kernel,
        grid=(e, 2),
        in_specs=[pl.BlockSpec((1, d, half), lambda i, j: (i, 0, j)),
                  pl.BlockSpec((W1_GROUP, W1_GROUP), lambda i, j: (0, 0))],
        out_specs=pl.BlockSpec((1, d, half), lambda i, j: (i, 0, j)),
        out_shape=jax.ShapeDtypeStruct((e, d, f2), BF16),
        compiler_params=_cparams(("parallel", "parallel")),
        name="w1prep",
    )(w1, perm)


def _expert_kernel(be_ref, nused_ref, xs_ref, w1_ref, w2_ref, b1_ref, b2_ref, o_ref):
    @pl.when(pl.program_id(0) < nused_ref[0])
    def _():
        x = _tok_load(xs_ref, EXP_BLK).astype(BF16)
        h = jnp.dot(x, w1_ref[0], preferred_element_type=F32) + b1_ref[0]
        groups = range(h.shape[1] // W1_GROUP)
        glu = jnp.concatenate([h[:, c * W1_GROUP: c * W1_GROUP + LANES] for c in groups], axis=1)
        lin = jnp.concatenate([h[:, c * W1_GROUP + LANES: (c + 1) * W1_GROUP] for c in groups], axis=1)
        glu = jnp.minimum(glu, SWIGLU_LIMIT)
        lin = jnp.clip(lin, -SWIGLU_LIMIT, SWIGLU_LIMIT)
        act = glu * jax.nn.sigmoid(SWIGLU_ALPHA * glu) * (lin + 1.0)
        _tok_store(o_ref, jnp.dot(act.astype(BF16), w2_ref[0], preferred_element_type=F32) + b2_ref[0])


def _experts(block_expert, n_used, xs, w1p, w2, b1p, b2):
    n_slots = xs.shape[0] // TOK_ROWS
    n_blocks = n_slots // EXP_BLK
    slot = lambda i, be, nu: (jnp.minimum(i, nu[0] - 1), 0)
    wsel = lambda i, be, nu: (be[i], 0, 0)
    return pl.pallas_call(
        _expert_kernel,
        grid_spec=pltpu.PrefetchScalarGridSpec(
            num_scalar_prefetch=2,
            grid=(n_blocks,),
            in_specs=[
                pl.BlockSpec((EXP_BLK * TOK_ROWS, LANES), slot),
                pl.BlockSpec((1, D_MODEL, 2 * D_FF), wsel),
                pl.BlockSpec((1, D_FF, D_MODEL), wsel),
                pl.BlockSpec((1, 1, 2 * D_FF), wsel),
                pl.BlockSpec((1, 1, D_MODEL), wsel),
            ],
            out_specs=pl.BlockSpec((EXP_BLK * TOK_ROWS, LANES), slot),
        ),
        out_shape=jax.ShapeDtypeStruct((n_slots * TOK_ROWS, LANES), F32),
        compiler_params=_cparams(("arbitrary",)),
        name="experts",
    )(block_expert, n_used, xs, w1p, w2, b1p, b2)


def _combine_kernel(dest_ref, dest_next_ref, x1_ref, gate_ref, lng_ref, lnb_ref, ys_ref, o_ref, buf_ref, sems):
    tm = x1_ref.shape[0] // TOK_ROWS
    step = pl.program_id(0)
    slot = step % 2

    def gather(idx_ref, into):
        def issue(i, c):
            for k in range(TOP_K):
                src = pl.multiple_of(idx_ref[0, 0, k * tm + i], TOK_ROWS)
                dst = pl.multiple_of(i * TOK_ROWS, TOK_ROWS)
                pltpu.make_async_copy(ys_ref.at[pl.ds(src, TOK_ROWS), :],
                                      buf_ref.at[into, k, pl.ds(dst, TOK_ROWS), :], sems.at[into]
                                      ).start(priority=k % 2)
            return c

        lax.fori_loop(0, tm, issue, 0, unroll=4)

    @pl.when(step == 0)
    def _():
        gather(dest_ref, slot)

    @pl.when(step + 1 < pl.num_programs(0))
    def _():
        gather(dest_next_ref, 1 - slot)

    for k in range(TOP_K):
        pltpu.make_async_copy(ys_ref.at[pl.ds(0, tm * TOK_ROWS), :], buf_ref.at[slot, k], sems.at[slot]).wait()
    gate = gate_ref[...]
    y = _tok_load(buf_ref, tm, (slot, 0)) * gate[:, 0:1]
    for k in range(1, TOP_K):
        y = y + _tok_load(buf_ref, tm, (slot, k)) * gate[:, k:k + 1]
    o_ref[...] = _layer_norm(DN_ALPHA * _tok_load(x1_ref, tm) + y, lng_ref[...], lnb_ref[...])


def _combine(dest_tiles, x1, gate, lng, lnb, ys):
    n = x1.shape[0] // TOK_ROWS
    tm = COMB_TM
    assert n % tm == 0
    n_tiles = n // tm
    return pl.pallas_call(
        _combine_kernel,
        grid=(n_tiles,),
        in_specs=[
            pl.BlockSpec((1, 1, TOP_K * tm), lambda i: (i, 0, 0), memory_space=pltpu.SMEM),
            pl.BlockSpec((1, 1, TOP_K * tm), lambda i: (jnp.minimum(i + 1, n_tiles - 1), 0, 0),
                         memory_space=pltpu.SMEM),
            pl.BlockSpec((tm * TOK_ROWS, LANES), lambda i: (i, 0)),
            pl.BlockSpec((tm, TOP_K), lambda i: (i, 0)),
            pl.BlockSpec((1, D_MODEL), lambda i: (0, 0)),
            pl.BlockSpec((1, D_MODEL), lambda i: (0, 0)),
            pl.BlockSpec(memory_space=pl.ANY),
        ],
        out_specs=pl.BlockSpec((tm, D_MODEL), lambda i: (i, 0)),
        out_shape=jax.ShapeDtypeStruct((n, D_MODEL), F32),
        scratch_shapes=[pltpu.VMEM((2, TOP_K, tm * TOK_ROWS, LANES), F32), pltpu.SemaphoreType.DMA((2,))],
        compiler_params=_cparams(("arbitrary",)),
        name="combine",
    )(dest_tiles, dest_tiles, x1, gate, lng, lnb, ys)


def _rope_tables(t):
    pos = jnp.arange(t)
    row = (pos // GRID_W).astype(F32)
    col = (pos % GRID_W).astype(F32)
    inv = ROPE_THETA ** (-jnp.arange(0, ROPE_AXIS_DIM, 2, dtype=F32) / ROPE_AXIS_DIM)
    ar = row[:, None] * inv[None, :]
    ac = col[:, None] * inv[None, :]
    cos = jnp.concatenate([jnp.cos(ar), jnp.cos(ar), jnp.cos(ac), jnp.cos(ac)], axis=1)
    sin = jnp.concatenate([-jnp.sin(ar), jnp.sin(ar), -jnp.sin(ac), jnp.sin(ac)], axis=1)
    return jnp.tile(cos, (1, 2)), jnp.tile(sin, (1, 2))


def _prep_w_in(w):
    scale = HEAD_DIM ** -0.5
    qa = w[:, :NA_WIDTH] * scale
    kva = w[:, NA_WIDTH:3 * NA_WIDTH]
    qb = _gqa_out_order(w[:, 3 * NA_WIDTH:3 * NA_WIDTH + GQA_WIDTH].T).T
    rest = w[:, 3 * NA_WIDTH + GQA_WIDTH:]
    return jnp.concatenate([qa, kva, qb, rest], axis=1).astype(BF16)


def _gqa_out_order(a):
    rest = a.shape[1:]
    a = a.reshape((GQA_KV_HEADS, GQA_GROUP, HEAD_DIM) + rest)
    return jnp.swapaxes(a, 0, 1).reshape((GQA_WIDTH,) + rest)


def _dest_tiles(dest_t, tm):
    n = dest_t.shape[1]
    return dest_t.reshape(TOP_K, n // tm, tm).transpose(1, 0, 2).reshape(n // tm, 1, TOP_K * tm)


def kernel(x_prompt, x_sample, w_in, rpb, q_norm_g, k_norm_g, g_out_na, g_out_gqa, w_o, ln1_g, ln1_b,
           router_w, router_b, w1, b1, w2, b2, ln2_g, ln2_b):
    assert GQA_KV_HEADS == 2 and KV_WIDTH == LANES
    xs_in = [x_prompt, x_sample]
    l = 0
    w_proj = _prep_w_in(w_in[l])
    scale = HEAD_DIM ** -0.5
    gq = jnp.tile(q_norm_g[l] * scale, 2).reshape(1, LANES)
    gk = jnp.tile(k_norm_g[l], 2).reshape(1, LANES)
    bias = _na_bias_tables(rpb[l])
    woa = w_o[l][:NA_WIDTH].astype(BF16)
    wob = _gqa_out_order(w_o[l][NA_WIDTH:]).astype(BF16)
    ga = g_out_na[l].reshape(1, NA_WIDTH)
    gb = _gqa_out_order(g_out_gqa[l]).reshape(1, GQA_WIDTH)
    ln1g, ln1b = ln1_g[l].reshape(1, D_MODEL), ln1_b[l].reshape(1, D_MODEL)
    ln2g, ln2b = ln2_g[l].reshape(1, D_MODEL), ln2_b[l].reshape(1, D_MODEL)
    rw_t = router_w[l].T.astype(BF16)
    rb = router_b[l].reshape(N_EXPERTS, 1)
    tri = (jnp.arange(MIX_TM)[:, None] < jnp.arange(MIX_TM)[None, :]).astype(BF16)
    w1p = _w1_prep(w1[l])
    w2b = w2[l].astype(BF16)
    b1p = jnp.swapaxes(b1[l].reshape(N_EXPERTS, 2 * D_FF // W1_GROUP, LANES, 2), 2, 3).reshape(N_EXPERTS, 1, 2 * D_FF)
    b2r = b2[l].reshape(N_EXPERTS, 1, D_MODEL)

    x1s, idxs, gates, ranks = [], [], [], []
    cnt = jnp.zeros((N_EXPERTS, LANES), F32)
    for x in xs_in:
        b, t, _ = x.shape
        x2d = x.reshape(b * t, D_MODEL)
        cos, sin = _rope_tables(t)
        qa, ka, va, qb, kb, vb = _proj(x2d, t, w_proj, cos, sin, gq, gk)
        sh = lambda a: a.reshape(b, t, a.shape[-1])
        oa = _na(sh(qa), sh(ka), sh(va), bias).reshape(b * t, NA_WIDTH)
        ob = _gqa(sh(qb), sh(kb), sh(vb)).reshape(b * t, GQA_WIDTH)
        x1, idx_t, gate_t, rank_t, cnt = _mix(oa, ob, x2d, woa, wob, ga, gb, ln1g, ln1b, rw_t, rb, tri, cnt)
        x1s.append(x1)
        idxs.append(idx_t)
        gates.append(gate_t)
        ranks.append(rank_t)

    counts = cnt[:, 0].astype(jnp.int32)
    nblk = (counts + EXP_BLK - 1) // EXP_BLK
    blk_end = jnp.cumsum(nblk)
    pad_start = (blk_end - nblk) * EXP_BLK
    n_assign = TOP_K * sum(x.shape[0] * x.shape[1] for x in xs_in)
    n_blocks = -(-n_assign // EXP_BLK) + N_EXPERTS
    n_slots = n_blocks * EXP_BLK
    blocks = jnp.arange(n_blocks, dtype=jnp.int32)
    block_expert = jnp.minimum(jnp.sum((blk_end[None, :] <= blocks[:, None]).astype(jnp.int32), axis=1),
                               N_EXPERTS - 1)
    n_used = blk_end[-1:].astype(jnp.int32)
    tail = ((blk_end - 1) * (EXP_BLK * TOK_ROWS)).astype(jnp.int32)
    has_tail = (counts % EXP_BLK != 0).astype(jnp.int32)
    dests = [_slots(pad_start.astype(jnp.int32), idx_t, rank_t) for idx_t, rank_t in zip(idxs, ranks)]

    slots = None
    for x1, dest_t in zip(x1s, dests):
        slots = _dispatch(tail, has_tail, _dest_tiles(dest_t, DISP_TM), x1, slots, n_slots)
    ys = _experts(block_expert, n_used, slots, w1p, w2b, b1p, b2r)
    outs = []
    for x, x1, dest_t, gate_t in zip(xs_in, x1s, dests, gates):
        y = _combine(_dest_tiles(dest_t, COMB_TM), x1, gate_t.T, ln2g, ln2b, ys)
        outs.append(y.reshape(x.shape))
    return tuple(outs)
```

```python
import functools

import jax
import jax.numpy as jnp
from jax import lax
from jax.experimental import pallas as pl
from jax.experimental.pallas import tpu as pltpu

D_MODEL = 1024
GRID_W = 64
HEAD_DIM = 64
NA_HEADS = 8
GQA_HEADS = 8
GQA_KV_HEADS = 2
GQA_GROUP = GQA_HEADS // GQA_KV_HEADS
NA_WIDTH = NA_HEADS * HEAD_DIM
GQA_WIDTH = GQA_HEADS * HEAD_DIM
KV_WIDTH = GQA_KV_HEADS * HEAD_DIM
NA_WIN_R = 8
NA_WIN_C = 16
ROPE_AXIS_DIM = HEAD_DIM // 2
ROPE_THETA = 10000.0
N_EXPERTS = 32
TOP_K = 4
D_FF = D_MODEL
SWIGLU_ALPHA = 1.702
SWIGLU_LIMIT = 7.0
DEPTH = 1
DN_ALPHA = (2.0 * DEPTH) ** 0.25
NEG_INF = -1e30
RMS_EPS = 1e-6
LN_EPS = 1e-5

LANES = 128
PROJ_COLS = 3 * NA_WIDTH + GQA_WIDTH + 2 * KV_WIDTH

PROJ_TM = 512
NA_ROWS = 8
NA_PASS_HEADS = 4
GQA_SCORE_ELEMS = 1024 * 2048
MIX_TM = 512
DISP_TM = 256
COMB_TM = 256
EXP_BLK = 512
W1_GROUP = 2 * LANES
SLOT_TM = 2048
VMEM_LIMIT = 56 * 1024 * 1024

F32 = jnp.float32
BF16 = jnp.bfloat16

TOK_ROWS = D_MODEL // LANES


def _tok_load(ref, n, lead=()):
    return jnp.concatenate([ref[lead + (pl.ds(s, n, stride=TOK_ROWS), slice(None))] for s in range(TOK_ROWS)],
                           axis=1)


def _tok_store(ref, val):
    n = val.shape[0]
    for s in range(TOK_ROWS):
        ref[pl.ds(s, n, stride=TOK_ROWS), :] = val[:, s * LANES:(s + 1) * LANES]


def _cparams(sem):
    return pltpu.CompilerParams(dimension_semantics=sem, vmem_limit_bytes=VMEM_LIMIT)


def _rope(y, cos, sin, first_half):
    partner = jnp.where(first_half, pltpu.roll(y, LANES - 16, 1), pltpu.roll(y, 16, 1))
    return y * cos + partner * sin


def _proj_kernel(x_ref, w_ref, cos_ref, sin_ref, gq_ref, gk_ref,
                 qa_ref, ka_ref, va_ref, qb_ref, kb_ref, vlo_ref, vhi_ref):
    x = x_ref[...].astype(BF16)

    def cols(c0, width):
        return jnp.dot(x, w_ref[:, c0:c0 + width], preferred_element_type=F32)

    qa_ref[...] = cols(0, NA_WIDTH).astype(BF16)
    ka_ref[...] = cols(NA_WIDTH, NA_WIDTH).astype(BF16)
    va_ref[...] = cols(2 * NA_WIDTH, NA_WIDTH).astype(BF16)
    cos = cos_ref[...]
    sin = sin_ref[...]
    lane = lax.broadcasted_iota(jnp.int32, cos.shape, 1)
    first_half = (lane % 32) < 16
    lo = lane < HEAD_DIM

    def norm_rope(y, g):
        y2 = y * y
        ms_lo = jnp.sum(jnp.where(lo, y2, 0.0), axis=-1, keepdims=True)
        ms_hi = jnp.sum(jnp.where(lo, 0.0, y2), axis=-1, keepdims=True)
        ms = jnp.where(lo, ms_lo, ms_hi) * (1.0 / HEAD_DIM)
        return _rope(y * lax.rsqrt(ms + RMS_EPS) * g, cos, sin, first_half).astype(BF16)

    base = 3 * NA_WIDTH
    q = cols(base, GQA_WIDTH)
    gq = gq_ref[...]
    for g in range(GQA_GROUP):
        qb_ref[:, g * LANES:(g + 1) * LANES] = norm_rope(q[:, g * LANES:(g + 1) * LANES], gq)
    base += GQA_WIDTH
    kv = cols(base, 2 * KV_WIDTH)
    kb_ref[...] = norm_rope(kv[:, :KV_WIDTH], gk_ref[...])
    v = kv[:, KV_WIDTH:]
    vlo_ref[...] = jnp.where(lo, v, 1.0).astype(BF16)
    vhi_ref[...] = jnp.where(lo, 1.0, v).astype(BF16)


def _proj(x2d, seq_len, w, cos, sin, gq, gk):
    n = x2d.shape[0]
    tm = PROJ_TM
    assert n % tm == 0 and seq_len % tm == 0
    pos_blocks = seq_len // tm
    row = lambda i: (i, 0)
    const = lambda i: (0, 0)
    outs = [(NA_WIDTH, BF16)] * 3 + [(GQA_WIDTH, BF16)] + [(KV_WIDTH, BF16)] * 3
    return pl.pallas_call(
        _proj_kernel,
        grid=(n // tm,),
        in_specs=[
            pl.BlockSpec((tm, D_MODEL), row),
            pl.BlockSpec((D_MODEL, PROJ_COLS), const),
            pl.BlockSpec((tm, LANES), lambda i: (i % pos_blocks, 0)),
            pl.BlockSpec((tm, LANES), lambda i: (i % pos_blocks, 0)),
            pl.BlockSpec((1, LANES), const),
            pl.BlockSpec((1, LANES), const),
        ],
        out_specs=[pl.BlockSpec((tm, c), row) for c, _ in outs],
        out_shape=[jax.ShapeDtypeStruct((n, c), dt) for c, dt in outs],
        compiler_params=_cparams(("parallel",)),
        name="proj",
    )(x2d, w, cos, sin, gq, gk)


def _na_kernel(q_ref, k_ref, v_ref, bias_ref, o_ref, *, rows):
    j = pl.program_id(1)
    width = NA_PASS_HEADS * HEAD_DIM
    lane_head = lax.broadcasted_iota(jnp.int32, (GRID_W, width), 1) // HEAD_DIM

    def one_row(rr, carry):
        r = j * NA_ROWS + rr
        rs = jnp.clip(r - NA_WIN_R // 2, 0, rows - NA_WIN_R)
        var = r - rs
        k0 = pl.multiple_of(rs * GRID_W, GRID_W)
        q0 = pl.multiple_of(rr * GRID_W, GRID_W)
        for g in range(NA_HEADS // NA_PASS_HEADS):
            cols = slice(g * width, (g + 1) * width)
            heads = range(g * NA_PASS_HEADS, (g + 1) * NA_PASS_HEADS)
            q = q_ref[0, pl.ds(q0, GRID_W), cols]
            zero = jnp.zeros_like(q)
            qs = jnp.concatenate([jnp.where(lane_head == a, q, zero) for a in range(NA_PASS_HEADS)], axis=0)
            k = k_ref[0, pl.ds(k0, NA_WIN_R * GRID_W), cols]
            v = v_ref[0, pl.ds(k0, NA_WIN_R * GRID_W), cols]
            s = lax.dot_general(qs, k, (((1,), (1,)), ((), ())), preferred_element_type=F32)
            s = s + jnp.concatenate([bias_ref[var, h] for h in heads], axis=0)
            m = jnp.max(s, axis=-1, keepdims=True)
            p = jnp.exp(s - m)
            l = jnp.sum(p, axis=-1, keepdims=True)
            os = jnp.dot(p.astype(BF16), v, preferred_element_type=F32) / l
            o = os[:GRID_W]
            for a in range(1, NA_PASS_HEADS):
                o = jnp.where(lane_head == a, os[a * GRID_W:(a + 1) * GRID_W], o)
            o_ref[0, pl.ds(q0, GRID_W), cols] = o.astype(BF16)
        return carry

    lax.fori_loop(0, NA_ROWS, one_row, 0, unroll=True)


def _na(qa, ka, va, bias):
    b, t, _ = qa.shape
    rows = t // GRID_W
    assert rows >= NA_WIN_R and rows % NA_ROWS == 0
    tq = NA_ROWS * GRID_W
    return pl.pallas_call(
        functools.partial(_na_kernel, rows=rows),
        grid=(b, rows // NA_ROWS),
        in_specs=[
            pl.BlockSpec((1, tq, NA_WIDTH), lambda i, j: (i, j, 0)),
            pl.BlockSpec((1, t, NA_WIDTH), lambda i, j: (i, 0, 0)),
            pl.BlockSpec((1, t, NA_WIDTH), lambda i, j: (i, 0, 0)),
            pl.BlockSpec(bias.shape, lambda i, j: (0, 0, 0, 0), pipeline_mode=pl.Buffered(1)),
        ],
        out_specs=pl.BlockSpec((1, tq, NA_WIDTH), lambda i, j: (i, j, 0)),
        out_shape=jax.ShapeDtypeStruct((b, t, NA_WIDTH), BF16),
        compiler_params=_cparams(("parallel", "parallel")),
        name="na",
    )(qa, ka, va, bias)


def _na_bias_tables(rpb):
    off = jnp.arange(NA_WIN_R)
    jrow = jnp.arange(NA_WIN_R)
    dr = jrow[None, :] - off[:, None] + (NA_WIN_R - 1)
    c = jnp.arange(GRID_W)
    cs = jnp.clip(c - NA_WIN_C // 2, 0, GRID_W - NA_WIN_C)
    col_ok = (c[None, :] >= cs[:, None]) & (c[None, :] < cs[:, None] + NA_WIN_C)
    dc = jnp.clip(c[None, :] - c[:, None], -(NA_WIN_C - 1), NA_WIN_C - 1) + (NA_WIN_C - 1)
    sel_r = (dr[:, :, None] == jnp.arange(2 * NA_WIN_R - 1)[None, None, :]).astype(F32)
    sel_c = (dc[:, :, None] == jnp.arange(2 * NA_WIN_C - 1)[None, None, :]).astype(F32)
    bias = jnp.einsum('vja,hab,qkb->hvqjk', sel_r, rpb.astype(F32), sel_c, precision=lax.Precision.HIGHEST)
    bias = jnp.where(col_ok[None, None, :, None, :], bias, NEG_INF)
    return jnp.moveaxis(bias, 0, 1).reshape(NA_WIN_R, NA_HEADS, GRID_W, NA_WIN_R * GRID_W)


def _gqa_kernel(q_ref, k_ref, vlo_ref, vhi_ref, o_ref):
    k = k_ref[0]
    lane = lax.broadcasted_iota(jnp.int32, (q_ref.shape[1], LANES), 1)
    lo = lane < HEAD_DIM

    def head(q, v):
        s = lax.dot_general(q, k, (((1,), (1,)), ((), ())), preferred_element_type=F32)
        p = jnp.exp(s - jnp.max(s, axis=-1, keepdims=True)).astype(BF16)
        ov = jnp.dot(p, v, preferred_element_type=F32)
        return ov / pltpu.roll(ov, HEAD_DIM, 1)

    for g in range(GQA_GROUP):
        q = q_ref[0, :, g * LANES:(g + 1) * LANES]
        zero = jnp.zeros_like(q)
        o = jnp.where(lo, head(jnp.where(lo, q, zero), vlo_ref[0]), head(jnp.where(lo, zero, q), vhi_ref[0]))
        o_ref[0, :, g * LANES:(g + 1) * LANES] = o.astype(BF16)


def _gqa(qb, kb, vlo, vhi):
    b, t, _ = qb.shape
    tq = min(t, GQA_SCORE_ELEMS // t)
    assert t % tq == 0 and tq % 8 == 0
    return pl.pallas_call(
        _gqa_kernel,
        grid=(b, t // tq),
        in_specs=[
            pl.BlockSpec((1, tq, GQA_WIDTH), lambda i, j: (i, j, 0)),
            pl.BlockSpec((1, t, KV_WIDTH), lambda i, j: (i, 0, 0)),
            pl.BlockSpec((1, t, KV_WIDTH), lambda i, j: (i, 0, 0)),
            pl.BlockSpec((1, t, KV_WIDTH), lambda i, j: (i, 0, 0)),
        ],
        out_specs=pl.BlockSpec((1, tq, GQA_WIDTH), lambda i, j: (i, j, 0)),
        out_shape=jax.ShapeDtypeStruct((b, t, GQA_WIDTH), BF16),
        compiler_params=_cparams(("parallel", "parallel")),
        name="gqa",
    )(qb, kb, vlo, vhi)


def _layer_norm(z, g, b):
    mu = jnp.mean(z, axis=-1, keepdims=True)
    zc = z - mu
    var = jnp.mean(zc * zc, axis=-1, keepdims=True)
    return zc * lax.rsqrt(var + LN_EPS) * g + b


def _rms(o, g):
    return o * lax.rsqrt(jnp.mean(o * o, axis=-1, keepdims=True) + RMS_EPS) * g


def _mix_kernel(oa_ref, ob_ref, x_ref, woa_ref, wob_ref, ga_ref, gb_ref, lng_ref, lnb_ref,
                rw_ref, rb_ref, tri_ref, cnt_in_ref,
                x1_ref, idx_ref, gate_ref, rank_ref, cnt_ref, carry_ref):
    @pl.when(pl.program_id(0) == 0)
    def _():
        carry_ref[...] = cnt_in_ref[...]

    na = _rms(oa_ref[...].astype(F32), ga_ref[...]).astype(BF16)
    nb = _rms(ob_ref[...].astype(F32), gb_ref[...]).astype(BF16)
    mixed = (jnp.dot(na, woa_ref[...], preferred_element_type=F32)
             + jnp.dot(nb, wob_ref[...], preferred_element_type=F32))
    x1 = _layer_norm(DN_ALPHA * x_ref[...] + mixed, lng_ref[...], lnb_ref[...])
    _tok_store(x1_ref, x1)

    logits = lax.dot_general(rw_ref[...], x1.astype(BF16), (((1,), (1,)), ((), ())),
                             preferred_element_type=F32) + rb_ref[...]
    tm = logits.shape[1]
    eidx = lax.broadcasted_iota(jnp.int32, (N_EXPERTS, tm), 0).astype(F32)
    work = logits
    vals, idxs, hots = [], [], []
    for _ in range(TOP_K):
        m = jnp.max(work, axis=0, keepdims=True)
        sel = jnp.min(jnp.where(work == m, eidx, float(N_EXPERTS)), axis=0, keepdims=True)
        hot = eidx == sel
        vals.append(m)
        idxs.append(sel)
        hots.append(hot)
        work = jnp.where(hot, -jnp.inf, work)
    es = [jnp.exp(v - vals[0]) for v in vals]
    den = es[0] + es[1] + es[2] + es[3]
    gate_ref[...] = jnp.concatenate([e / den for e in es], axis=0)
    idx_ref[...] = jnp.concatenate(idxs, axis=0).astype(jnp.int32)

    hot_all = hots[0] | hots[1] | hots[2] | hots[3]
    onehot = jnp.where(hot_all, 1.0, 0.0)
    before = jnp.dot(onehot.astype(BF16), tri_ref[...], preferred_element_type=F32)
    before = before + carry_ref[:, 0:1]
    ranks = [jnp.sum(jnp.where(hot, before, 0.0), axis=0, keepdims=True) for hot in hots]
    rank_ref[...] = jnp.concatenate(ranks, axis=0).astype(jnp.int32)
    carry_ref[...] = carry_ref[...] + jnp.sum(onehot, axis=1, keepdims=True)
    cnt_ref[...] = carry_ref[...]


def _mix(oa, ob, x2d, woa, wob, ga, gb, lng, lnb, rw_t, rb, tri, cnt_in):
    n = x2d.shape[0]
    tm = MIX_TM
    assert n % tm == 0
    row = lambda i: (i, 0)
    col = lambda i: (0, i)
    const = lambda i: (0, 0)
    full = lambda a: pl.BlockSpec(a.shape, const)
    return pl.pallas_call(
        _mix_kernel,
        grid=(n // tm,),
        in_specs=[
            pl.BlockSpec((tm, NA_WIDTH), row),
            pl.BlockSpec((tm, GQA_WIDTH), row),
            pl.BlockSpec((tm, D_MODEL), row),
            full(woa), full(wob), full(ga), full(gb), full(lng), full(lnb),
            full(rw_t), full(rb), full(tri), full(cnt_in),
        ],
        out_specs=[
            pl.BlockSpec((tm * TOK_ROWS, LANES), row),
            pl.BlockSpec((TOP_K, tm), col),
            pl.BlockSpec((TOP_K, tm), col),
            pl.BlockSpec((TOP_K, tm), col),
            pl.BlockSpec((N_EXPERTS, LANES), const),
        ],
        out_shape=[
            jax.ShapeDtypeStruct((n * TOK_ROWS, LANES), F32),
            jax.ShapeDtypeStruct((TOP_K, n), jnp.int32),
            jax.ShapeDtypeStruct((TOP_K, n), F32),
            jax.ShapeDtypeStruct((TOP_K, n), jnp.int32),
            jax.ShapeDtypeStruct((N_EXPERTS, LANES), F32),
        ],
        scratch_shapes=[pltpu.VMEM((N_EXPERTS, LANES), F32)],
        compiler_params=_cparams(("arbitrary",)),
        name="mix",
    )(oa, ob, x2d, woa, wob, ga, gb, lng, lnb, rw_t, rb, tri, cnt_in)


def _slot_kernel(pad_ref, idx_ref, rank_ref, o_ref):
    idx = idx_ref[...]
    start = jnp.zeros_like(idx)
    for e in range(N_EXPERTS):
        start = jnp.where(idx == e, pad_ref[e], start)
    o_ref[...] = (start + rank_ref[...]) * TOK_ROWS


def _slots(pad_start, idx_t, rank_t):
    n = idx_t.shape[1]
    tm = min(SLOT_TM, n)
    assert n % tm == 0
    col = lambda i, *_: (0, i)
    return pl.pallas_call(
        _slot_kernel,
        grid_spec=pltpu.PrefetchScalarGridSpec(
            num_scalar_prefetch=1,
            grid=(n // tm,),
            in_specs=[pl.BlockSpec((TOP_K, tm), col), pl.BlockSpec((TOP_K, tm), col)],
            out_specs=pl.BlockSpec((TOP_K, tm), col),
        ),
        out_shape=jax.ShapeDtypeStruct((TOP_K, n), jnp.int32),
        compiler_params=_cparams(("parallel",)),
        name="slots",
    )(pad_start, idx_t, rank_t)


def _dispatch_kernel(tail_ref, has_tail_ref, dest_ref, x_ref, *rest, zero_tails):
    if zero_tails:
        xs_ref, zeros_ref, sem = rest
    else:
        _, xs_ref, sem = rest
    tm = x_ref.shape[0] // TOK_ROWS

    if zero_tails:
        @pl.when(pl.program_id(0) == 0)
        def _():
            zeros_ref[...] = jnp.zeros_like(zeros_ref)

            def tail_copy(e):
                start = pl.multiple_of(tail_ref[e], EXP_BLK * TOK_ROWS)
                return pltpu.make_async_copy(zeros_ref, xs_ref.at[pl.ds(start, EXP_BLK * TOK_ROWS), :], sem)

            for e in range(N_EXPERTS):
                @pl.when(has_tail_ref[e] != 0)
                def _():
                    tail_copy(e).start()
            for e in range(N_EXPERTS):
                @pl.when(has_tail_ref[e] != 0)
                def _():
                    tail_copy(e).wait()

    def row_copy(i, k):
        src = pl.multiple_of(i * TOK_ROWS, TOK_ROWS)
        dst = pl.multiple_of(dest_ref[0, 0, k * tm + i], TOK_ROWS)
        return pltpu.make_async_copy(x_ref.at[pl.ds(src, TOK_ROWS), :], xs_ref.at[pl.ds(dst, TOK_ROWS), :], sem)

    def issue(i, c):
        for k in range(TOP_K):
            row_copy(i, k).start(priority=k % 2)
        return c

    lax.fori_loop(0, tm, issue, 0, unroll=4)
    for k in range(TOP_K):
        pltpu.make_async_copy(x_ref, xs_ref.at[pl.ds(0, tm * TOK_ROWS), :], sem).wait()


def _dispatch(tail, has_tail, dest_tiles, x1, xs_prev, n_slots):
    n = x1.shape[0] // TOK_ROWS
    tm = DISP_TM
    assert n % tm == 0
    zero_tails = xs_prev is None
    in_specs = [
        pl.BlockSpec((1, 1, TOP_K * tm), lambda i, *_: (i, 0, 0), memory_space=pltpu.SMEM),
        pl.BlockSpec((tm * TOK_ROWS, LANES), lambda i, *_: (i, 0)),
    ]
    args = [dest_tiles, x1]
    scratch = []
    aliases = {}
    if zero_tails:
        scratch.append(pltpu.VMEM((EXP_BLK * TOK_ROWS, LANES), F32))
    else:
        in_specs.append(pl.BlockSpec(memory_space=pl.ANY))
        args.append(xs_prev)
        aliases = {4: 0}
    scratch.append(pltpu.SemaphoreType.DMA(()))
    return pl.pallas_call(
        functools.partial(_dispatch_kernel, zero_tails=zero_tails),
        grid_spec=pltpu.PrefetchScalarGridSpec(
            num_scalar_prefetch=2,
            grid=(n // tm,),
            in_specs=in_specs,
            out_specs=pl.BlockSpec(memory_space=pl.ANY),
            scratch_shapes=scratch,
        ),
        out_shape=jax.ShapeDtypeStruct((n_slots * TOK_ROWS, LANES), F32),
        input_output_aliases=aliases,
        compiler_params=pltpu.CompilerParams(dimension_semantics=("arbitrary",), vmem_limit_bytes=VMEM_LIMIT,
                                             has_side_effects=True),
        name="dispatch",
    )(tail, has_tail, *args)


def _w1_prep_kernel(w_ref, perm_ref, o_ref):
    for c in range(w_ref.shape[2] // W1_GROUP):
        cols = slice(c * W1_GROUP, (c + 1) * W1_GROUP)
        w = w_ref[0, :, cols].astype(BF16)
        o_ref[0, :, cols] = jnp.dot(w, perm_ref[...], preferred_element_type=F32).astype(BF16)


def _w1_prep(w1):
    e, d, f2 = w1.shape
    half = f2 // 2
    j = jnp.arange(W1_GROUP)
    dst = jnp.where(j % 2 == 0, j // 2, LANES + j // 2)
    perm = (dst[:, None] == jnp.arange(W1_GROUP)[None, :]).astype(BF16)
    return pl.pallas_call(
        _w1_prep_kernel,
        grid=(e, 2),
        in_specs=[pl.BlockSpec((1, d, half), lambda i, j: (i, 0, j)),
                  pl.BlockSpec((W1_GROUP, W1_GROUP), lambda i, j: (0, 0))],
        out_specs=pl.BlockSpec((1, d, half), lambda i, j: (i, 0, j)),
        out_shape=jax.ShapeDtypeStruct((e, d, f2), BF16),
        compiler_params=_cparams(("parallel", "parallel")),
        name="w1prep",
    )(w1, perm)


def _expert_kernel(be_ref, nused_ref, xs_ref, w1_ref, w2_ref, b1_ref, b2_ref, o_ref):
    @pl.when(pl.program_id(0) < nused_ref[0])
    def _():
        x = _tok_load(xs_ref, EXP_BLK).astype(BF16)
        h = jnp.dot(x, w1_ref[0], preferred_element_type=F32) + b1_ref[0]
        groups = range(h.shape[1] // W1_GROUP)
        glu = jnp.concatenate([h[:, c * W1_GROUP: c * W1_GROUP + LANES] for c in groups], axis=1)
        lin = jnp.concatenate([h[:, c * W1_GROUP + LANES: (c + 1) * W1_GROUP] for c in groups], axis=1)
        glu = jnp.minimum(glu, SWIGLU_LIMIT)
        lin = jnp.clip(lin, -SWIGLU_LIMIT, SWIGLU_LIMIT)
        act = glu * jax.nn.sigmoid(SWIGLU_ALPHA * glu) * (lin + 1.0)
        _tok_store(o_ref, jnp.dot(act.astype(BF16), w2_ref[0], preferred_element_type=F32) + b2_ref[0])


def _experts(block_expert, n_used, xs, w1p, w2, b1p, b2):
    n_slots = xs.shape[0] // TOK_ROWS
    n_blocks = n_slots // EXP_BLK
    slot = lambda i, be, nu: (jnp.minimum(i, nu[0] - 1), 0)
    wsel = lambda i, be, nu: (be[i], 0, 0)
    return pl.pallas_call(
        _expert_kernel,
        grid_spec=pltpu.PrefetchScalarGridSpec(
            num_scalar_prefetch=2,
            grid=(n_blocks,),
            in_specs=[
                pl.BlockSpec((EXP_BLK * TOK_ROWS, LANES), slot),
                pl.BlockSpec((1, D_MODEL, 2 * D_FF), wsel),
                pl.BlockSpec((1, D_FF, D_MODEL), wsel),
                pl.BlockSpec((1, 1, 2 * D_FF), wsel),
                pl.BlockSpec((1, 1, D_MODEL), wsel),
            ],
            out_specs=pl.BlockSpec((EXP_BLK * TOK_ROWS, LANES), slot),
        ),
        out_shape=jax.ShapeDtypeStruct((n_slots * TOK_ROWS, LANES), F32),
        compiler_params=_cparams(("arbitrary",)),
        name="experts",
    )(block_expert, n_used, xs, w1p, w2, b1p, b2)


def _combine_kernel(dest_ref, dest_next_ref, x1_ref, gate_ref, lng_ref, lnb_ref, ys_ref, o_ref, buf_ref, sems):
    tm = x1_ref.shape[0] // TOK_ROWS
    step = pl.program_id(0)
    slot = step % 2

    def gather(idx_ref, into):
        def issue(i, c):
            for k in range(TOP_K):
                src = pl.multiple_of(idx_ref[0, 0, k * tm + i], TOK_ROWS)
                dst = pl.multiple_of(i * TOK_ROWS, TOK_ROWS)
                pltpu.make_async_copy(ys_ref.at[pl.ds(src, TOK_ROWS), :],
                                      buf_ref.at[into, k, pl.ds(dst, TOK_ROWS), :], sems.at[into]
                                      ).start(priority=k % 2)
            return c

        lax.fori_loop(0, tm, issue, 0, unroll=4)

    @pl.when(step == 0)
    def _():
        gather(dest_ref, slot)

    @pl.when(step + 1 < pl.num_programs(0))
    def _():
        gather(dest_next_ref, 1 - slot)

    for k in range(TOP_K):
        pltpu.make_async_copy(ys_ref.at[pl.ds(0, tm * TOK_ROWS), :], buf_ref.at[slot, k], sems.at[slot]).wait()
    gate = gate_ref[...]
    y = _tok_load(buf_ref, tm, (slot, 0)) * gate[:, 0:1]
    for k in range(1, TOP_K):
        y = y + _tok_load(buf_ref, tm, (slot, k)) * gate[:, k:k + 1]
    o_ref[...] = _layer_norm(DN_ALPHA * _tok_load(x1_ref, tm) + y, lng_ref[...], lnb_ref[...])


def _combine(dest_tiles, x1, gate, lng, lnb, ys):
    n = x1.shape[0] // TOK_ROWS
    tm = COMB_TM
    assert n % tm == 0
    n_tiles = n // tm
    return pl.pallas_call(
        _combine_kernel,
        grid=(n_tiles,),
        in_specs=[
            pl.BlockSpec((1, 1, TOP_K * tm), lambda i: (i, 0, 0), memory_space=pltpu.SMEM),
            pl.BlockSpec((1, 1, TOP_K * tm), lambda i: (jnp.minimum(i + 1, n_tiles - 1), 0, 0),
                         memory_space=pltpu.SMEM),
            pl.BlockSpec((tm * TOK_ROWS, LANES), lambda i: (i, 0)),
            pl.BlockSpec((tm, TOP_K), lambda i: (i, 0)),
            pl.BlockSpec((1, D_MODEL), lambda i: (0, 0)),
            pl.BlockSpec((1, D_MODEL), lambda i: (0, 0)),
            pl.BlockSpec(memory_space=pl.ANY),
        ],
        out_specs=pl.BlockSpec((tm, D_MODEL), lambda i: (i, 0)),
        out_shape=jax.ShapeDtypeStruct((n, D_MODEL), F32),
        scratch_shapes=[pltpu.VMEM((2, TOP_K, tm * TOK_ROWS, LANES), F32), pltpu.SemaphoreType.DMA((2,))],
        compiler_params=_cparams(("arbitrary",)),
        name="combine",
    )(dest_tiles, dest_tiles, x1, gate, lng, lnb, ys)


def _rope_tables(t):
    pos = jnp.arange(t)
    row = (pos // GRID_W).astype(F32)
    col = (pos % GRID_W).astype(F32)
    inv = ROPE_THETA ** (-jnp.arange(0, ROPE_AXIS_DIM, 2, dtype=F32) / ROPE_AXIS_DIM)
    ar = row[:, None] * inv[None, :]
    ac = col[:, None] * inv[None, :]
    cos = jnp.concatenate([jnp.cos(ar), jnp.cos(ar), jnp.cos(ac), jnp.cos(ac)], axis=1)
    sin = jnp.concatenate([-jnp.sin(ar), jnp.sin(ar), -jnp.sin(ac), jnp.sin(ac)], axis=1)
    return jnp.tile(cos, (1, 2)), jnp.tile(sin, (1, 2))


def _prep_w_in(w):
    scale = HEAD_DIM ** -0.5
    qa = w[:, :NA_WIDTH] * scale
    kva = w[:, NA_WIDTH:3 * NA_WIDTH]
    qb = _gqa_out_order(w[:, 3 * NA_WIDTH:3 * NA_WIDTH + GQA_WIDTH].T).T
    rest = w[:, 3 * NA_WIDTH + GQA_WIDTH:]
    return jnp.concatenate([qa, kva, qb, rest], axis=1).astype(BF16)


def _gqa_out_order(a):
    rest = a.shape[1:]
    a = a.reshape((GQA_KV_HEADS, GQA_GROUP, HEAD_DIM) + rest)
    return jnp.swapaxes(a, 0, 1).reshape((GQA_WIDTH,) + rest)


def _dest_tiles(dest_t, tm):
    n = dest_t.shape[1]
    return dest_t.reshape(TOP_K, n // tm, tm).transpose(1, 0, 2).reshape(n // tm, 1, TOP_K * tm)


def kernel(x_prompt, x_sample, w_in, rpb, q_norm_g, k_norm_g, g_out_na, g_out_gqa, w_o, ln1_g, ln1_b,
           router_w, router_b, w1, b1, w2, b2, ln2_g, ln2_b):
    assert GQA_KV_HEADS == 2 and KV_WIDTH == LANES
    xs_in = [x_prompt, x_sample]
    l = 0
    w_proj = _prep_w_in(w_in[l])
    scale = HEAD_DIM ** -0.5
    gq = jnp.tile(q_norm_g[l] * scale, 2).reshape(1, LANES)
    gk = jnp.tile(k_norm_g[l], 2).reshape(1, LANES)
    bias = _na_bias_tables(rpb[l])
    woa = w_o[l][:NA_WIDTH].astype(BF16)
    wob = _gqa_out_order(w_o[l][NA_WIDTH:]).astype(BF16)
    ga = g_out_na[l].reshape(1, NA_WIDTH)
    gb = _gqa_out_order(g_out_gqa[l]).reshape(1, GQA_WIDTH)
    ln1g, ln1b = ln1_g[l].reshape(1, D_MODEL), ln1_b[l].reshape(1, D_MODEL)
    ln2g, ln2b = ln2_g[l].reshape(1, D_MODEL), ln2_b[l].reshape(1, D_MODEL)
    rw_t = router_w[l].T.astype(BF16)
    rb = router_b[l].reshape(N_EXPERTS, 1)
    tri = (jnp.arange(MIX_TM)[:, None] < jnp.arange(MIX_TM)[None, :]).astype(BF16)
    w1p = _w1_prep(w1[l])
    w2b = w2[l].astype(BF16)
    b1p = jnp.swapaxes(b1[l].reshape(N_EXPERTS, 2 * D_FF // W1_GROUP, LANES, 2), 2, 3).reshape(N_EXPERTS, 1, 2 * D_FF)
    b2r = b2[l].reshape(N_EXPERTS, 1, D_MODEL)

    x1s, idxs, gates, ranks = [], [], [], []
    cnt = jnp.zeros((N_EXPERTS, LANES), F32)
    for x in xs_in:
        b, t, _ = x.shape
        x2d = x.reshape(b * t, D_MODEL)
        cos, sin = _rope_tables(t)
        qa, ka, va, qb, kb, vlo, vhi = _proj(x2d, t, w_proj, cos, sin, gq, gk)
        sh = lambda a: a.reshape(b, t, a.shape[-1])
        oa = _na(sh(qa), sh(ka), sh(va), bias).reshape(b * t, NA_WIDTH)
        ob = _gqa(sh(qb), sh(kb), sh(vlo), sh(vhi)).reshape(b * t, GQA_WIDTH)
        x1, idx_t, gate_t, rank_t, cnt = _mix(oa, ob, x2d, woa, wob, ga, gb, ln1g, ln1b, rw_t, rb, tri, cnt)
        x1s.append(x1)
        idxs.append(idx_t)
        gates.append(gate_t)
        ranks.append(rank_t)

    counts = cnt[:, 0].astype(jnp.int32)
    nblk = (counts + EXP_BLK - 1) // EXP_BLK
    blk_end = jnp.cumsum(nblk)
    pad_start = (blk_end - nblk) * EXP_BLK
    n_assign = TOP_K * sum(x.shape[0] * x.shape[1] for x in xs_in)
    n_blocks = -(-n_assign // EXP_BLK) + N_EXPERTS
    n_slots = n_blocks * EXP_BLK
    blocks = jnp.arange(n_blocks, dtype=jnp.int32)
    block_expert = jnp.minimum(jnp.sum((blk_end[None, :] <= blocks[:, None]).astype(jnp.int32), axis=1),
                               N_EXPERTS - 1)
    n_used = blk_end[-1:].astype(jnp.int32)
    tail = ((blk_end - 1) * (EXP_BLK * TOK_ROWS)).astype(jnp.int32)
    has_tail = (counts % EXP_BLK != 0).astype(jnp.int32)
    dests = [_slots(pad_start.astype(jnp.int32), idx_t, rank_t) for idx_t, rank_t in zip(idxs, ranks)]

    slots = None
    for x1, dest_t in zip(x1s, dests):
        slots = _dispatch(tail, has_tail, _dest_tiles(dest_t, DISP_TM), x1, slots, n_slots)
    ys = _experts(block_expert, n_used, slots, w1p, w2b, b1p, b2r)
    outs = []
    for x, x1, dest_t, gate_t in zip(xs_in, x1s, dests, gates):
        y = _combine(_dest_tiles(dest_t, COMB_TM), x1, gate_t.T, ln2g, ln2b, ys)
        outs.append(y.reshape(x.shape))
    return tuple(outs)
```

```python
import functools

import jax
import jax.numpy as jnp
from jax import lax
from jax.experimental import pallas as pl
from jax.experimental.pallas import tpu as pltpu

D_MODEL = 1024
GRID_W = 64
HEAD_DIM = 64
NA_HEADS = 8
GQA_HEADS = 8
GQA_KV_HEADS = 2
GQA_GROUP = GQA_HEADS // GQA_KV_HEADS
NA_WIDTH = NA_HEADS * HEAD_DIM
GQA_WIDTH = GQA_HEADS * HEAD_DIM
KV_WIDTH = GQA_KV_HEADS * HEAD_DIM
NA_WIN_R = 8
NA_WIN_C = 16
ROPE_AXIS_DIM = HEAD_DIM // 2
ROPE_THETA = 10000.0
N_EXPERTS = 32
TOP_K = 4
D_FF = D_MODEL
SWIGLU_ALPHA = 1.702
SWIGLU_LIMIT = 7.0
DEPTH = 1
DN_ALPHA = (2.0 * DEPTH) ** 0.25
NEG_INF = -1e30
RMS_EPS = 1e-6
LN_EPS = 1e-5

LANES = 128
PROJ_COLS = 3 * NA_WIDTH + GQA_WIDTH + 2 * KV_WIDTH

PROJ_TM = 1024
NA_ROWS = 8
NA_PASS_HEADS = 4
GQA_SCORE_ELEMS = 1024 * 2048
MIX_TM = 512
DISP_TM = 512
COMB_TM = 512
EXP_BLK = 512
W1_GROUP = 2 * LANES
SLOT_TM = 2048
V7X_VMEM_BYTES = 64 * 1024 * 1024
VMEM_LIMIT = V7X_VMEM_BYTES - 8 * 1024 * 1024

F32 = jnp.float32
BF16 = jnp.bfloat16

TOK_ROWS = D_MODEL // LANES


def _tok_load(ref, n, lead=()):
    return jnp.concatenate([ref[lead + (pl.ds(s, n, stride=TOK_ROWS), slice(None))] for s in range(TOK_ROWS)],
                           axis=1)


def _tok_store(ref, val):
    n = val.shape[0]
    for s in range(TOK_ROWS):
        ref[pl.ds(s, n, stride=TOK_ROWS), :] = val[:, s * LANES:(s + 1) * LANES]


def _cparams(sem):
    return pltpu.CompilerParams(dimension_semantics=sem, vmem_limit_bytes=VMEM_LIMIT)


def _rope(y, cos, sin, first_half):
    half = ROPE_AXIS_DIM // 2
    partner = jnp.where(first_half, pltpu.roll(y, LANES - half, 1), pltpu.roll(y, half, 1))
    return y * cos + partner * sin


def _proj_kernel(x_ref, w_ref, cos_ref, sin_ref, gq_ref, gk_ref,
                 qa_ref, ka_ref, va_ref, qb_ref, kb_ref, vlo_ref, vhi_ref):
    x = x_ref[...].astype(BF16)

    def cols(c0, width):
        return jnp.dot(x, w_ref[:, c0:c0 + width], preferred_element_type=F32)

    qa_ref[...] = cols(0, NA_WIDTH).astype(BF16)
    ka_ref[...] = cols(NA_WIDTH, NA_WIDTH).astype(BF16)
    va_ref[...] = cols(2 * NA_WIDTH, NA_WIDTH).astype(BF16)
    cos = cos_ref[...]
    sin = sin_ref[...]
    lane = lax.broadcasted_iota(jnp.int32, cos.shape, 1)
    first_half = (lane % ROPE_AXIS_DIM) < ROPE_AXIS_DIM // 2
    lo = lane < HEAD_DIM

    def norm_rope(y, g):
        y2 = y * y
        ms_lo = jnp.sum(jnp.where(lo, y2, 0.0), axis=-1, keepdims=True)
        ms_hi = jnp.sum(jnp.where(lo, 0.0, y2), axis=-1, keepdims=True)
        ms = jnp.where(lo, ms_lo, ms_hi) * (1.0 / HEAD_DIM)
        return _rope(y * lax.rsqrt(ms + RMS_EPS) * g, cos, sin, first_half).astype(BF16)

    base = 3 * NA_WIDTH
    q = cols(base, GQA_WIDTH)
    gq = gq_ref[...]
    for g in range(GQA_GROUP):
        qb_ref[:, g * LANES:(g + 1) * LANES] = norm_rope(q[:, g * LANES:(g + 1) * LANES], gq)
    base += GQA_WIDTH
    kv = cols(base, 2 * KV_WIDTH)
    kb_ref[...] = norm_rope(kv[:, :KV_WIDTH], gk_ref[...])
    v = kv[:, KV_WIDTH:]
    vlo_ref[...] = jnp.where(lo, v, 1.0).astype(BF16)
    vhi_ref[...] = jnp.where(lo, 1.0, v).astype(BF16)


def _proj(x2d, seq_len, w, cos, sin, gq, gk):
    n = x2d.shape[0]
    tm = min(PROJ_TM, seq_len)
    assert n % tm == 0 and seq_len % tm == 0
    pos_blocks = seq_len // tm
    row = lambda i: (i, 0)
    const = lambda i: (0, 0)
    outs = [(NA_WIDTH, BF16)] * 3 + [(GQA_WIDTH, BF16)] + [(KV_WIDTH, BF16)] * 3
    return pl.pallas_call(
        _proj_kernel,
        grid=(n // tm,),
        in_specs=[
            pl.BlockSpec((tm, D_MODEL), row),
            pl.BlockSpec((D_MODEL, PROJ_COLS), const),
            pl.BlockSpec((tm, LANES), lambda i: (i % pos_blocks, 0)),
            pl.BlockSpec((tm, LANES), lambda i: (i % pos_blocks, 0)),
            pl.BlockSpec((1, LANES), const),
            pl.BlockSpec((1, LANES), const),
        ],
        out_specs=[pl.BlockSpec((tm, c), row) for c, _ in outs],
        out_shape=[jax.ShapeDtypeStruct((n, c), dt) for c, dt in outs],
        compiler_params=_cparams(("parallel",)),
        name="proj",
    )(x2d, w, cos, sin, gq, gk)


def _na_kernel(q_ref, k_ref, v_ref, bias_ref, o_ref, *, rows):
    j = pl.program_id(1)
    width = NA_PASS_HEADS * HEAD_DIM
    lane_head = lax.broadcasted_iota(jnp.int32, (GRID_W, width), 1) // HEAD_DIM

    def one_row(rr, carry):
        r = j * NA_ROWS + rr
        rs = jnp.clip(r - NA_WIN_R // 2, 0, rows - NA_WIN_R)
        var = r - rs
        k0 = pl.multiple_of(rs * GRID_W, GRID_W)
        q0 = pl.multiple_of(rr * GRID_W, GRID_W)
        for g in range(NA_HEADS // NA_PASS_HEADS):
            cols = slice(g * width, (g + 1) * width)
            heads = range(g * NA_PASS_HEADS, (g + 1) * NA_PASS_HEADS)
            q = q_ref[0, pl.ds(q0, GRID_W), cols]
            zero = jnp.zeros_like(q)
            qs = jnp.concatenate([jnp.where(lane_head == a, q, zero) for a in range(NA_PASS_HEADS)], axis=0)
            k = k_ref[0, pl.ds(k0, NA_WIN_R * GRID_W), cols]
            v = v_ref[0, pl.ds(k0, NA_WIN_R * GRID_W), cols]
            s = lax.dot_general(qs, k, (((1,), (1,)), ((), ())), preferred_element_type=F32)
            s = s + jnp.concatenate([bias_ref[var, h] for h in heads], axis=0)
            m = jnp.max(s, axis=-1, keepdims=True)
            p = jnp.exp(s - m)
            l = jnp.sum(p, axis=-1, keepdims=True)
            os = jnp.dot(p.astype(BF16), v, preferred_element_type=F32) / l
            o = os[:GRID_W]
            for a in range(1, NA_PASS_HEADS):
                o = jnp.where(lane_head == a, os[a * GRID_W:(a + 1) * GRID_W], o)
            o_ref[0, pl.ds(q0, GRID_W), cols] = o.astype(BF16)
        return carry

    lax.fori_loop(0, NA_ROWS, one_row, 0, unroll=True)


def _na(qa, ka, va, bias):
    b, t, _ = qa.shape
    rows = t // GRID_W
    assert rows >= NA_WIN_R and rows % NA_ROWS == 0
    tq = NA_ROWS * GRID_W
    return pl.pallas_call(
        functools.partial(_na_kernel, rows=rows),
        grid=(b, rows // NA_ROWS),
        in_specs=[
            pl.BlockSpec((1, tq, NA_WIDTH), lambda i, j: (i, j, 0)),
            pl.BlockSpec((1, t, NA_WIDTH), lambda i, j: (i, 0, 0)),
            pl.BlockSpec((1, t, NA_WIDTH), lambda i, j: (i, 0, 0)),
            pl.BlockSpec(bias.shape, lambda i, j: (0, 0, 0, 0), pipeline_mode=pl.Buffered(1)),
        ],
        out_specs=pl.BlockSpec((1, tq, NA_WIDTH), lambda i, j: (i, j, 0)),
        out_shape=jax.ShapeDtypeStruct((b, t, NA_WIDTH), BF16),
        compiler_params=_cparams(("parallel", "parallel")),
        name="na",
    )(qa, ka, va, bias)


def _na_bias_tables(rpb):
    off = jnp.arange(NA_WIN_R)
    jrow = jnp.arange(NA_WIN_R)
    dr = jrow[None, :] - off[:, None] + (NA_WIN_R - 1)
    c = jnp.arange(GRID_W)
    cs = jnp.clip(c - NA_WIN_C // 2, 0, GRID_W - NA_WIN_C)
    col_ok = (c[None, :] >= cs[:, None]) & (c[None, :] < cs[:, None] + NA_WIN_C)
    dc = jnp.clip(c[None, :] - c[:, None], -(NA_WIN_C - 1), NA_WIN_C - 1) + (NA_WIN_C - 1)
    sel_r = (dr[:, :, None] == jnp.arange(2 * NA_WIN_R - 1)[None, None, :]).astype(F32)
    sel_c = (dc[:, :, None] == jnp.arange(2 * NA_WIN_C - 1)[None, None, :]).astype(F32)
    bias = jnp.einsum('vja,hab,qkb->hvqjk', sel_r, rpb.astype(F32), sel_c, precision=lax.Precision.HIGHEST)
    bias = jnp.where(col_ok[None, None, :, None, :], bias, NEG_INF)
    return jnp.moveaxis(bias, 0, 1).reshape(NA_WIN_R, NA_HEADS, GRID_W, NA_WIN_R * GRID_W)


def _gqa_kernel(q_ref, k_ref, vlo_ref, vhi_ref, o_ref):
    k = k_ref[0]
    lane = lax.broadcasted_iota(jnp.int32, (q_ref.shape[1], LANES), 1)
    lo = lane < HEAD_DIM

    def head(q, v):
        s = lax.dot_general(q, k, (((1,), (1,)), ((), ())), preferred_element_type=F32)
        p = jnp.exp(s - jnp.max(s, axis=-1, keepdims=True)).astype(BF16)
        ov = jnp.dot(p, v, preferred_element_type=F32)
        return ov / pltpu.roll(ov, HEAD_DIM, 1)

    for g in range(GQA_GROUP):
        q = q_ref[0, :, g * LANES:(g + 1) * LANES]
        zero = jnp.zeros_like(q)
        o = jnp.where(lo, head(jnp.where(lo, q, zero), vlo_ref[0]), head(jnp.where(lo, zero, q), vhi_ref[0]))
        o_ref[0, :, g * LANES:(g + 1) * LANES] = o.astype(BF16)


def _gqa(qb, kb, vlo, vhi):
    b, t, _ = qb.shape
    tq = min(t, GQA_SCORE_ELEMS // t)
    assert t % tq == 0 and tq % 8 == 0
    return pl.pallas_call(
        _gqa_kernel,
        grid=(b, t // tq),
        in_specs=[
            pl.BlockSpec((1, tq, GQA_WIDTH), lambda i, j: (i, j, 0)),
            pl.BlockSpec((1, t, KV_WIDTH), lambda i, j: (i, 0, 0)),
            pl.BlockSpec((1, t, KV_WIDTH), lambda i, j: (i, 0, 0)),
            pl.BlockSpec((1, t, KV_WIDTH), lambda i, j: (i, 0, 0)),
        ],
        out_specs=pl.BlockSpec((1, tq, GQA_WIDTH), lambda i, j: (i, j, 0)),
        out_shape=jax.ShapeDtypeStruct((b, t, GQA_WIDTH), BF16),
        compiler_params=_cparams(("parallel", "parallel")),
        name="gqa",
    )(qb, kb, vlo, vhi)


def _layer_norm(z, g, b):
    mu = jnp.mean(z, axis=-1, keepdims=True)
    zc = z - mu
    var = jnp.mean(zc * zc, axis=-1, keepdims=True)
    return zc * lax.rsqrt(var + LN_EPS) * g + b


def _rms(o, g):
    return o * lax.rsqrt(jnp.mean(o * o, axis=-1, keepdims=True) + RMS_EPS) * g


def _mix_kernel(oa_ref, ob_ref, x_ref, woa_ref, wob_ref, ga_ref, gb_ref, lng_ref, lnb_ref,
                rw_ref, rb_ref, tri_ref, cnt_in_ref,
                x1_ref, idx_ref, gate_ref, rank_ref, cnt_ref, carry_ref):
    @pl.when(pl.program_id(0) == 0)
    def _():
        carry_ref[...] = cnt_in_ref[...]

    na = _rms(oa_ref[...].astype(F32), ga_ref[...]).astype(BF16)
    nb = _rms(ob_ref[...].astype(F32), gb_ref[...]).astype(BF16)
    mixed = (jnp.dot(na, woa_ref[...], preferred_element_type=F32)
             + jnp.dot(nb, wob_ref[...], preferred_element_type=F32))
    x1 = _layer_norm(DN_ALPHA * x_ref[...] + mixed, lng_ref[...], lnb_ref[...])
    _tok_store(x1_ref, x1)

    logits = lax.dot_general(rw_ref[...], x1.astype(BF16), (((1,), (1,)), ((), ())),
                             preferred_element_type=F32) + rb_ref[...]
    tm = logits.shape[1]
    eidx = lax.broadcasted_iota(jnp.int32, (N_EXPERTS, tm), 0).astype(F32)
    work = logits
    vals, idxs, hots = [], [], []
    for _ in range(TOP_K):
        m = jnp.max(work, axis=0, keepdims=True)
        sel = jnp.min(jnp.where(work == m, eidx, float(N_EXPERTS)), axis=0, keepdims=True)
        hot = eidx == sel
        vals.append(m)
        idxs.append(sel)
        hots.append(hot)
        work = jnp.where(hot, -jnp.inf, work)
    es = [jnp.exp(v - vals[0]) for v in vals]
    den = es[0] + es[1] + es[2] + es[3]
    gate_ref[...] = jnp.concatenate([e / den for e in es], axis=0)
    idx_ref[...] = jnp.concatenate(idxs, axis=0).astype(jnp.int32)

    hot_all = hots[0] | hots[1] | hots[2] | hots[3]
    onehot = jnp.where(hot_all, 1.0, 0.0)
    before = jnp.dot(onehot.astype(BF16), tri_ref[...], preferred_element_type=F32)
    before = before + carry_ref[:, 0:1]
    ranks = [jnp.sum(jnp.where(hot, before, 0.0), axis=0, keepdims=True) for hot in hots]
    rank_ref[...] = jnp.concatenate(ranks, axis=0).astype(jnp.int32)
    carry_ref[...] = carry_ref[...] + jnp.sum(onehot, axis=1, keepdims=True)
    cnt_ref[...] = carry_ref[...]


def _mix(oa, ob, x2d, woa, wob, ga, gb, lng, lnb, rw_t, rb, tri, cnt_in):
    n = x2d.shape[0]
    tm = MIX_TM
    assert n % tm == 0
    row = lambda i: (i, 0)
    col = lambda i: (0, i)
    const = lambda i: (0, 0)
    full = lambda a: pl.BlockSpec(a.shape, const)
    return pl.pallas_call(
        _mix_kernel,
        grid=(n // tm,),
        in_specs=[
            pl.BlockSpec((tm, NA_WIDTH), row),
            pl.BlockSpec((tm, GQA_WIDTH), row),
            pl.BlockSpec((tm, D_MODEL), row),
            full(woa), full(wob), full(ga), full(gb), full(lng), full(lnb),
            full(rw_t), full(rb), full(tri), full(cnt_in),
        ],
        out_specs=[
            pl.BlockSpec((tm * TOK_ROWS, LANES), row),
            pl.BlockSpec((TOP_K, tm), col),
            pl.BlockSpec((TOP_K, tm), col),
            pl.BlockSpec((TOP_K, tm), col),
            pl.BlockSpec((N_EXPERTS, LANES), const),
        ],
        out_shape=[
            jax.ShapeDtypeStruct((n * TOK_ROWS, LANES), F32),
            jax.ShapeDtypeStruct((TOP_K, n), jnp.int32),
            jax.ShapeDtypeStruct((TOP_K, n), F32),
            jax.ShapeDtypeStruct((TOP_K, n), jnp.int32),
            jax.ShapeDtypeStruct((N_EXPERTS, LANES), F32),
        ],
        scratch_shapes=[pltpu.VMEM((N_EXPERTS, LANES), F32)],
        compiler_params=_cparams(("arbitrary",)),
        name="mix",
    )(oa, ob, x2d, woa, wob, ga, gb, lng, lnb, rw_t, rb, tri, cnt_in)


def _slot_kernel(pad_ref, idx_ref, rank_ref, o_ref):
    idx = idx_ref[...]
    start = jnp.zeros_like(idx)
    for e in range(N_EXPERTS):
        start = jnp.where(idx == e, pad_ref[e], start)
    o_ref[...] = (start + rank_ref[...]) * TOK_ROWS


def _slots(pad_start, idx_t, rank_t):
    n = idx_t.shape[1]
    tm = min(SLOT_TM, n)
    assert n % tm == 0
    col = lambda i, *_: (0, i)
    return pl.pallas_call(
        _slot_kernel,
        grid_spec=pltpu.PrefetchScalarGridSpec(
            num_scalar_prefetch=1,
            grid=(n // tm,),
            in_specs=[pl.BlockSpec((TOP_K, tm), col), pl.BlockSpec((TOP_K, tm), col)],
            out_specs=pl.BlockSpec((TOP_K, tm), col),
        ),
        out_shape=jax.ShapeDtypeStruct((TOP_K, n), jnp.int32),
        compiler_params=_cparams(("parallel",)),
        name="slots",
    )(pad_start, idx_t, rank_t)


def _dispatch_kernel(tail_ref, has_tail_ref, dest_ref, x_ref, *rest, zero_tails):
    if zero_tails:
        xs_ref, zeros_ref, sem = rest
    else:
        _, xs_ref, sem = rest
    tm = x_ref.shape[0] // TOK_ROWS

    if zero_tails:
        @pl.when(pl.program_id(0) == 0)
        def _():
            zeros_ref[...] = jnp.zeros_like(zeros_ref)

            def tail_copy(e):
                start = pl.multiple_of(tail_ref[e], EXP_BLK * TOK_ROWS)
                return pltpu.make_async_copy(zeros_ref, xs_ref.at[pl.ds(start, EXP_BLK * TOK_ROWS), :], sem)

            for e in range(N_EXPERTS):
                @pl.when(has_tail_ref[e] != 0)
                def _():
                    tail_copy(e).start()
            for e in range(N_EXPERTS):
                @pl.when(has_tail_ref[e] != 0)
                def _():
                    tail_copy(e).wait()

    def row_copy(i, k):
        src = pl.multiple_of(i * TOK_ROWS, TOK_ROWS)
        dst = pl.multiple_of(dest_ref[0, 0, k * tm + i], TOK_ROWS)
        return pltpu.make_async_copy(x_ref.at[pl.ds(src, TOK_ROWS), :], xs_ref.at[pl.ds(dst, TOK_ROWS), :], sem)

    def issue(i, c):
        for k in range(TOP_K):
            row_copy(i, k).start(priority=k % 2)
        return c

    lax.fori_loop(0, tm, issue, 0, unroll=4)
    for k in range(TOP_K):
        pltpu.make_async_copy(x_ref, xs_ref.at[pl.ds(0, tm * TOK_ROWS), :], sem).wait()


def _dispatch(tail, has_tail, dest_tiles, x1, xs_prev, n_slots):
    n = x1.shape[0] // TOK_ROWS
    tm = DISP_TM
    assert n % tm == 0
    zero_tails = xs_prev is None
    in_specs = [
        pl.BlockSpec((1, 1, TOP_K * tm), lambda i, *_: (i, 0, 0), memory_space=pltpu.SMEM),
        pl.BlockSpec((tm * TOK_ROWS, LANES), lambda i, *_: (i, 0)),
    ]
    args = [dest_tiles, x1]
    scratch = []
    aliases = {}
    if zero_tails:
        scratch.append(pltpu.VMEM((EXP_BLK * TOK_ROWS, LANES), F32))
    else:
        in_specs.append(pl.BlockSpec(memory_space=pl.ANY))
        args.append(xs_prev)
        aliases = {4: 0}
    scratch.append(pltpu.SemaphoreType.DMA(()))
    return pl.pallas_call(
        functools.partial(_dispatch_kernel, zero_tails=zero_tails),
        grid_spec=pltpu.PrefetchScalarGridSpec(
            num_scalar_prefetch=2,
            grid=(n // tm,),
            in_specs=in_specs,
            out_specs=pl.BlockSpec(memory_space=pl.ANY),
            scratch_shapes=scratch,
        ),
        out_shape=jax.ShapeDtypeStruct((n_slots * TOK_ROWS, LANES), F32),
        input_output_aliases=aliases,
        compiler_params=pltpu.CompilerParams(dimension_semantics=("arbitrary",), vmem_limit_bytes=VMEM_LIMIT,
                                             has_side_effects=True),
        name="dispatch",
    )(tail, has_tail, *args)


def _w1_prep_kernel(w_ref, perm_ref, o_ref):
    for c in range(w_ref.shape[2] // W1_GROUP):
        cols = slice(c * W1_GROUP, (c + 1) * W1_GROUP)
        w = w_ref[0, :, cols].astype(BF16)
        o_ref[0, :, cols] = jnp.dot(w, perm_ref[...], preferred_element_type=F32).astype(BF16)


def _w1_prep(w1):
    e, d, f2 = w1.shape
    half = f2 // 2
    j = jnp.arange(W1_GROUP)
    dst = jnp.where(j % 2 == 0, j // 2, LANES + j // 2)
    perm = (dst[:, None] == jnp.arange(W1_GROUP)[None, :]).astype(BF16)
    return pl.pallas_call(
        _w1_prep_kernel,
        grid=(e, 2),
        in_specs=[pl.BlockSpec((1, d, half), lambda i, j: (i, 0, j)),
                  pl.BlockSpec((W1_GROUP, W1_GROUP), lambda i, j: (0, 0))],
        out_specs=pl.BlockSpec((1, d, half), lambda i, j: (i, 0, j)),
        out_shape=jax.ShapeDtypeStruct((e, d, f2), BF16),
        compiler_params=_cparams(("parallel", "parallel")),
        name="w1prep",
    )(w1, perm)


def _expert_kernel(be_ref, nused_ref, xs_ref, w1_ref, w2_ref, b1_ref, b2_ref, o_ref):
    @pl.when(pl.program_id(0) < nused_ref[0])
    def _():
        x = _tok_load(xs_ref, EXP_BLK).astype(BF16)
        h = jnp.dot(x, w1_ref[0], preferred_element_type=F32) + b1_ref[0]
        groups = range(h.shape[1] // W1_GROUP)
        glu = jnp.concatenate([h[:, c * W1_GROUP: c * W1_GROUP + LANES] for c in groups], axis=1)
        lin = jnp.concatenate([h[:, c * W1_GROUP + LANES: (c + 1) * W1_GROUP] for c in groups], axis=1)
        glu = jnp.minimum(glu, SWIGLU_LIMIT)
        lin = jnp.clip(lin, -SWIGLU_LIMIT, SWIGLU_LIMIT)
        act = glu * jax.nn.sigmoid(SWIGLU_ALPHA * glu) * (lin + 1.0)
        _tok_store(o_ref, jnp.dot(act.astype(BF16), w2_ref[0], preferred_element_type=F32) + b2_ref[0])


def _experts(block_expert, n_used, xs, w1p, w2, b1p, b2):
    n_slots = xs.shape[0] // TOK_ROWS
    n_blocks = n_slots // EXP_BLK
    slot = lambda i, be, nu: (jnp.minimum(i, nu[0] - 1), 0)
    wsel = lambda i, be, nu: (be[i], 0, 0)
    return pl.pallas_call(
        _expert_kernel,
        grid_spec=pltpu.PrefetchScalarGridSpec(
            num_scalar_prefetch=2,
            grid=(n_blocks,),
            in_specs=[
                pl.BlockSpec((EXP_BLK * TOK_ROWS, LANES), slot),
                pl.BlockSpec((1, D_MODEL, 2 * D_FF), wsel),
                pl.BlockSpec((1, D_FF, D_MODEL), wsel),
                pl.BlockSpec((1, 1, 2 * D_FF), wsel),
                pl.BlockSpec((1, 1, D_MODEL), wsel),
            ],
            out_specs=pl.BlockSpec((EXP_BLK * TOK_ROWS, LANES), slot),
        ),
        out_shape=jax.ShapeDtypeStruct((n_slots * TOK_ROWS, LANES), F32),
        compiler_params=_cparams(("arbitrary",)),
        name="experts",
    )(block_expert, n_used, xs, w1p, w2, b1p, b2)


def _combine_kernel(dest_ref, dest_next_ref, x1_ref, gate_ref, lng_ref, lnb_ref, ys_ref, o_ref, buf_ref, sems):
    tm = x1_ref.shape[0] // TOK_ROWS
    step = pl.program_id(0)
    slot = step % 2

    def gather(idx_ref, into):
        def issue(i, c):
            for k in range(TOP_K):
                src = pl.multiple_of(idx_ref[0, 0, k * tm + i], TOK_ROWS)
                dst = pl.multiple_of(i * TOK_ROWS, TOK_ROWS)
                pltpu.make_async_copy(ys_ref.at[pl.ds(src, TOK_ROWS), :],
                                      buf_ref.at[into, k, pl.ds(dst, TOK_ROWS), :], sems.at[into]
                                      ).start(priority=k % 2)
            return c

        lax.fori_loop(0, tm, issue, 0, unroll=4)

    @pl.when(step == 0)
    def _():
        gather(dest_ref, slot)

    @pl.when(step + 1 < pl.num_programs(0))
    def _():
        gather(dest_next_ref, 1 - slot)

    for k in range(TOP_K):
        pltpu.make_async_copy(ys_ref.at[pl.ds(0, tm * TOK_ROWS), :], buf_ref.at[slot, k], sems.at[slot]).wait()
    gate = gate_ref[...]
    y = _tok_load(buf_ref, tm, (slot, 0)) * gate[:, 0:1]
    for k in range(1, TOP_K):
        y = y + _tok_load(buf_ref, tm, (slot, k)) * gate[:, k:k + 1]
    o_ref[...] = _layer_norm(DN_ALPHA * _tok_load(x1_ref, tm) + y, lng_ref[...], lnb_ref[...])


def _combine(dest_tiles, x1, gate, lng, lnb, ys):
    n = x1.shape[0] // TOK_ROWS
    tm = COMB_TM
    assert n % tm == 0
    n_tiles = n // tm
    return pl.pallas_call(
        _combine_kernel,
        grid=(n_tiles,),
        in_specs=[
            pl.BlockSpec((1, 1, TOP_K * tm), lambda i: (i, 0, 0), memory_space=pltpu.SMEM),
            pl.BlockSpec((1, 1, TOP_K * tm), lambda i: (jnp.minimum(i + 1, n_tiles - 1), 0, 0),
                         memory_space=pltpu.SMEM),
            pl.BlockSpec((tm * TOK_ROWS, LANES), lambda i: (i, 0)),
            pl.BlockSpec((tm, TOP_K), lambda i: (i, 0)),
            pl.BlockSpec((1, D_MODEL), lambda i: (0, 0)),
            pl.BlockSpec((1, D_MODEL), lambda i: (0, 0)),
            pl.BlockSpec(memory_space=pl.ANY),
        ],
        out_specs=pl.BlockSpec((tm, D_MODEL), lambda i: (i, 0)),
        out_shape=jax.ShapeDtypeStruct((n, D_MODEL), F32),
        scratch_shapes=[pltpu.VMEM((2, TOP_K, tm * TOK_ROWS, LANES), F32), pltpu.SemaphoreType.DMA((2,))],
        compiler_params=_cparams(("arbitrary",)),
        name="combine",
    )(dest_tiles, dest_tiles, x1, gate, lng, lnb, ys)


def _rope_tables(t):
    pos = jnp.arange(t)
    row = (pos // GRID_W).astype(F32)
    col = (pos % GRID_W).astype(F32)
    inv = ROPE_THETA ** (-jnp.arange(0, ROPE_AXIS_DIM, 2, dtype=F32) / ROPE_AXIS_DIM)
    ar = row[:, None] * inv[None, :]
    ac = col[:, None] * inv[None, :]
    cos = jnp.concatenate([jnp.cos(ar), jnp.cos(ar), jnp.cos(ac), jnp.cos(ac)], axis=1)
    sin = jnp.concatenate([-jnp.sin(ar), jnp.sin(ar), -jnp.sin(ac), jnp.sin(ac)], axis=1)
    return jnp.tile(cos, (1, 2)), jnp.tile(sin, (1, 2))


def _prep_w_in(w):
    scale = HEAD_DIM ** -0.5
    qa = w[:, :NA_WIDTH] * scale
    kva = w[:, NA_WIDTH:3 * NA_WIDTH]
    qb = _gqa_out_order(w[:, 3 * NA_WIDTH:3 * NA_WIDTH + GQA_WIDTH].T).T
    rest = w[:, 3 * NA_WIDTH + GQA_WIDTH:]
    return jnp.concatenate([qa, kva, qb, rest], axis=1).astype(BF16)


def _gqa_out_order(a):
    rest = a.shape[1:]
    a = a.reshape((GQA_KV_HEADS, GQA_GROUP, HEAD_DIM) + rest)
    return jnp.swapaxes(a, 0, 1).reshape((GQA_WIDTH,) + rest)


def _dest_tiles(dest_t, tm):
    n = dest_t.shape[1]
    return dest_t.reshape(TOP_K, n // tm, tm).transpose(1, 0, 2).reshape(n // tm, 1, TOP_K * tm)


def kernel(x_prompt, x_sample, w_in, rpb, q_norm_g, k_norm_g, g_out_na, g_out_gqa, w_o, ln1_g, ln1_b,
           router_w, router_b, w1, b1, w2, b2, ln2_g, ln2_b):
    assert GQA_KV_HEADS == 2 and KV_WIDTH == LANES
    xs_in = [x_prompt, x_sample]
    l = 0
    w_proj = _prep_w_in(w_in[l])
    scale = HEAD_DIM ** -0.5
    gq = jnp.tile(q_norm_g[l] * scale, 2).reshape(1, LANES)
    gk = jnp.tile(k_norm_g[l], 2).reshape(1, LANES)
    bias = _na_bias_tables(rpb[l])
    woa = w_o[l][:NA_WIDTH].astype(BF16)
    wob = _gqa_out_order(w_o[l][NA_WIDTH:]).astype(BF16)
    ga = g_out_na[l].reshape(1, NA_WIDTH)
    gb = _gqa_out_order(g_out_gqa[l]).reshape(1, GQA_WIDTH)
    ln1g, ln1b = ln1_g[l].reshape(1, D_MODEL), ln1_b[l].reshape(1, D_MODEL)
    ln2g, ln2b = ln2_g[l].reshape(1, D_MODEL), ln2_b[l].reshape(1, D_MODEL)
    rw_t = router_w[l].T.astype(BF16)
    rb = router_b[l].reshape(N_EXPERTS, 1)
    tri = (jnp.arange(MIX_TM)[:, None] < jnp.arange(MIX_TM)[None, :]).astype(BF16)
    w1p = _w1_prep(w1[l])
    w2b = w2[l].astype(BF16)
    b1p = jnp.swapaxes(b1[l].reshape(N_EXPERTS, 2 * D_FF // W1_GROUP, LANES, 2), 2, 3).reshape(N_EXPERTS, 1, 2 * D_FF)
    b2r = b2[l].reshape(N_EXPERTS, 1, D_MODEL)

    x1s, idxs, gates, ranks = [], [], [], []
    cnt = jnp.zeros((N_EXPERTS, LANES), F32)
    for x in xs_in:
        b, t, _ = x.shape
        x2d = x.reshape(b * t, D_MODEL)
        cos, sin = _rope_tables(t)
        qa, ka, va, qb, kb, vlo, vhi = _proj(x2d, t, w_proj, cos, sin, gq, gk)
        sh = lambda a: a.reshape(b, t, a.shape[-1])
        oa = _na(sh(qa), sh(ka), sh(va), bias).reshape(b * t, NA_WIDTH)
        ob = _gqa(sh(qb), sh(kb), sh(vlo), sh(vhi)).reshape(b * t, GQA_WIDTH)
        x1, idx_t, gate_t, rank_t, cnt = _mix(oa, ob, x2d, woa, wob, ga, gb, ln1g, ln1b, rw_t, rb, tri, cnt)
        x1s.append(x1)
        idxs.append(idx_t)
        gates.append(gate_t)
        ranks.append(rank_t)

    counts = cnt[:, 0].astype(jnp.int32)
    nblk = (counts + EXP_BLK - 1) // EXP_BLK
    blk_end = jnp.cumsum(nblk)
    pad_start = (blk_end - nblk) * EXP_BLK
    n_assign = TOP_K * sum(x.shape[0] * x.shape[1] for x in xs_in)
    n_blocks = -(-n_assign // EXP_BLK) + N_EXPERTS
    n_slots = n_blocks * EXP_BLK
    blocks = jnp.arange(n_blocks, dtype=jnp.int32)
    block_expert = jnp.minimum(jnp.sum((blk_end[None, :] <= blocks[:, None]).astype(jnp.int32), axis=1),
                               N_EXPERTS - 1)
    n_used = blk_end[-1:].astype(jnp.int32)
    tail = ((blk_end - 1) * (EXP_BLK * TOK_ROWS)).astype(jnp.int32)
    has_tail = (counts % EXP_BLK != 0).astype(jnp.int32)
    dests = [_slots(pad_start.astype(jnp.int32), idx_t, rank_t) for idx_t, rank_t in zip(idxs, ranks)]

    slots = None
    for x1, dest_t in zip(x1s, dests):
        slots = _dispatch(tail, has_tail, _dest_tiles(dest_t, DISP_TM), x1, slots, n_slots)
    ys = _experts(block_expert, n_used, slots, w1p, w2b, b1p, b2r)
    outs = []
    for x, x1, dest_t, gate_t in zip(xs_in, x1s, dests, gates):
        y = _combine(_dest_tiles(dest_t, COMB_TM), x1, gate_t.T, ln2g, ln2b, ys)
        outs.append(y.reshape(x.shape))
    return tuple(outs)
```

```python
import functools

import jax
import jax.numpy as jnp
from jax import lax
from jax.experimental import pallas as pl
from jax.experimental.pallas import tpu as pltpu

D_MODEL = 1024
GRID_W = 64
HEAD_DIM = 64
NA_HEADS = 8
GQA_HEADS = 8
GQA_KV_HEADS = 2
GQA_GROUP = GQA_HEADS // GQA_KV_HEADS
NA_WIDTH = NA_HEADS * HEAD_DIM
GQA_WIDTH = GQA_HEADS * HEAD_DIM
KV_WIDTH = GQA_KV_HEADS * HEAD_DIM
NA_WIN_R = 8
NA_WIN_C = 16
ROPE_AXIS_DIM = HEAD_DIM // 2
ROPE_THETA = 10000.0
N_EXPERTS = 32
TOP_K = 4
D_FF = D_MODEL
SWIGLU_ALPHA = 1.702
SWIGLU_LIMIT = 7.0
DEPTH = 1
DN_ALPHA = (2.0 * DEPTH) ** 0.25
NEG_INF = -1e30
RMS_EPS = 1e-6
LN_EPS = 1e-5

LANES = 128
PROJ_COLS = 3 * NA_WIDTH + GQA_WIDTH + 2 * KV_WIDTH

PROJ_TM = 512
NA_ROWS = 8
NA_PASS_HEADS = 4
GQA_SCORE_ELEMS = 1024 * 2048
MIX_TM = 512
DISP_TM = 1024
COMB_TM = 256
EXP_BLK = 1024
W1_GROUP = 2 * LANES
SLOT_TM = 2048
V7X_VMEM_BYTES = 64 * 1024 * 1024
VMEM_LIMIT = V7X_VMEM_BYTES - 8 * 1024 * 1024

F32 = jnp.float32
BF16 = jnp.bfloat16

TOK_ROWS = D_MODEL // LANES


def _tok_load(ref, n, lead=()):
    return jnp.concatenate([ref[lead + (pl.ds(s, n, stride=TOK_ROWS), slice(None))] for s in range(TOK_ROWS)],
                           axis=1)


def _tok_store(ref, val):
    n = val.shape[0]
    for s in range(TOK_ROWS):
        ref[pl.ds(s, n, stride=TOK_ROWS), :] = val[:, s * LANES:(s + 1) * LANES]


def _cparams(sem):
    return pltpu.CompilerParams(dimension_semantics=sem, vmem_limit_bytes=VMEM_LIMIT)


def _rope(y, cos, sin, first_half):
    half = ROPE_AXIS_DIM // 2
    partner = jnp.where(first_half, pltpu.roll(y, LANES - half, 1), pltpu.roll(y, half, 1))
    return y * cos + partner * sin


def _proj_kernel(x_ref, w_ref, cos_ref, sin_ref, gq_ref, gk_ref,
                 qa_ref, ka_ref, va_ref, qb_ref, kb_ref, vlo_ref, vhi_ref):
    x = x_ref[...].astype(BF16)

    def cols(c0, width):
        return jnp.dot(x, w_ref[:, c0:c0 + width], preferred_element_type=F32)

    qa_ref[...] = cols(0, NA_WIDTH).astype(BF16)
    ka_ref[...] = cols(NA_WIDTH, NA_WIDTH).astype(BF16)
    va_ref[...] = cols(2 * NA_WIDTH, NA_WIDTH).astype(BF16)
    cos = cos_ref[...]
    sin = sin_ref[...]
    lane = lax.broadcasted_iota(jnp.int32, cos.shape, 1)
    first_half = (lane % ROPE_AXIS_DIM) < ROPE_AXIS_DIM // 2
    lo = lane < HEAD_DIM

    def norm_rope(y, g):
        y2 = y * y
        ms_lo = jnp.sum(jnp.where(lo, y2, 0.0), axis=-1, keepdims=True)
        ms_hi = jnp.sum(jnp.where(lo, 0.0, y2), axis=-1, keepdims=True)
        ms = jnp.where(lo, ms_lo, ms_hi) * (1.0 / HEAD_DIM)
        return _rope(y * lax.rsqrt(ms + RMS_EPS) * g, cos, sin, first_half).astype(BF16)

    base = 3 * NA_WIDTH
    q = cols(base, GQA_WIDTH)
    gq = gq_ref[...]
    for g in range(GQA_GROUP):
        qb_ref[:, g * LANES:(g + 1) * LANES] = norm_rope(q[:, g * LANES:(g + 1) * LANES], gq)
    base += GQA_WIDTH
    kv = cols(base, 2 * KV_WIDTH)
    kb_ref[...] = norm_rope(kv[:, :KV_WIDTH], gk_ref[...])
    v = kv[:, KV_WIDTH:]
    vlo_ref[...] = jnp.where(lo, v, 1.0).astype(BF16)
    vhi_ref[...] = jnp.where(lo, 1.0, v).astype(BF16)


def _proj(x2d, seq_len, w, cos, sin, gq, gk):
    n = x2d.shape[0]
    tm = min(PROJ_TM, seq_len)
    assert n % tm == 0 and seq_len % tm == 0
    pos_blocks = seq_len // tm
    row = lambda i: (i, 0)
    const = lambda i: (0, 0)
    outs = [(NA_WIDTH, BF16)] * 3 + [(GQA_WIDTH, BF16)] + [(KV_WIDTH, BF16)] * 3
    return pl.pallas_call(
        _proj_kernel,
        grid=(n // tm,),
        in_specs=[
            pl.BlockSpec((tm, D_MODEL), row),
            pl.BlockSpec((D_MODEL, PROJ_COLS), const),
            pl.BlockSpec((tm, LANES), lambda i: (i % pos_blocks, 0)),
            pl.BlockSpec((tm, LANES), lambda i: (i % pos_blocks, 0)),
            pl.BlockSpec((1, LANES), const),
            pl.BlockSpec((1, LANES), const),
        ],
        out_specs=[pl.BlockSpec((tm, c), row) for c, _ in outs],
        out_shape=[jax.ShapeDtypeStruct((n, c), dt) for c, dt in outs],
        compiler_params=_cparams(("parallel",)),
        name="proj",
    )(x2d, w, cos, sin, gq, gk)


def _na_kernel(q_ref, k_ref, v_ref, bias_ref, o_ref, *, rows):
    j = pl.program_id(1)
    width = NA_PASS_HEADS * HEAD_DIM
    lane_head = lax.broadcasted_iota(jnp.int32, (GRID_W, width), 1) // HEAD_DIM

    def one_row(rr, carry):
        r = j * NA_ROWS + rr
        rs = jnp.clip(r - NA_WIN_R // 2, 0, rows - NA_WIN_R)
        var = r - rs
        k0 = pl.multiple_of(rs * GRID_W, GRID_W)
        q0 = pl.multiple_of(rr * GRID_W, GRID_W)
        for g in range(NA_HEADS // NA_PASS_HEADS):
            cols = slice(g * width, (g + 1) * width)
            heads = range(g * NA_PASS_HEADS, (g + 1) * NA_PASS_HEADS)
            q = q_ref[0, pl.ds(q0, GRID_W), cols]
            zero = jnp.zeros_like(q)
            qs = jnp.concatenate([jnp.where(lane_head == a, q, zero) for a in range(NA_PASS_HEADS)], axis=0)
            k = k_ref[0, pl.ds(k0, NA_WIN_R * GRID_W), cols]
            v = v_ref[0, pl.ds(k0, NA_WIN_R * GRID_W), cols]
            s = lax.dot_general(qs, k, (((1,), (1,)), ((), ())), preferred_element_type=F32)
            s = s + jnp.concatenate([bias_ref[var, h] for h in heads], axis=0)
            m = jnp.max(s, axis=-1, keepdims=True)
            p = jnp.exp(s - m)
            l = jnp.sum(p, axis=-1, keepdims=True)
            os = jnp.dot(p.astype(BF16), v, preferred_element_type=F32) / l
            o = os[:GRID_W]
            for a in range(1, NA_PASS_HEADS):
                o = jnp.where(lane_head == a, os[a * GRID_W:(a + 1) * GRID_W], o)
            o_ref[0, pl.ds(q0, GRID_W), cols] = o.astype(BF16)
        return carry

    lax.fori_loop(0, NA_ROWS, one_row, 0, unroll=True)


def _na(qa, ka, va, bias):
    b, t, _ = qa.shape
    rows = t // GRID_W
    assert rows >= NA_WIN_R and rows % NA_ROWS == 0
    tq = NA_ROWS * GRID_W
    return pl.pallas_call(
        functools.partial(_na_kernel, rows=rows),
        grid=(b, rows // NA_ROWS),
        in_specs=[
            pl.BlockSpec((1, tq, NA_WIDTH), lambda i, j: (i, j, 0)),
            pl.BlockSpec((1, t, NA_WIDTH), lambda i, j: (i, 0, 0)),
            pl.BlockSpec((1, t, NA_WIDTH), lambda i, j: (i, 0, 0)),
            pl.BlockSpec(bias.shape, lambda i, j: (0, 0, 0, 0), pipeline_mode=pl.Buffered(1)),
        ],
        out_specs=pl.BlockSpec((1, tq, NA_WIDTH), lambda i, j: (i, j, 0)),
        out_shape=jax.ShapeDtypeStruct((b, t, NA_WIDTH), BF16),
        compiler_params=_cparams(("parallel", "parallel")),
        name="na",
    )(qa, ka, va, bias)


def _na_bias_tables(rpb):
    off = jnp.arange(NA_WIN_R)
    jrow = jnp.arange(NA_WIN_R)
    dr = jrow[None, :] - off[:, None] + (NA_WIN_R - 1)
    c = jnp.arange(GRID_W)
    cs = jnp.clip(c - NA_WIN_C // 2, 0, GRID_W - NA_WIN_C)
    col_ok = (c[None, :] >= cs[:, None]) & (c[None, :] < cs[:, None] + NA_WIN_C)
    dc = jnp.clip(c[None, :] - c[:, None], -(NA_WIN_C - 1), NA_WIN_C - 1) + (NA_WIN_C - 1)
    sel_r = (dr[:, :, None] == jnp.arange(2 * NA_WIN_R - 1)[None, None, :]).astype(F32)
    sel_c = (dc[:, :, None] == jnp.arange(2 * NA_WIN_C - 1)[None, None, :]).astype(F32)
    bias = jnp.einsum('vja,hab,qkb->hvqjk', sel_r, rpb.astype(F32), sel_c, precision=lax.Precision.HIGHEST)
    bias = jnp.where(col_ok[None, None, :, None, :], bias, NEG_INF)
    return jnp.moveaxis(bias, 0, 1).reshape(NA_WIN_R, NA_HEADS, GRID_W, NA_WIN_R * GRID_W)


def _gqa_kernel(q_ref, k_ref, vlo_ref, vhi_ref, o_ref):
    k = k_ref[0]
    lane = lax.broadcasted_iota(jnp.int32, (q_ref.shape[1], LANES), 1)
    lo = lane < HEAD_DIM

    def head(q, v):
        s = lax.dot_general(q, k, (((1,), (1,)), ((), ())), preferred_element_type=F32)
        p = jnp.exp(s - jnp.max(s, axis=-1, keepdims=True)).astype(BF16)
        ov = jnp.dot(p, v, preferred_element_type=F32)
        return ov / pltpu.roll(ov, HEAD_DIM, 1)

    for g in range(GQA_GROUP):
        q = q_ref[0, :, g * LANES:(g + 1) * LANES]
        zero = jnp.zeros_like(q)
        o = jnp.where(lo, head(jnp.where(lo, q, zero), vlo_ref[0]), head(jnp.where(lo, zero, q), vhi_ref[0]))
        o_ref[0, :, g * LANES:(g + 1) * LANES] = o.astype(BF16)


def _gqa(qb, kb, vlo, vhi):
    b, t, _ = qb.shape
    tq = min(t, GQA_SCORE_ELEMS // t)
    assert t % tq == 0 and tq % 8 == 0
    return pl.pallas_call(
        _gqa_kernel,
        grid=(b, t // tq),
        in_specs=[
            pl.BlockSpec((1, tq, GQA_WIDTH), lambda i, j: (i, j, 0)),
            pl.BlockSpec((1, t, KV_WIDTH), lambda i, j: (i, 0, 0)),
            pl.BlockSpec((1, t, KV_WIDTH), lambda i, j: (i, 0, 0)),
            pl.BlockSpec((1, t, KV_WIDTH), lambda i, j: (i, 0, 0)),
        ],
        out_specs=pl.BlockSpec((1, tq, GQA_WIDTH), lambda i, j: (i, j, 0)),
        out_shape=jax.ShapeDtypeStruct((b, t, GQA_WIDTH), BF16),
        compiler_params=_cparams(("parallel", "parallel")),
        name="gqa",
    )(qb, kb, vlo, vhi)


def _layer_norm(z, g, b):
    mu = jnp.mean(z, axis=-1, keepdims=True)
    zc = z - mu
    var = jnp.mean(zc * zc, axis=-1, keepdims=True)
    return zc * lax.rsqrt(var + LN_EPS) * g + b


def _rms(o, g):
    return o * lax.rsqrt(jnp.mean(o * o, axis=-1, keepdims=True) + RMS_EPS) * g


def _mix_kernel(oa_ref, ob_ref, x_ref, woa_ref, wob_ref, ga_ref, gb_ref, lng_ref, lnb_ref,
                rw_ref, rb_ref, tri_ref, cnt_in_ref,
                x1_ref, idx_ref, gate_ref, rank_ref, cnt_ref, carry_ref):
    @pl.when(pl.program_id(0) == 0)
    def _():
        carry_ref[...] = cnt_in_ref[...]

    na = _rms(oa_ref[...].astype(F32), ga_ref[...]).astype(BF16)
    nb = _rms(ob_ref[...].astype(F32), gb_ref[...]).astype(BF16)
    mixed = (jnp.dot(na, woa_ref[...], preferred_element_type=F32)
             + jnp.dot(nb, wob_ref[...], preferred_element_type=F32))
    x1 = _layer_norm(DN_ALPHA * x_ref[...] + mixed, lng_ref[...], lnb_ref[...])
    _tok_store(x1_ref, x1)

    logits = lax.dot_general(rw_ref[...], x1.astype(BF16), (((1,), (1,)), ((), ())),
                             preferred_element_type=F32) + rb_ref[...]
    tm = logits.shape[1]
    eidx = lax.broadcasted_iota(jnp.int32, (N_EXPERTS, tm), 0).astype(F32)
    work = logits
    vals, idxs, hots = [], [], []
    for _ in range(TOP_K):
        m = jnp.max(work, axis=0, keepdims=True)
        sel = jnp.min(jnp.where(work == m, eidx, float(N_EXPERTS)), axis=0, keepdims=True)
        hot = eidx == sel
        vals.append(m)
        idxs.append(sel)
        hots.append(hot)
        work = jnp.where(hot, -jnp.inf, work)
    es = [jnp.exp(v - vals[0]) for v in vals]
    den = es[0] + es[1] + es[2] + es[3]
    gate_ref[...] = jnp.concatenate([e / den for e in es], axis=0)
    idx_ref[...] = jnp.concatenate(idxs, axis=0).astype(jnp.int32)

    hot_all = hots[0] | hots[1] | hots[2] | hots[3]
    onehot = jnp.where(hot_all, 1.0, 0.0)
    before = jnp.dot(onehot.astype(BF16), tri_ref[...], preferred_element_type=F32)
    before = before + carry_ref[:, 0:1]
    ranks = [jnp.sum(jnp.where(hot, before, 0.0), axis=0, keepdims=True) for hot in hots]
    rank_ref[...] = jnp.concatenate(ranks, axis=0).astype(jnp.int32)
    carry_ref[...] = carry_ref[...] + jnp.sum(onehot, axis=1, keepdims=True)
    cnt_ref[...] = carry_ref[...]


def _mix(oa, ob, x2d, woa, wob, ga, gb, lng, lnb, rw_t, rb, tri, cnt_in):
    n = x2d.shape[0]
    tm = MIX_TM
    assert n % tm == 0
    row = lambda i: (i, 0)
    col = lambda i: (0, i)
    const = lambda i: (0, 0)
    full = lambda a: pl.BlockSpec(a.shape, const)
    return pl.pallas_call(
        _mix_kernel,
        grid=(n // tm,),
        in_specs=[
            pl.BlockSpec((tm, NA_WIDTH), row),
            pl.BlockSpec((tm, GQA_WIDTH), row),
            pl.BlockSpec((tm, D_MODEL), row),
            full(woa), full(wob), full(ga), full(gb), full(lng), full(lnb),
            full(rw_t), full(rb), full(tri), full(cnt_in),
        ],
        out_specs=[
            pl.BlockSpec((tm * TOK_ROWS, LANES), row),
            pl.BlockSpec((TOP_K, tm), col),
            pl.BlockSpec((TOP_K, tm), col),
            pl.BlockSpec((TOP_K, tm), col),
            pl.BlockSpec((N_EXPERTS, LANES), const),
        ],
        out_shape=[
            jax.ShapeDtypeStruct((n * TOK_ROWS, LANES), F32),
            jax.ShapeDtypeStruct((TOP_K, n), jnp.int32),
            jax.ShapeDtypeStruct((TOP_K, n), F32),
            jax.ShapeDtypeStruct((TOP_K, n), jnp.int32),
            jax.ShapeDtypeStruct((N_EXPERTS, LANES), F32),
        ],
        scratch_shapes=[pltpu.VMEM((N_EXPERTS, LANES), F32)],
        compiler_params=_cparams(("arbitrary",)),
        name="mix",
    )(oa, ob, x2d, woa, wob, ga, gb, lng, lnb, rw_t, rb, tri, cnt_in)


def _slot_kernel(pad_ref, idx_ref, rank_ref, o_ref):
    idx = idx_ref[...]
    start = jnp.zeros_like(idx)
    for e in range(N_EXPERTS):
        start = jnp.where(idx == e, pad_ref[e], start)
    o_ref[...] = (start + rank_ref[...]) * TOK_ROWS


def _slots(pad_start, idx_t, rank_t):
    n = idx_t.shape[1]
    tm = min(SLOT_TM, n)
    assert n % tm == 0
    col = lambda i, *_: (0, i)
    return pl.pallas_call(
        _slot_kernel,
        grid_spec=pltpu.PrefetchScalarGridSpec(
            num_scalar_prefetch=1,
            grid=(n // tm,),
            in_specs=[pl.BlockSpec((TOP_K, tm), col), pl.BlockSpec((TOP_K, tm), col)],
            out_specs=pl.BlockSpec((TOP_K, tm), col),
        ),
        out_shape=jax.ShapeDtypeStruct((TOP_K, n), jnp.int32),
        compiler_params=_cparams(("parallel",)),
        name="slots",
    )(pad_start, idx_t, rank_t)


def _dispatch_kernel(tail_ref, has_tail_ref, dest_ref, x_ref, *rest, zero_tails):
    if zero_tails:
        xs_ref, zeros_ref, sem = rest
    else:
        _, xs_ref, sem = rest
    tm = x_ref.shape[0] // TOK_ROWS

    if zero_tails:
        @pl.when(pl.program_id(0) == 0)
        def _():
            zeros_ref[...] = jnp.zeros_like(zeros_ref)

            def tail_copy(e):
                start = pl.multiple_of(tail_ref[e], EXP_BLK * TOK_ROWS)
                return pltpu.make_async_copy(zeros_ref, xs_ref.at[pl.ds(start, EXP_BLK * TOK_ROWS), :], sem)

            for e in range(N_EXPERTS):
                @pl.when(has_tail_ref[e] != 0)
                def _():
                    tail_copy(e).start()
            for e in range(N_EXPERTS):
                @pl.when(has_tail_ref[e] != 0)
                def _():
                    tail_copy(e).wait()

    def row_copy(i, k):
        src = pl.multiple_of(i * TOK_ROWS, TOK_ROWS)
        dst = pl.multiple_of(dest_ref[0, 0, k * tm + i], TOK_ROWS)
        return pltpu.make_async_copy(x_ref.at[pl.ds(src, TOK_ROWS), :], xs_ref.at[pl.ds(dst, TOK_ROWS), :], sem)

    def issue(i, c):
        for k in range(TOP_K):
            row_copy(i, k).start(priority=k % 2)
        return c

    lax.fori_loop(0, tm, issue, 0, unroll=4)
    for k in range(TOP_K):
        pltpu.make_async_copy(x_ref, xs_ref.at[pl.ds(0, tm * TOK_ROWS), :], sem).wait()


def _dispatch(tail, has_tail, dest_tiles, x1, xs_prev, n_slots):
    n = x1.shape[0] // TOK_ROWS
    tm = dest_tiles.shape[2] // TOP_K
    assert n % tm == 0
    zero_tails = xs_prev is None
    in_specs = [
        pl.BlockSpec((1, 1, TOP_K * tm), lambda i, *_: (i, 0, 0), memory_space=pltpu.SMEM),
        pl.BlockSpec((tm * TOK_ROWS, LANES), lambda i, *_: (i, 0)),
    ]
    args = [dest_tiles, x1]
    scratch = []
    aliases = {}
    if zero_tails:
        scratch.append(pltpu.VMEM((EXP_BLK * TOK_ROWS, LANES), F32))
    else:
        in_specs.append(pl.BlockSpec(memory_space=pl.ANY))
        args.append(xs_prev)
        aliases = {4: 0}
    scratch.append(pltpu.SemaphoreType.DMA(()))
    return pl.pallas_call(
        functools.partial(_dispatch_kernel, zero_tails=zero_tails),
        grid_spec=pltpu.PrefetchScalarGridSpec(
            num_scalar_prefetch=2,
            grid=(n // tm,),
            in_specs=in_specs,
            out_specs=pl.BlockSpec(memory_space=pl.ANY),
            scratch_shapes=scratch,
        ),
        out_shape=jax.ShapeDtypeStruct((n_slots * TOK_ROWS, LANES), F32),
        input_output_aliases=aliases,
        compiler_params=pltpu.CompilerParams(dimension_semantics=("arbitrary",), vmem_limit_bytes=VMEM_LIMIT,
                                             has_side_effects=True),
        name="dispatch",
    )(tail, has_tail, *args)


def _w1_prep_kernel(w_ref, perm_ref, o_ref):
    for c in range(w_ref.shape[2] // W1_GROUP):
        cols = slice(c * W1_GROUP, (c + 1) * W1_GROUP)
        w = w_ref[0, :, cols].astype(BF16)
        o_ref[0, :, cols] = jnp.dot(w, perm_ref[...], preferred_element_type=F32).astype(BF16)


def _w1_prep(w1):
    e, d, f2 = w1.shape
    half = f2 // 2
    j = jnp.arange(W1_GROUP)
    dst = jnp.where(j % 2 == 0, j // 2, LANES + j // 2)
    perm = (dst[:, None] == jnp.arange(W1_GROUP)[None, :]).astype(BF16)
    return pl.pallas_call(
        _w1_prep_kernel,
        grid=(e, 2),
        in_specs=[pl.BlockSpec((1, d, half), lambda i, j: (i, 0, j)),
                  pl.BlockSpec((W1_GROUP, W1_GROUP), lambda i, j: (0, 0))],
        out_specs=pl.BlockSpec((1, d, half), lambda i, j: (i, 0, j)),
        out_shape=jax.ShapeDtypeStruct((e, d, f2), BF16),
        compiler_params=_cparams(("parallel", "parallel")),
        name="w1prep",
    )(w1, perm)


def _expert_kernel(be_ref, nused_ref, xs_ref, w1_ref, w2_ref, b1_ref, b2_ref, o_ref):
    @pl.when(pl.program_id(0) < nused_ref[0])
    def _():
        x = _tok_load(xs_ref, EXP_BLK).astype(BF16)
        h = jnp.dot(x, w1_ref[0], preferred_element_type=F32) + b1_ref[0]
        groups = range(h.shape[1] // W1_GROUP)
        glu = jnp.concatenate([h[:, c * W1_GROUP: c * W1_GROUP + LANES] for c in groups], axis=1)
        lin = jnp.concatenate([h[:, c * W1_GROUP + LANES: (c + 1) * W1_GROUP] for c in groups], axis=1)
        glu = jnp.minimum(glu, SWIGLU_LIMIT)
        lin = jnp.clip(lin, -SWIGLU_LIMIT, SWIGLU_LIMIT)
        act = glu * jax.nn.sigmoid(SWIGLU_ALPHA * glu) * (lin + 1.0)
        _tok_store(o_ref, jnp.dot(act.astype(BF16), w2_ref[0], preferred_element_type=F32) + b2_ref[0])


def _experts(block_expert, n_used, xs, w1p, w2, b1p, b2):
    n_slots = xs.shape[0] // TOK_ROWS
    n_blocks = n_slots // EXP_BLK
    slot = lambda i, be, nu: (jnp.minimum(i, nu[0] - 1), 0)
    wsel = lambda i, be, nu: (be[i], 0, 0)
    return pl.pallas_call(
        _expert_kernel,
        grid_spec=pltpu.PrefetchScalarGridSpec(
            num_scalar_prefetch=2,
            grid=(n_blocks,),
            in_specs=[
                pl.BlockSpec((EXP_BLK * TOK_ROWS, LANES), slot),
                pl.BlockSpec((1, D_MODEL, 2 * D_FF), wsel),
                pl.BlockSpec((1, D_FF, D_MODEL), wsel),
                pl.BlockSpec((1, 1, 2 * D_FF), wsel),
                pl.BlockSpec((1, 1, D_MODEL), wsel),
            ],
            out_specs=pl.BlockSpec((EXP_BLK * TOK_ROWS, LANES), slot),
        ),
        out_shape=jax.ShapeDtypeStruct((n_slots * TOK_ROWS, LANES), F32),
        compiler_params=_cparams(("arbitrary",)),
        name="experts",
    )(block_expert, n_used, xs, w1p, w2, b1p, b2)


def _combine_kernel(dest_ref, dest_next_ref, x1_ref, gate_ref, lng_ref, lnb_ref, ys_ref, o_ref, buf_ref, sems):
    tm = x1_ref.shape[0] // TOK_ROWS
    step = pl.program_id(0)
    slot = step % 2

    def gather(idx_ref, into):
        def issue(i, c):
            for k in range(TOP_K):
                src = pl.multiple_of(idx_ref[0, 0, k * tm + i], TOK_ROWS)
                dst = pl.multiple_of(i * TOK_ROWS, TOK_ROWS)
                pltpu.make_async_copy(ys_ref.at[pl.ds(src, TOK_ROWS), :],
                                      buf_ref.at[into, k, pl.ds(dst, TOK_ROWS), :], sems.at[into]
                                      ).start(priority=k % 2)
            return c

        lax.fori_loop(0, tm, issue, 0, unroll=4)

    @pl.when(step == 0)
    def _():
        gather(dest_ref, slot)

    @pl.when(step + 1 < pl.num_programs(0))
    def _():
        gather(dest_next_ref, 1 - slot)

    for k in range(TOP_K):
        pltpu.make_async_copy(ys_ref.at[pl.ds(0, tm * TOK_ROWS), :], buf_ref.at[slot, k], sems.at[slot]).wait()
    gate = gate_ref[...]
    y = _tok_load(buf_ref, tm, (slot, 0)) * gate[:, 0:1]
    for k in range(1, TOP_K):
        y = y + _tok_load(buf_ref, tm, (slot, k)) * gate[:, k:k + 1]
    o_ref[...] = _layer_norm(DN_ALPHA * _tok_load(x1_ref, tm) + y, lng_ref[...], lnb_ref[...])


def _combine(dest_tiles, x1, gate, lng, lnb, ys):
    n = x1.shape[0] // TOK_ROWS
    tm = COMB_TM
    assert n % tm == 0
    n_tiles = n // tm
    return pl.pallas_call(
        _combine_kernel,
        grid=(n_tiles,),
        in_specs=[
            pl.BlockSpec((1, 1, TOP_K * tm), lambda i: (i, 0, 0), memory_space=pltpu.SMEM),
            pl.BlockSpec((1, 1, TOP_K * tm), lambda i: (jnp.minimum(i + 1, n_tiles - 1), 0, 0),
                         memory_space=pltpu.SMEM),
            pl.BlockSpec((tm * TOK_ROWS, LANES), lambda i: (i, 0)),
            pl.BlockSpec((tm, TOP_K), lambda i: (i, 0)),
            pl.BlockSpec((1, D_MODEL), lambda i: (0, 0)),
            pl.BlockSpec((1, D_MODEL), lambda i: (0, 0)),
            pl.BlockSpec(memory_space=pl.ANY),
        ],
        out_specs=pl.BlockSpec((tm, D_MODEL), lambda i: (i, 0)),
        out_shape=jax.ShapeDtypeStruct((n, D_MODEL), F32),
        scratch_shapes=[pltpu.VMEM((2, TOP_K, tm * TOK_ROWS, LANES), F32), pltpu.SemaphoreType.DMA((2,))],
        compiler_params=_cparams(("arbitrary",)),
        name="combine",
    )(dest_tiles, dest_tiles, x1, gate, lng, lnb, ys)


def _rope_tables(t):
    pos = jnp.arange(t)
    row = (pos // GRID_W).astype(F32)
    col = (pos % GRID_W).astype(F32)
    inv = ROPE_THETA ** (-jnp.arange(0, ROPE_AXIS_DIM, 2, dtype=F32) / ROPE_AXIS_DIM)
    ar = row[:, None] * inv[None, :]
    ac = col[:, None] * inv[None, :]
    cos = jnp.concatenate([jnp.cos(ar), jnp.cos(ar), jnp.cos(ac), jnp.cos(ac)], axis=1)
    sin = jnp.concatenate([-jnp.sin(ar), jnp.sin(ar), -jnp.sin(ac), jnp.sin(ac)], axis=1)
    return jnp.tile(cos, (1, 2)), jnp.tile(sin, (1, 2))


def _prep_w_in(w):
    scale = HEAD_DIM ** -0.5
    qa = w[:, :NA_WIDTH] * scale
    kva = w[:, NA_WIDTH:3 * NA_WIDTH]
    qb = _gqa_out_order(w[:, 3 * NA_WIDTH:3 * NA_WIDTH + GQA_WIDTH].T).T
    rest = w[:, 3 * NA_WIDTH + GQA_WIDTH:]
    return jnp.concatenate([qa, kva, qb, rest], axis=1).astype(BF16)


def _gqa_out_order(a):
    rest = a.shape[1:]
    a = a.reshape((GQA_KV_HEADS, GQA_GROUP, HEAD_DIM) + rest)
    return jnp.swapaxes(a, 0, 1).reshape((GQA_WIDTH,) + rest)


def _dest_tiles(dest_t, tm):
    n = dest_t.shape[1]
    return dest_t.reshape(TOP_K, n // tm, tm).transpose(1, 0, 2).reshape(n // tm, 1, TOP_K * tm)


def kernel(x_prompt, x_sample, w_in, rpb, q_norm_g, k_norm_g, g_out_na, g_out_gqa, w_o, ln1_g, ln1_b,
           router_w, router_b, w1, b1, w2, b2, ln2_g, ln2_b):
    assert GQA_KV_HEADS == 2 and KV_WIDTH == LANES
    xs_in = [x_prompt, x_sample]
    l = 0
    w_proj = _prep_w_in(w_in[l])
    scale = HEAD_DIM ** -0.5
    gq = jnp.tile(q_norm_g[l] * scale, 2).reshape(1, LANES)
    gk = jnp.tile(k_norm_g[l], 2).reshape(1, LANES)
    bias = _na_bias_tables(rpb[l])
    woa = w_o[l][:NA_WIDTH].astype(BF16)
    wob = _gqa_out_order(w_o[l][NA_WIDTH:]).astype(BF16)
    ga = g_out_na[l].reshape(1, NA_WIDTH)
    gb = _gqa_out_order(g_out_gqa[l]).reshape(1, GQA_WIDTH)
    ln1g, ln1b = ln1_g[l].reshape(1, D_MODEL), ln1_b[l].reshape(1, D_MODEL)
    ln2g, ln2b = ln2_g[l].reshape(1, D_MODEL), ln2_b[l].reshape(1, D_MODEL)
    rw_t = router_w[l].T.astype(BF16)
    rb = router_b[l].reshape(N_EXPERTS, 1)
    tri = (jnp.arange(MIX_TM)[:, None] < jnp.arange(MIX_TM)[None, :]).astype(BF16)
    w1p = _w1_prep(w1[l])
    w2b = w2[l].astype(BF16)
    b1p = jnp.swapaxes(b1[l].reshape(N_EXPERTS, 2 * D_FF // W1_GROUP, LANES, 2), 2, 3).reshape(N_EXPERTS, 1, 2 * D_FF)
    b2r = b2[l].reshape(N_EXPERTS, 1, D_MODEL)

    x1s, idxs, gates, ranks = [], [], [], []
    cnt = jnp.zeros((N_EXPERTS, LANES), F32)
    for x in xs_in:
        b, t, _ = x.shape
        x2d = x.reshape(b * t, D_MODEL)
        cos, sin = _rope_tables(t)
        qa, ka, va, qb, kb, vlo, vhi = _proj(x2d, t, w_proj, cos, sin, gq, gk)
        sh = lambda a: a.reshape(b, t, a.shape[-1])
        oa = _na(sh(qa), sh(ka), sh(va), bias).reshape(b * t, NA_WIDTH)
        ob = _gqa(sh(qb), sh(kb), sh(vlo), sh(vhi)).reshape(b * t, GQA_WIDTH)
        x1, idx_t, gate_t, rank_t, cnt = _mix(oa, ob, x2d, woa, wob, ga, gb, ln1g, ln1b, rw_t, rb, tri, cnt)
        x1s.append(x1)
        idxs.append(idx_t)
        gates.append(gate_t)
        ranks.append(rank_t)

    counts = cnt[:, 0].astype(jnp.int32)
    nblk = (counts + EXP_BLK - 1) // EXP_BLK
    blk_end = jnp.cumsum(nblk)
    pad_start = (blk_end - nblk) * EXP_BLK
    n_assign = TOP_K * sum(x.shape[0] * x.shape[1] for x in xs_in)
    n_blocks = -(-n_assign // EXP_BLK) + N_EXPERTS
    n_slots = n_blocks * EXP_BLK
    blocks = jnp.arange(n_blocks, dtype=jnp.int32)
    block_expert = jnp.minimum(jnp.sum((blk_end[None, :] <= blocks[:, None]).astype(jnp.int32), axis=1),
                               N_EXPERTS - 1)
    n_used = blk_end[-1:].astype(jnp.int32)
    tail = ((blk_end - 1) * (EXP_BLK * TOK_ROWS)).astype(jnp.int32)
    has_tail = (counts % EXP_BLK != 0).astype(jnp.int32)
    dests = [_slots(pad_start.astype(jnp.int32), idx_t, rank_t) for idx_t, rank_t in zip(idxs, ranks)]

    slots = None
    for x1, dest_t in zip(x1s, dests):
        slots = _dispatch(tail, has_tail, _dest_tiles(dest_t, min(DISP_TM, dest_t.shape[1])), x1, slots, n_slots)
    ys = _experts(block_expert, n_used, slots, w1p, w2b, b1p, b2r)
    outs = []
    for x, x1, dest_t, gate_t in zip(xs_in, x1s, dests, gates):
        y = _combine(_dest_tiles(dest_t, COMB_TM), x1, gate_t.T, ln2g, ln2b, ys)
        outs.append(y.reshape(x.shape))
    return tuple(outs)
```

```python
import functools

import jax
import jax.numpy as jnp
from jax import lax
from jax.experimental import pallas as pl
from jax.experimental.pallas import tpu as pltpu

D_MODEL = 1024
GRID_W = 64
HEAD_DIM = 64
NA_HEADS = 8
GQA_HEADS = 8
GQA_KV_HEADS = 2
GQA_GROUP = GQA_HEADS // GQA_KV_HEADS
NA_WIDTH = NA_HEADS * HEAD_DIM
GQA_WIDTH = GQA_HEADS * HEAD_DIM
KV_WIDTH = GQA_KV_HEADS * HEAD_DIM
NA_WIN_R = 8
NA_WIN_C = 16
ROPE_AXIS_DIM = HEAD_DIM // 2
ROPE_THETA = 10000.0
N_EXPERTS = 32
TOP_K = 4
D_FF = D_MODEL
SWIGLU_ALPHA = 1.702
SWIGLU_LIMIT = 7.0
DEPTH = 1
DN_ALPHA = (2.0 * DEPTH) ** 0.25
NEG_INF = -1e30
LOG2_E = 1.4426950408889634
SCORE_SCALE = HEAD_DIM ** -0.5 * LOG2_E
RMS_EPS = 1e-6
LN_EPS = 1e-5

LANES = 128
PROJ_COLS = 3 * NA_WIDTH + GQA_WIDTH + 2 * KV_WIDTH

PROJ_TM = 512
NA_ROWS = 8
NA_PASS_HEADS = 4
GQA_SCORE_ELEMS = 1024 * 2048
MIX_TM = 512
DISP_TM = 1024
COMB_TM = 256
EXP_BLK = 1024
W1_GROUP = 2 * LANES
SLOT_TM = 2048
V7X_VMEM_BYTES = 64 * 1024 * 1024
VMEM_LIMIT = V7X_VMEM_BYTES - 8 * 1024 * 1024

F32 = jnp.float32
BF16 = jnp.bfloat16

TOK_ROWS = D_MODEL // LANES


def _tok_load(ref, n, lead=()):
    return jnp.concatenate([ref[lead + (pl.ds(s, n, stride=TOK_ROWS), slice(None))] for s in range(TOK_ROWS)],
                           axis=1)


def _tok_store(ref, val):
    n = val.shape[0]
    for s in range(TOK_ROWS):
        ref[pl.ds(s, n, stride=TOK_ROWS), :] = val[:, s * LANES:(s + 1) * LANES]


def _cparams(sem):
    return pltpu.CompilerParams(dimension_semantics=sem, vmem_limit_bytes=VMEM_LIMIT)


def _rope(y, cos, sin, first_half):
    half = ROPE_AXIS_DIM // 2
    partner = jnp.where(first_half, pltpu.roll(y, LANES - half, 1), pltpu.roll(y, half, 1))
    return y * cos + partner * sin


def _proj_kernel(x_ref, w_ref, cos_ref, sin_ref, gq_ref, gk_ref,
                 qa_ref, ka_ref, va_ref, qb_ref, kb_ref, vlo_ref, vhi_ref):
    x = x_ref[...].astype(BF16)

    def cols(c0, width):
        return jnp.dot(x, w_ref[:, c0:c0 + width], preferred_element_type=F32)

    qa_ref[...] = cols(0, NA_WIDTH).astype(BF16)
    ka_ref[...] = cols(NA_WIDTH, NA_WIDTH).astype(BF16)
    va_ref[...] = cols(2 * NA_WIDTH, NA_WIDTH).astype(BF16)
    cos = cos_ref[...]
    sin = sin_ref[...]
    lane = lax.broadcasted_iota(jnp.int32, cos.shape, 1)
    first_half = (lane % ROPE_AXIS_DIM) < ROPE_AXIS_DIM // 2
    lo = lane < HEAD_DIM

    def norm_rope(y, g):
        y2 = y * y
        ms_lo = jnp.sum(jnp.where(lo, y2, 0.0), axis=-1, keepdims=True)
        ms_hi = jnp.sum(jnp.where(lo, 0.0, y2), axis=-1, keepdims=True)
        ms = jnp.where(lo, ms_lo, ms_hi) * (1.0 / HEAD_DIM)
        return _rope(y * lax.rsqrt(ms + RMS_EPS) * g, cos, sin, first_half).astype(BF16)

    base = 3 * NA_WIDTH
    q = cols(base, GQA_WIDTH)
    gq = gq_ref[...]
    for g in range(GQA_GROUP):
        qb_ref[:, g * LANES:(g + 1) * LANES] = norm_rope(q[:, g * LANES:(g + 1) * LANES], gq)
    base += GQA_WIDTH
    kv = cols(base, 2 * KV_WIDTH)
    kb_ref[...] = norm_rope(kv[:, :KV_WIDTH], gk_ref[...])
    v = kv[:, KV_WIDTH:]
    vlo_ref[...] = jnp.where(lo, v, 1.0).astype(BF16)
    vhi_ref[...] = jnp.where(lo, 1.0, v).astype(BF16)


def _proj(x2d, seq_len, w, cos, sin, gq, gk):
    n = x2d.shape[0]
    tm = min(PROJ_TM, seq_len)
    assert n % tm == 0 and seq_len % tm == 0
    pos_blocks = seq_len // tm
    row = lambda i: (i, 0)
    const = lambda i: (0, 0)
    outs = [(NA_WIDTH, BF16)] * 3 + [(GQA_WIDTH, BF16)] + [(KV_WIDTH, BF16)] * 3
    return pl.pallas_call(
        _proj_kernel,
        grid=(n // tm,),
        in_specs=[
            pl.BlockSpec((tm, D_MODEL), row),
            pl.BlockSpec((D_MODEL, PROJ_COLS), const),
            pl.BlockSpec((tm, LANES), lambda i: (i % pos_blocks, 0)),
            pl.BlockSpec((tm, LANES), lambda i: (i % pos_blocks, 0)),
            pl.BlockSpec((1, LANES), const),
            pl.BlockSpec((1, LANES), const),
        ],
        out_specs=[pl.BlockSpec((tm, c), row) for c, _ in outs],
        out_shape=[jax.ShapeDtypeStruct((n, c), dt) for c, dt in outs],
        compiler_params=_cparams(("parallel",)),
        name="proj",
    )(x2d, w, cos, sin, gq, gk)


def _na_kernel(q_ref, k_ref, v_ref, bias_ref, o_ref, *, rows):
    j = pl.program_id(1)
    width = NA_PASS_HEADS * HEAD_DIM
    lane_head = lax.broadcasted_iota(jnp.int32, (GRID_W, width), 1) // HEAD_DIM

    def one_row(rr, carry):
        r = j * NA_ROWS + rr
        rs = jnp.clip(r - NA_WIN_R // 2, 0, rows - NA_WIN_R)
        var = r - rs
        k0 = pl.multiple_of(rs * GRID_W, GRID_W)
        q0 = pl.multiple_of(rr * GRID_W, GRID_W)
        for g in range(NA_HEADS // NA_PASS_HEADS):
            cols = slice(g * width, (g + 1) * width)
            heads = range(g * NA_PASS_HEADS, (g + 1) * NA_PASS_HEADS)
            q = q_ref[0, pl.ds(q0, GRID_W), cols]
            zero = jnp.zeros_like(q)
            qs = jnp.concatenate([jnp.where(lane_head == a, q, zero) for a in range(NA_PASS_HEADS)], axis=0)
            k = k_ref[0, pl.ds(k0, NA_WIN_R * GRID_W), cols]
            v = v_ref[0, pl.ds(k0, NA_WIN_R * GRID_W), cols]
            s = lax.dot_general(qs, k, (((1,), (1,)), ((), ())), preferred_element_type=F32)
            s = s + jnp.concatenate([bias_ref[var, h] for h in heads], axis=0)
            m = jnp.max(s, axis=-1, keepdims=True)
            p = jnp.exp2(s - m)
            l = jnp.sum(p, axis=-1, keepdims=True)
            os = jnp.dot(p.astype(BF16), v, preferred_element_type=F32) / l
            o = os[:GRID_W]
            for a in range(1, NA_PASS_HEADS):
                o = jnp.where(lane_head == a, os[a * GRID_W:(a + 1) * GRID_W], o)
            o_ref[0, pl.ds(q0, GRID_W), cols] = o.astype(BF16)
        return carry

    lax.fori_loop(0, NA_ROWS, one_row, 0, unroll=True)


def _na(qa, ka, va, bias):
    b, t, _ = qa.shape
    rows = t // GRID_W
    assert rows >= NA_WIN_R and rows % NA_ROWS == 0
    tq = NA_ROWS * GRID_W
    return pl.pallas_call(
        functools.partial(_na_kernel, rows=rows),
        grid=(b, rows // NA_ROWS),
        in_specs=[
            pl.BlockSpec((1, tq, NA_WIDTH), lambda i, j: (i, j, 0)),
            pl.BlockSpec((1, t, NA_WIDTH), lambda i, j: (i, 0, 0)),
            pl.BlockSpec((1, t, NA_WIDTH), lambda i, j: (i, 0, 0)),
            pl.BlockSpec(bias.shape, lambda i, j: (0, 0, 0, 0), pipeline_mode=pl.Buffered(1)),
        ],
        out_specs=pl.BlockSpec((1, tq, NA_WIDTH), lambda i, j: (i, j, 0)),
        out_shape=jax.ShapeDtypeStruct((b, t, NA_WIDTH), BF16),
        compiler_params=_cparams(("parallel", "parallel")),
        name="na",
    )(qa, ka, va, bias)


def _na_bias_tables(rpb):
    off = jnp.arange(NA_WIN_R)
    jrow = jnp.arange(NA_WIN_R)
    dr = jrow[None, :] - off[:, None] + (NA_WIN_R - 1)
    c = jnp.arange(GRID_W)
    cs = jnp.clip(c - NA_WIN_C // 2, 0, GRID_W - NA_WIN_C)
    col_ok = (c[None, :] >= cs[:, None]) & (c[None, :] < cs[:, None] + NA_WIN_C)
    dc = jnp.clip(c[None, :] - c[:, None], -(NA_WIN_C - 1), NA_WIN_C - 1) + (NA_WIN_C - 1)
    sel_r = (dr[:, :, None] == jnp.arange(2 * NA_WIN_R - 1)[None, None, :]).astype(F32)
    sel_c = (dc[:, :, None] == jnp.arange(2 * NA_WIN_C - 1)[None, None, :]).astype(F32)
    bias = jnp.einsum('vja,hab,qkb->hvqjk', sel_r, rpb.astype(F32), sel_c, precision=lax.Precision.HIGHEST)
    bias = jnp.where(col_ok[None, None, :, None, :], bias * LOG2_E, NEG_INF)
    return jnp.moveaxis(bias, 0, 1).reshape(NA_WIN_R, NA_HEADS, GRID_W, NA_WIN_R * GRID_W)


def _gqa_kernel(q_ref, k_ref, vlo_ref, vhi_ref, o_ref):
    k = k_ref[0]
    lane = lax.broadcasted_iota(jnp.int32, (q_ref.shape[1], LANES), 1)
    lo = lane < HEAD_DIM

    def head(q, v):
        s = lax.dot_general(q, k, (((1,), (1,)), ((), ())), preferred_element_type=F32)
        p = jnp.exp2(s - jnp.max(s, axis=-1, keepdims=True)).astype(BF16)
        ov = jnp.dot(p, v, preferred_element_type=F32)
        return ov / pltpu.roll(ov, HEAD_DIM, 1)

    for g in range(GQA_GROUP):
        q = q_ref[0, :, g * LANES:(g + 1) * LANES]
        zero = jnp.zeros_like(q)
        o = jnp.where(lo, head(jnp.where(lo, q, zero), vlo_ref[0]), head(jnp.where(lo, zero, q), vhi_ref[0]))
        o_ref[0, :, g * LANES:(g + 1) * LANES] = o.astype(BF16)


def _gqa(qb, kb, vlo, vhi):
    b, t, _ = qb.shape
    tq = min(t, GQA_SCORE_ELEMS // t)
    assert t % tq == 0 and tq % 8 == 0
    return pl.pallas_call(
        _gqa_kernel,
        grid=(b, t // tq),
        in_specs=[
            pl.BlockSpec((1, tq, GQA_WIDTH), lambda i, j: (i, j, 0)),
            pl.BlockSpec((1, t, KV_WIDTH), lambda i, j: (i, 0, 0)),
            pl.BlockSpec((1, t, KV_WIDTH), lambda i, j: (i, 0, 0)),
            pl.BlockSpec((1, t, KV_WIDTH), lambda i, j: (i, 0, 0)),
        ],
        out_specs=pl.BlockSpec((1, tq, GQA_WIDTH), lambda i, j: (i, j, 0)),
        out_shape=jax.ShapeDtypeStruct((b, t, GQA_WIDTH), BF16),
        compiler_params=_cparams(("parallel", "parallel")),
        name="gqa",
    )(qb, kb, vlo, vhi)


def _layer_norm(z, g, b):
    mu = jnp.mean(z, axis=-1, keepdims=True)
    zc = z - mu
    var = jnp.mean(zc * zc, axis=-1, keepdims=True)
    return zc * lax.rsqrt(var + LN_EPS) * g + b


def _rms(o, g):
    return o * lax.rsqrt(jnp.mean(o * o, axis=-1, keepdims=True) + RMS_EPS) * g


def _mix_kernel(oa_ref, ob_ref, x_ref, woa_ref, wob_ref, ga_ref, gb_ref, lng_ref, lnb_ref,
                rw_ref, rb_ref, tri_ref, cnt_in_ref,
                x1_ref, idx_ref, gate_ref, rank_ref, cnt_ref, carry_ref):
    @pl.when(pl.program_id(0) == 0)
    def _():
        carry_ref[...] = cnt_in_ref[...]

    na = _rms(oa_ref[...].astype(F32), ga_ref[...]).astype(BF16)
    nb = _rms(ob_ref[...].astype(F32), gb_ref[...]).astype(BF16)
    mixed = (jnp.dot(na, woa_ref[...], preferred_element_type=F32)
             + jnp.dot(nb, wob_ref[...], preferred_element_type=F32))
    x1 = _layer_norm(DN_ALPHA * x_ref[...] + mixed, lng_ref[...], lnb_ref[...])
    _tok_store(x1_ref, x1)

    logits = lax.dot_general(rw_ref[...], x1.astype(BF16), (((1,), (1,)), ((), ())),
                             preferred_element_type=F32) + rb_ref[...]
    tm = logits.shape[1]
    eidx = lax.broadcasted_iota(jnp.int32, (N_EXPERTS, tm), 0).astype(F32)
    work = logits
    vals, idxs, hots = [], [], []
    for _ in range(TOP_K):
        m = jnp.max(work, axis=0, keepdims=True)
        sel = jnp.min(jnp.where(work == m, eidx, float(N_EXPERTS)), axis=0, keepdims=True)
        hot = eidx == sel
        vals.append(m)
        idxs.append(sel)
        hots.append(hot)
        work = jnp.where(hot, -jnp.inf, work)
    es = [jnp.exp(v - vals[0]) for v in vals]
    den = es[0] + es[1] + es[2] + es[3]
    gate_ref[...] = jnp.concatenate([e / den for e in es], axis=0)
    idx_ref[...] = jnp.concatenate(idxs, axis=0).astype(jnp.int32)

    hot_all = hots[0] | hots[1] | hots[2] | hots[3]
    onehot = jnp.where(hot_all, 1.0, 0.0)
    before = jnp.dot(onehot.astype(BF16), tri_ref[...], preferred_element_type=F32)
    before = before + carry_ref[:, 0:1]
    ranks = [jnp.sum(jnp.where(hot, before, 0.0), axis=0, keepdims=True) for hot in hots]
    rank_ref[...] = jnp.concatenate(ranks, axis=0).astype(jnp.int32)
    carry_ref[...] = carry_ref[...] + jnp.sum(onehot, axis=1, keepdims=True)
    cnt_ref[...] = carry_ref[...]


def _mix(oa, ob, x2d, woa, wob, ga, gb, lng, lnb, rw_t, rb, tri, cnt_in):
    n = x2d.shape[0]
    tm = MIX_TM
    assert n % tm == 0
    row = lambda i: (i, 0)
    col = lambda i: (0, i)
    const = lambda i: (0, 0)
    full = lambda a: pl.BlockSpec(a.shape, const)
    return pl.pallas_call(
        _mix_kernel,
        grid=(n // tm,),
        in_specs=[
            pl.BlockSpec((tm, NA_WIDTH), row),
            pl.BlockSpec((tm, GQA_WIDTH), row),
            pl.BlockSpec((tm, D_MODEL), row),
            full(woa), full(wob), full(ga), full(gb), full(lng), full(lnb),
            full(rw_t), full(rb), full(tri), full(cnt_in),
        ],
        out_specs=[
            pl.BlockSpec((tm * TOK_ROWS, LANES), row),
            pl.BlockSpec((TOP_K, tm), col),
            pl.BlockSpec((TOP_K, tm), col),
            pl.BlockSpec((TOP_K, tm), col),
            pl.BlockSpec((N_EXPERTS, LANES), const),
        ],
        out_shape=[
            jax.ShapeDtypeStruct((n * TOK_ROWS, LANES), F32),
            jax.ShapeDtypeStruct((TOP_K, n), jnp.int32),
            jax.ShapeDtypeStruct((TOP_K, n), F32),
            jax.ShapeDtypeStruct((TOP_K, n), jnp.int32),
            jax.ShapeDtypeStruct((N_EXPERTS, LANES), F32),
        ],
        scratch_shapes=[pltpu.VMEM((N_EXPERTS, LANES), F32)],
        compiler_params=_cparams(("arbitrary",)),
        name="mix",
    )(oa, ob, x2d, woa, wob, ga, gb, lng, lnb, rw_t, rb, tri, cnt_in)


def _slot_kernel(pad_ref, idx_ref, rank_ref, o_ref):
    idx = idx_ref[...]
    start = jnp.zeros_like(idx)
    for e in range(N_EXPERTS):
        start = jnp.where(idx == e, pad_ref[e], start)
    o_ref[...] = (start + rank_ref[...]) * TOK_ROWS


def _slots(pad_start, idx_t, rank_t):
    n = idx_t.shape[1]
    tm = min(SLOT_TM, n)
    assert n % tm == 0
    col = lambda i, *_: (0, i)
    return pl.pallas_call(
        _slot_kernel,
        grid_spec=pltpu.PrefetchScalarGridSpec(
            num_scalar_prefetch=1,
            grid=(n // tm,),
            in_specs=[pl.BlockSpec((TOP_K, tm), col), pl.BlockSpec((TOP_K, tm), col)],
            out_specs=pl.BlockSpec((TOP_K, tm), col),
        ),
        out_shape=jax.ShapeDtypeStruct((TOP_K, n), jnp.int32),
        compiler_params=_cparams(("parallel",)),
        name="slots",
    )(pad_start, idx_t, rank_t)


def _dispatch_kernel(tail_ref, has_tail_ref, dest_ref, x_ref, *rest, zero_tails):
    if zero_tails:
        xs_ref, zeros_ref, sem = rest
    else:
        _, xs_ref, sem = rest
    tm = x_ref.shape[0] // TOK_ROWS

    if zero_tails:
        @pl.when(pl.program_id(0) == 0)
        def _():
            zeros_ref[...] = jnp.zeros_like(zeros_ref)

            def tail_copy(e):
                start = pl.multiple_of(tail_ref[e], EXP_BLK * TOK_ROWS)
                return pltpu.make_async_copy(zeros_ref, xs_ref.at[pl.ds(start, EXP_BLK * TOK_ROWS), :], sem)

            for e in range(N_EXPERTS):
                @pl.when(has_tail_ref[e] != 0)
                def _():
                    tail_copy(e).start()
            for e in range(N_EXPERTS):
                @pl.when(has_tail_ref[e] != 0)
                def _():
                    tail_copy(e).wait()

    def row_copy(i, k):
        src = pl.multiple_of(i * TOK_ROWS, TOK_ROWS)
        dst = pl.multiple_of(dest_ref[0, 0, k * tm + i], TOK_ROWS)
        return pltpu.make_async_copy(x_ref.at[pl.ds(src, TOK_ROWS), :], xs_ref.at[pl.ds(dst, TOK_ROWS), :], sem)

    def issue(i, c):
        for k in range(TOP_K):
            row_copy(i, k).start(priority=k % 2)
        return c

    lax.fori_loop(0, tm, issue, 0, unroll=4)
    for k in range(TOP_K):
        pltpu.make_async_copy(x_ref, xs_ref.at[pl.ds(0, tm * TOK_ROWS), :], sem).wait()


def _dispatch(tail, has_tail, dest_tiles, x1, xs_prev, n_slots):
    n = x1.shape[0] // TOK_ROWS
    tm = dest_tiles.shape[2] // TOP_K
    assert n % tm == 0
    zero_tails = xs_prev is None
    in_specs = [
        pl.BlockSpec((1, 1, TOP_K * tm), lambda i, *_: (i, 0, 0), memory_space=pltpu.SMEM),
        pl.BlockSpec((tm * TOK_ROWS, LANES), lambda i, *_: (i, 0)),
    ]
    args = [dest_tiles, x1]
    scratch = []
    aliases = {}
    if zero_tails:
        scratch.append(pltpu.VMEM((EXP_BLK * TOK_ROWS, LANES), F32))
    else:
        in_specs.append(pl.BlockSpec(memory_space=pl.ANY))
        args.append(xs_prev)
        aliases = {4: 0}
    scratch.append(pltpu.SemaphoreType.DMA(()))
    return pl.pallas_call(
        functools.partial(_dispatch_kernel, zero_tails=zero_tails),
        grid_spec=pltpu.PrefetchScalarGridSpec(
            num_scalar_prefetch=2,
            grid=(n // tm,),
            in_specs=in_specs,
            out_specs=pl.BlockSpec(memory_space=pl.ANY),
            scratch_shapes=scratch,
        ),
        out_shape=jax.ShapeDtypeStruct((n_slots * TOK_ROWS, LANES), F32),
        input_output_aliases=aliases,
        compiler_params=pltpu.CompilerParams(dimension_semantics=("arbitrary",), vmem_limit_bytes=VMEM_LIMIT,
                                             has_side_effects=True),
        name="dispatch",
    )(tail, has_tail, *args)


def _w1_prep_kernel(w_ref, perm_ref, o_ref):
    for c in range(w_ref.shape[2] // W1_GROUP):
        cols = slice(c * W1_GROUP, (c + 1) * W1_GROUP)
        w = w_ref[0, :, cols].astype(BF16)
        o_ref[0, :, cols] = jnp.dot(w, perm_ref[...], preferred_element_type=F32).astype(BF16)


def _w1_prep(w1):
    e, d, f2 = w1.shape
    half = f2 // 2
    j = jnp.arange(W1_GROUP)
    dst = jnp.where(j % 2 == 0, j // 2, LANES + j // 2)
    perm = (dst[:, None] == jnp.arange(W1_GROUP)[None, :]).astype(BF16)
    return pl.pallas_call(
        _w1_prep_kernel,
        grid=(e, 2),
        in_specs=[pl.BlockSpec((1, d, half), lambda i, j: (i, 0, j)),
                  pl.BlockSpec((W1_GROUP, W1_GROUP), lambda i, j: (0, 0))],
        out_specs=pl.BlockSpec((1, d, half), lambda i, j: (i, 0, j)),
        out_shape=jax.ShapeDtypeStruct((e, d, f2), BF16),
        compiler_params=_cparams(("parallel", "parallel")),
        name="w1prep",
    )(w1, perm)


def _expert_kernel(be_ref, nused_ref, xs_ref, w1_ref, w2_ref, b1_ref, b2_ref, o_ref):
    @pl.when(pl.program_id(0) < nused_ref[0])
    def _():
        x = _tok_load(xs_ref, EXP_BLK).astype(BF16)
        h = jnp.dot(x, w1_ref[0], preferred_element_type=F32) + b1_ref[0]
        groups = range(h.shape[1] // W1_GROUP)
        glu = jnp.concatenate([h[:, c * W1_GROUP: c * W1_GROUP + LANES] for c in groups], axis=1)
        lin = jnp.concatenate([h[:, c * W1_GROUP + LANES: (c + 1) * W1_GROUP] for c in groups], axis=1)
        glu = jnp.minimum(glu, SWIGLU_LIMIT)
        lin = jnp.clip(lin, -SWIGLU_LIMIT, SWIGLU_LIMIT)
        act = glu * jax.nn.sigmoid(SWIGLU_ALPHA * glu) * (lin + 1.0)
        _tok_store(o_ref, jnp.dot(act.astype(BF16), w2_ref[0], preferred_element_type=F32) + b2_ref[0])


def _experts(block_expert, n_used, xs, w1p, w2, b1p, b2):
    n_slots = xs.shape[0] // TOK_ROWS
    n_blocks = n_slots // EXP_BLK
    slot = lambda i, be, nu: (jnp.minimum(i, nu[0] - 1), 0)
    wsel = lambda i, be, nu: (be[i], 0, 0)
    return pl.pallas_call(
        _expert_kernel,
        grid_spec=pltpu.PrefetchScalarGridSpec(
            num_scalar_prefetch=2,
            grid=(n_blocks,),
            in_specs=[
                pl.BlockSpec((EXP_BLK * TOK_ROWS, LANES), slot),
                pl.BlockSpec((1, D_MODEL, 2 * D_FF), wsel),
                pl.BlockSpec((1, D_FF, D_MODEL), wsel),
                pl.BlockSpec((1, 1, 2 * D_FF), wsel),
                pl.BlockSpec((1, 1, D_MODEL), wsel),
            ],
            out_specs=pl.BlockSpec((EXP_BLK * TOK_ROWS, LANES), slot),
        ),
        out_shape=jax.ShapeDtypeStruct((n_slots * TOK_ROWS, LANES), F32),
        compiler_params=_cparams(("arbitrary",)),
        name="experts",
    )(block_expert, n_used, xs, w1p, w2, b1p, b2)


def _combine_kernel(dest_ref, dest_next_ref, x1_ref, gate_ref, lng_ref, lnb_ref, ys_ref, o_ref, buf_ref, sems):
    tm = x1_ref.shape[0] // TOK_ROWS
    step = pl.program_id(0)
    slot = step % 2

    def gather(idx_ref, into):
        def issue(i, c):
            for k in range(TOP_K):
                src = pl.multiple_of(idx_ref[0, 0, k * tm + i], TOK_ROWS)
                dst = pl.multiple_of(i * TOK_ROWS, TOK_ROWS)
                pltpu.make_async_copy(ys_ref.at[pl.ds(src, TOK_ROWS), :],
                                      buf_ref.at[into, k, pl.ds(dst, TOK_ROWS), :], sems.at[into]
                                      ).start(priority=k % 2)
            return c

        lax.fori_loop(0, tm, issue, 0, unroll=4)

    @pl.when(step == 0)
    def _():
        gather(dest_ref, slot)

    @pl.when(step + 1 < pl.num_programs(0))
    def _():
        gather(dest_next_ref, 1 - slot)

    for k in range(TOP_K):
        pltpu.make_async_copy(ys_ref.at[pl.ds(0, tm * TOK_ROWS), :], buf_ref.at[slot, k], sems.at[slot]).wait()
    gate = gate_ref[...]
    y = _tok_load(buf_ref, tm, (slot, 0)) * gate[:, 0:1]
    for k in range(1, TOP_K):
        y = y + _tok_load(buf_ref, tm, (slot, k)) * gate[:, k:k + 1]
    o_ref[...] = _layer_norm(DN_ALPHA * _tok_load(x1_ref, tm) + y, lng_ref[...], lnb_ref[...])


def _combine(dest_tiles, x1, gate, lng, lnb, ys):
    n = x1.shape[0] // TOK_ROWS
    tm = COMB_TM
    assert n % tm == 0
    n_tiles = n // tm
    return pl.pallas_call(
        _combine_kernel,
        grid=(n_tiles,),
        in_specs=[
            pl.BlockSpec((1, 1, TOP_K * tm), lambda i: (i, 0, 0), memory_space=pltpu.SMEM),
            pl.BlockSpec((1, 1, TOP_K * tm), lambda i: (jnp.minimum(i + 1, n_tiles - 1), 0, 0),
                         memory_space=pltpu.SMEM),
            pl.BlockSpec((tm * TOK_ROWS, LANES), lambda i: (i, 0)),
            pl.BlockSpec((tm, TOP_K), lambda i: (i, 0)),
            pl.BlockSpec((1, D_MODEL), lambda i: (0, 0)),
            pl.BlockSpec((1, D_MODEL), lambda i: (0, 0)),
            pl.BlockSpec(memory_space=pl.ANY),
        ],
        out_specs=pl.BlockSpec((tm, D_MODEL), lambda i: (i, 0)),
        out_shape=jax.ShapeDtypeStruct((n, D_MODEL), F32),
        scratch_shapes=[pltpu.VMEM((2, TOP_K, tm * TOK_ROWS, LANES), F32), pltpu.SemaphoreType.DMA((2,))],
        compiler_params=_cparams(("arbitrary",)),
        name="combine",
    )(dest_tiles, dest_tiles, x1, gate, lng, lnb, ys)


def _rope_tables(t):
    pos = jnp.arange(t)
    row = (pos // GRID_W).astype(F32)
    col = (pos % GRID_W).astype(F32)
    inv = ROPE_THETA ** (-jnp.arange(0, ROPE_AXIS_DIM, 2, dtype=F32) / ROPE_AXIS_DIM)
    ar = row[:, None] * inv[None, :]
    ac = col[:, None] * inv[None, :]
    cos = jnp.concatenate([jnp.cos(ar), jnp.cos(ar), jnp.cos(ac), jnp.cos(ac)], axis=1)
    sin = jnp.concatenate([-jnp.sin(ar), jnp.sin(ar), -jnp.sin(ac), jnp.sin(ac)], axis=1)
    return jnp.tile(cos, (1, 2)), jnp.tile(sin, (1, 2))


def _prep_w_in(w):
    qa = w[:, :NA_WIDTH] * SCORE_SCALE
    kva = w[:, NA_WIDTH:3 * NA_WIDTH]
    qb = _gqa_out_order(w[:, 3 * NA_WIDTH:3 * NA_WIDTH + GQA_WIDTH].T).T
    rest = w[:, 3 * NA_WIDTH + GQA_WIDTH:]
    return jnp.concatenate([qa, kva, qb, rest], axis=1).astype(BF16)


def _gqa_out_order(a):
    rest = a.shape[1:]
    a = a.reshape((GQA_KV_HEADS, GQA_GROUP, HEAD_DIM) + rest)
    return jnp.swapaxes(a, 0, 1).reshape((GQA_WIDTH,) + rest)


def _dest_tiles(dest_t, tm):
    n = dest_t.shape[1]
    return dest_t.reshape(TOP_K, n // tm, tm).transpose(1, 0, 2).reshape(n // tm, 1, TOP_K * tm)


def kernel(x_prompt, x_sample, w_in, rpb, q_norm_g, k_norm_g, g_out_na, g_out_gqa, w_o, ln1_g, ln1_b,
           router_w, router_b, w1, b1, w2, b2, ln2_g, ln2_b):
    assert GQA_KV_HEADS == 2 and KV_WIDTH == LANES
    xs_in = [x_prompt, x_sample]
    l = 0
    w_proj = _prep_w_in(w_in[l])
    gq = jnp.tile(q_norm_g[l] * SCORE_SCALE, 2).reshape(1, LANES)
    gk = jnp.tile(k_norm_g[l], 2).reshape(1, LANES)
    bias = _na_bias_tables(rpb[l])
    woa = w_o[l][:NA_WIDTH].astype(BF16)
    wob = _gqa_out_order(w_o[l][NA_WIDTH:]).astype(BF16)
    ga = g_out_na[l].reshape(1, NA_WIDTH)
    gb = _gqa_out_order(g_out_gqa[l]).reshape(1, GQA_WIDTH)
    ln1g, ln1b = ln1_g[l].reshape(1, D_MODEL), ln1_b[l].reshape(1, D_MODEL)
    ln2g, ln2b = ln2_g[l].reshape(1, D_MODEL), ln2_b[l].reshape(1, D_MODEL)
    rw_t = router_w[l].T.astype(BF16)
    rb = router_b[l].reshape(N_EXPERTS, 1)
    tri = (jnp.arange(MIX_TM)[:, None] < jnp.arange(MIX_TM)[None, :]).astype(BF16)
    w1p = _w1_prep(w1[l])
    w2b = w2[l].astype(BF16)
    b1p = jnp.swapaxes(b1[l].reshape(N_EXPERTS, 2 * D_FF // W1_GROUP, LANES, 2), 2, 3).reshape(N_EXPERTS, 1, 2 * D_FF)
    b2r = b2[l].reshape(N_EXPERTS, 1, D_MODEL)

    x1s, idxs, gates, ranks = [], [], [], []
    cnt = jnp.zeros((N_EXPERTS, LANES), F32)
    for x in xs_in:
        b, t, _ = x.shape
        x2d = x.reshape(b * t, D_MODEL)
        cos, sin = _rope_tables(t)
        qa, ka, va, qb, kb, vlo, vhi = _proj(x2d, t, w_proj, cos, sin, gq, gk)
        sh = lambda a: a.reshape(b, t, a.shape[-1])
        oa = _na(sh(qa), sh(ka), sh(va), bias).reshape(b * t, NA_WIDTH)
        ob = _gqa(sh(qb), sh(kb), sh(vlo), sh(vhi)).reshape(b * t, GQA_WIDTH)
        x1, idx_t, gate_t, rank_t, cnt = _mix(oa, ob, x2d, woa, wob, ga, gb, ln1g, ln1b, rw_t, rb, tri, cnt)
        x1s.append(x1)
        idxs.append(idx_t)
        gates.append(gate_t)
        ranks.append(rank_t)

    counts = cnt[:, 0].astype(jnp.int32)
    nblk = (counts + EXP_BLK - 1) // EXP_BLK
    blk_end = jnp.cumsum(nblk)
    pad_start = (blk_end - nblk) * EXP_BLK
    n_assign = TOP_K * sum(x.shape[0] * x.shape[1] for x in xs_in)
    n_blocks = -(-n_assign // EXP_BLK) + N_EXPERTS
    n_slots = n_blocks * EXP_BLK
    blocks = jnp.arange(n_blocks, dtype=jnp.int32)
    block_expert = jnp.minimum(jnp.sum((blk_end[None, :] <= blocks[:, None]).astype(jnp.int32), axis=1),
                               N_EXPERTS - 1)
    n_used = blk_end[-1:].astype(jnp.int32)
    tail = ((blk_end - 1) * (EXP_BLK * TOK_ROWS)).astype(jnp.int32)
    has_tail = (counts % EXP_BLK != 0).astype(jnp.int32)
    dests = [_slots(pad_start.astype(jnp.int32), idx_t, rank_t) for idx_t, rank_t in zip(idxs, ranks)]

    slots = None
    for x1, dest_t in zip(x1s, dests):
        slots = _dispatch(tail, has_tail, _dest_tiles(dest_t, min(DISP_TM, dest_t.shape[1])), x1, slots, n_slots)
    ys = _experts(block_expert, n_used, slots, w1p, w2b, b1p, b2r)
    outs = []
    for x, x1, dest_t, gate_t in zip(xs_in, x1s, dests, gates):
        y = _combine(_dest_tiles(dest_t, COMB_TM), x1, gate_t.T, ln2g, ln2b, ys)
        outs.append(y.reshape(x.shape))
    return tuple(outs)
```

```python
import functools

import jax
import jax.numpy as jnp
from jax import lax
from jax.experimental import pallas as pl
from jax.experimental.pallas import tpu as pltpu

D_MODEL = 1024
GRID_W = 64
HEAD_DIM = 64
NA_HEADS = 8
GQA_HEADS = 8
GQA_KV_HEADS = 2
GQA_GROUP = GQA_HEADS // GQA_KV_HEADS
NA_WIDTH = NA_HEADS * HEAD_DIM
GQA_WIDTH = GQA_HEADS * HEAD_DIM
KV_WIDTH = GQA_KV_HEADS * HEAD_DIM
NA_WIN_R = 8
NA_WIN_C = 16
ROPE_AXIS_DIM = HEAD_DIM // 2
ROPE_THETA = 10000.0
N_EXPERTS = 32
TOP_K = 4
D_FF = D_MODEL
SWIGLU_ALPHA = 1.702
SWIGLU_LIMIT = 7.0
DEPTH = 1
DN_ALPHA = (2.0 * DEPTH) ** 0.25
NEG_INF = -1e30
LOG2_E = 1.4426950408889634
SCORE_SCALE = HEAD_DIM ** -0.5 * LOG2_E
RMS_EPS = 1e-6
LN_EPS = 1e-5

LANES = 128
PROJ_COLS = 3 * NA_WIDTH + GQA_WIDTH + 2 * KV_WIDTH

PROJ_TM = 512
NA_ROWS = 8
NA_PASS_HEADS = 4
GQA_SCORE_ELEMS = 1024 * 2048
MIX_TM = 512
DISP_TM = 1024
COMB_TM = 256
EXP_BLK = 1024
W1_GROUP = 2 * LANES
SLOT_TM = 2048
V7X_VMEM_BYTES = 64 * 1024 * 1024
VMEM_LIMIT = V7X_VMEM_BYTES - 8 * 1024 * 1024

F32 = jnp.float32
BF16 = jnp.bfloat16

TOK_ROWS = D_MODEL // LANES


def _tok_load(ref, n, lead=()):
    return jnp.concatenate([ref[lead + (pl.ds(s, n, stride=TOK_ROWS), slice(None))] for s in range(TOK_ROWS)],
                           axis=1)


def _tok_store(ref, val):
    n = val.shape[0]
    for s in range(TOK_ROWS):
        ref[pl.ds(s, n, stride=TOK_ROWS), :] = val[:, s * LANES:(s + 1) * LANES]


def _cparams(sem):
    return pltpu.CompilerParams(dimension_semantics=sem, vmem_limit_bytes=VMEM_LIMIT)


def _rope(y, cos, sin, first_half):
    half = ROPE_AXIS_DIM // 2
    partner = jnp.where(first_half, pltpu.roll(y, LANES - half, 1), pltpu.roll(y, half, 1))
    return y * cos + partner * sin


def _proj_kernel(x_ref, w_ref, cos_ref, sin_ref, gq_ref, gk_ref,
                 qa_ref, ka_ref, va_ref, qb_ref, kb_ref, vlo_ref, vhi_ref):
    x = x_ref[...].astype(BF16)

    def cols(c0, width):
        return jnp.dot(x, w_ref[:, c0:c0 + width], preferred_element_type=F32)

    cos = cos_ref[...]
    sin = sin_ref[...]
    lane = lax.broadcasted_iota(jnp.int32, cos.shape, 1)
    first_half = (lane % ROPE_AXIS_DIM) < ROPE_AXIS_DIM // 2
    lo = lane < HEAD_DIM

    def norm_rope(y, g):
        y2 = y * y
        ms_lo = jnp.sum(jnp.where(lo, y2, 0.0), axis=-1, keepdims=True)
        ms_hi = jnp.sum(jnp.where(lo, 0.0, y2), axis=-1, keepdims=True)
        ms = jnp.where(lo, ms_lo, ms_hi) * (1.0 / HEAD_DIM)
        return _rope(y * lax.rsqrt(ms + RMS_EPS) * g, cos, sin, first_half).astype(BF16)

    base = 3 * NA_WIDTH
    q = cols(base, GQA_WIDTH)
    gq = gq_ref[...]
    for g in range(GQA_GROUP):
        qb_ref[:, g * LANES:(g + 1) * LANES] = norm_rope(q[:, g * LANES:(g + 1) * LANES], gq)
    base += GQA_WIDTH
    kv = cols(base, 2 * KV_WIDTH)
    kb_ref[...] = norm_rope(kv[:, :KV_WIDTH], gk_ref[...])
    v = kv[:, KV_WIDTH:]
    vlo_ref[...] = jnp.where(lo, v, 1.0).astype(BF16)
    vhi_ref[...] = jnp.where(lo, 1.0, v).astype(BF16)
    qa_ref[...] = cols(0, NA_WIDTH).astype(BF16)
    ka_ref[...] = cols(NA_WIDTH, NA_WIDTH).astype(BF16)
    va_ref[...] = cols(2 * NA_WIDTH, NA_WIDTH).astype(BF16)


def _proj(x2d, seq_len, w, cos, sin, gq, gk):
    n = x2d.shape[0]
    tm = min(PROJ_TM, seq_len)
    assert n % tm == 0 and seq_len % tm == 0
    pos_blocks = seq_len // tm
    row = lambda i: (i, 0)
    const = lambda i: (0, 0)
    outs = [(NA_WIDTH, BF16)] * 3 + [(GQA_WIDTH, BF16)] + [(KV_WIDTH, BF16)] * 3
    return pl.pallas_call(
        _proj_kernel,
        grid=(n // tm,),
        in_specs=[
            pl.BlockSpec((tm, D_MODEL), row),
            pl.BlockSpec((D_MODEL, PROJ_COLS), const),
            pl.BlockSpec((tm, LANES), lambda i: (i % pos_blocks, 0)),
            pl.BlockSpec((tm, LANES), lambda i: (i % pos_blocks, 0)),
            pl.BlockSpec((1, LANES), const),
            pl.BlockSpec((1, LANES), const),
        ],
        out_specs=[pl.BlockSpec((tm, c), row) for c, _ in outs],
        out_shape=[jax.ShapeDtypeStruct((n, c), dt) for c, dt in outs],
        compiler_params=_cparams(("parallel",)),
        name="proj",
    )(x2d, w, cos, sin, gq, gk)


def _na_kernel(q_ref, k_ref, v_ref, bias_ref, o_ref, *, rows):
    j = pl.program_id(1)
    width = NA_PASS_HEADS * HEAD_DIM
    lane_head = lax.broadcasted_iota(jnp.int32, (GRID_W, width), 1) // HEAD_DIM

    def one_row(rr, carry):
        r = j * NA_ROWS + rr
        rs = jnp.clip(r - NA_WIN_R // 2, 0, rows - NA_WIN_R)
        var = r - rs
        k0 = pl.multiple_of(rs * GRID_W, GRID_W)
        q0 = pl.multiple_of(rr * GRID_W, GRID_W)
        for g in range(NA_HEADS // NA_PASS_HEADS):
            cols = slice(g * width, (g + 1) * width)
            heads = range(g * NA_PASS_HEADS, (g + 1) * NA_PASS_HEADS)
            q = q_ref[0, pl.ds(q0, GRID_W), cols]
            zero = jnp.zeros_like(q)
            qs = jnp.concatenate([jnp.where(lane_head == a, q, zero) for a in range(NA_PASS_HEADS)], axis=0)
            k = k_ref[0, pl.ds(k0, NA_WIN_R * GRID_W), cols]
            v = v_ref[0, pl.ds(k0, NA_WIN_R * GRID_W), cols]
            s = lax.dot_general(qs, k, (((1,), (1,)), ((), ())), preferred_element_type=F32)
            s = s + jnp.concatenate([bias_ref[var, h] for h in heads], axis=0)
            m = jnp.max(s, axis=-1, keepdims=True)
            p = jnp.exp2(s - m)
            l = jnp.sum(p, axis=-1, keepdims=True)
            os = jnp.dot(p.astype(BF16), v, preferred_element_type=F32) / l
            o = os[:GRID_W]
            for a in range(1, NA_PASS_HEADS):
                o = jnp.where(lane_head == a, os[a * GRID_W:(a + 1) * GRID_W], o)
            o_ref[0, pl.ds(q0, GRID_W), cols] = o.astype(BF16)
        return carry

    lax.fori_loop(0, NA_ROWS, one_row, 0, unroll=True)


def _na(qa, ka, va, bias):
    b, t, _ = qa.shape
    rows = t // GRID_W
    assert rows >= NA_WIN_R and rows % NA_ROWS == 0
    tq = NA_ROWS * GRID_W
    return pl.pallas_call(
        functools.partial(_na_kernel, rows=rows),
        grid=(b, rows // NA_ROWS),
        in_specs=[
            pl.BlockSpec((1, tq, NA_WIDTH), lambda i, j: (i, j, 0)),
            pl.BlockSpec((1, t, NA_WIDTH), lambda i, j: (i, 0, 0)),
            pl.BlockSpec((1, t, NA_WIDTH), lambda i, j: (i, 0, 0)),
            pl.BlockSpec(bias.shape, lambda i, j: (0, 0, 0, 0), pipeline_mode=pl.Buffered(1)),
        ],
        out_specs=pl.BlockSpec((1, tq, NA_WIDTH), lambda i, j: (i, j, 0)),
        out_shape=jax.ShapeDtypeStruct((b, t, NA_WIDTH), BF16),
        compiler_params=_cparams(("parallel", "parallel")),
        name="na",
    )(qa, ka, va, bias)


def _na_bias_tables(rpb):
    off = jnp.arange(NA_WIN_R)
    jrow = jnp.arange(NA_WIN_R)
    dr = jrow[None, :] - off[:, None] + (NA_WIN_R - 1)
    c = jnp.arange(GRID_W)
    cs = jnp.clip(c - NA_WIN_C // 2, 0, GRID_W - NA_WIN_C)
    col_ok = (c[None, :] >= cs[:, None]) & (c[None, :] < cs[:, None] + NA_WIN_C)
    dc = jnp.clip(c[None, :] - c[:, None], -(NA_WIN_C - 1), NA_WIN_C - 1) + (NA_WIN_C - 1)
    sel_r = (dr[:, :, None] == jnp.arange(2 * NA_WIN_R - 1)[None, None, :]).astype(F32)
    sel_c = (dc[:, :, None] == jnp.arange(2 * NA_WIN_C - 1)[None, None, :]).astype(F32)
    bias = jnp.einsum('vja,hab,qkb->hvqjk', sel_r, rpb.astype(F32), sel_c, precision=lax.Precision.HIGHEST)
    bias = jnp.where(col_ok[None, None, :, None, :], bias * LOG2_E, NEG_INF)
    return jnp.moveaxis(bias, 0, 1).reshape(NA_WIN_R, NA_HEADS, GRID_W, NA_WIN_R * GRID_W)


def _gqa_kernel(q_ref, k_ref, vlo_ref, vhi_ref, o_ref):
    k = k_ref[0]
    lane = lax.broadcasted_iota(jnp.int32, (q_ref.shape[1], LANES), 1)
    lo = lane < HEAD_DIM

    def scores(h):
        g, hi = divmod(h, 2)
        q = q_ref[0, :, g * LANES:(g + 1) * LANES]
        zero = jnp.zeros_like(q)
        q = jnp.where(lo, zero, q) if hi else jnp.where(lo, q, zero)
        return lax.dot_general(q, k, (((1,), (1,)), ((), ())), preferred_element_type=F32)

    def attend(s, v):
        p = jnp.exp2(s - jnp.max(s, axis=-1, keepdims=True)).astype(BF16)
        ov = jnp.dot(p, v, preferred_element_type=F32)
        return ov / pltpu.roll(ov, HEAD_DIM, 1)

    n_heads = 2 * GQA_GROUP
    outs = []
    s_next = scores(0)
    for h in range(n_heads):
        s = s_next
        if h + 1 < n_heads:
            s_next = scores(h + 1)
        outs.append(attend(s, vhi_ref[0] if h % 2 else vlo_ref[0]))
        if h % 2:
            g = h // 2
            o_ref[0, :, g * LANES:(g + 1) * LANES] = jnp.where(lo, outs[h - 1], outs[h]).astype(BF16)


def _gqa(qb, kb, vlo, vhi):
    b, t, _ = qb.shape
    tq = min(t, GQA_SCORE_ELEMS // t)
    assert t % tq == 0 and tq % 8 == 0
    return pl.pallas_call(
        _gqa_kernel,
        grid=(b, t // tq),
        in_specs=[
            pl.BlockSpec((1, tq, GQA_WIDTH), lambda i, j: (i, j, 0)),
            pl.BlockSpec((1, t, KV_WIDTH), lambda i, j: (i, 0, 0)),
            pl.BlockSpec((1, t, KV_WIDTH), lambda i, j: (i, 0, 0)),
            pl.BlockSpec((1, t, KV_WIDTH), lambda i, j: (i, 0, 0)),
        ],
        out_specs=pl.BlockSpec((1, tq, GQA_WIDTH), lambda i, j: (i, j, 0)),
        out_shape=jax.ShapeDtypeStruct((b, t, GQA_WIDTH), BF16),
        compiler_params=_cparams(("parallel", "parallel")),
        name="gqa",
    )(qb, kb, vlo, vhi)


def _layer_norm(z, g, b):
    mu = jnp.mean(z, axis=-1, keepdims=True)
    zc = z - mu
    var = jnp.mean(zc * zc, axis=-1, keepdims=True)
    return zc * lax.rsqrt(var + LN_EPS) * g + b


def _rms(o, g):
    return o * lax.rsqrt(jnp.mean(o * o, axis=-1, keepdims=True) + RMS_EPS) * g


def _mix_kernel(oa_ref, ob_ref, x_ref, woa_ref, wob_ref, ga_ref, gb_ref, lng_ref, lnb_ref,
                rw_ref, rb_ref, tri_ref, cnt_in_ref,
                x1_ref, idx_ref, gate_ref, rank_ref, cnt_ref, carry_ref):
    @pl.when(pl.program_id(0) == 0)
    def _():
        carry_ref[...] = cnt_in_ref[...]

    na = _rms(oa_ref[...].astype(F32), ga_ref[...]).astype(BF16)
    nb = _rms(ob_ref[...].astype(F32), gb_ref[...]).astype(BF16)
    mixed = (jnp.dot(na, woa_ref[...], preferred_element_type=F32)
             + jnp.dot(nb, wob_ref[...], preferred_element_type=F32))
    x1 = _layer_norm(DN_ALPHA * x_ref[...] + mixed, lng_ref[...], lnb_ref[...])
    _tok_store(x1_ref, x1)

    logits = lax.dot_general(rw_ref[...], x1.astype(BF16), (((1,), (1,)), ((), ())),
                             preferred_element_type=F32) + rb_ref[...]
    tm = logits.shape[1]
    eidx = lax.broadcasted_iota(jnp.int32, (N_EXPERTS, tm), 0).astype(F32)
    work = logits
    vals, idxs, hots = [], [], []
    for _ in range(TOP_K):
        m = jnp.max(work, axis=0, keepdims=True)
        sel = jnp.min(jnp.where(work == m, eidx, float(N_EXPERTS)), axis=0, keepdims=True)
        hot = eidx == sel
        vals.append(m)
        idxs.append(sel)
        hots.append(hot)
        work = jnp.where(hot, -jnp.inf, work)
    es = [jnp.exp(v - vals[0]) for v in vals]
    den = es[0] + es[1] + es[2] + es[3]
    gate_ref[...] = jnp.concatenate([e / den for e in es], axis=0)
    idx_ref[...] = jnp.concatenate(idxs, axis=0).astype(jnp.int32)

    hot_all = hots[0] | hots[1] | hots[2] | hots[3]
    onehot = jnp.where(hot_all, 1.0, 0.0)
    before = jnp.dot(onehot.astype(BF16), tri_ref[...], preferred_element_type=F32)
    before = before + carry_ref[:, 0:1]
    ranks = [jnp.sum(jnp.where(hot, before, 0.0), axis=0, keepdims=True) for hot in hots]
    rank_ref[...] = jnp.concatenate(ranks, axis=0).astype(jnp.int32)
    carry_ref[...] = carry_ref[...] + jnp.sum(onehot, axis=1, keepdims=True)
    cnt_ref[...] = carry_ref[...]


def _mix(oa, ob, x2d, woa, wob, ga, gb, lng, lnb, rw_t, rb, tri, cnt_in):
    n = x2d.shape[0]
    tm = MIX_TM
    assert n % tm == 0
    row = lambda i: (i, 0)
    col = lambda i: (0, i)
    const = lambda i: (0, 0)
    full = lambda a: pl.BlockSpec(a.shape, const)
    return pl.pallas_call(
        _mix_kernel,
        grid=(n // tm,),
        in_specs=[
            pl.BlockSpec((tm, NA_WIDTH), row),
            pl.BlockSpec((tm, GQA_WIDTH), row),
            pl.BlockSpec((tm, D_MODEL), row),
            full(woa), full(wob), full(ga), full(gb), full(lng), full(lnb),
            full(rw_t), full(rb), full(tri), full(cnt_in),
        ],
        out_specs=[
            pl.BlockSpec((tm * TOK_ROWS, LANES), row),
            pl.BlockSpec((TOP_K, tm), col),
            pl.BlockSpec((TOP_K, tm), col),
            pl.BlockSpec((TOP_K, tm), col),
            pl.BlockSpec((N_EXPERTS, LANES), const),
        ],
        out_shape=[
            jax.ShapeDtypeStruct((n * TOK_ROWS, LANES), F32),
            jax.ShapeDtypeStruct((TOP_K, n), jnp.int32),
            jax.ShapeDtypeStruct((TOP_K, n), F32),
            jax.ShapeDtypeStruct((TOP_K, n), jnp.int32),
            jax.ShapeDtypeStruct((N_EXPERTS, LANES), F32),
        ],
        scratch_shapes=[pltpu.VMEM((N_EXPERTS, LANES), F32)],
        compiler_params=_cparams(("arbitrary",)),
        name="mix",
    )(oa, ob, x2d, woa, wob, ga, gb, lng, lnb, rw_t, rb, tri, cnt_in)


def _slot_kernel(pad_ref, idx_ref, rank_ref, o_ref):
    idx = idx_ref[...]
    start = jnp.zeros_like(idx)
    for e in range(N_EXPERTS):
        start = jnp.where(idx == e, pad_ref[e], start)
    o_ref[...] = (start + rank_ref[...]) * TOK_ROWS


def _slots(pad_start, idx_t, rank_t):
    n = idx_t.shape[1]
    tm = min(SLOT_TM, n)
    assert n % tm == 0
    col = lambda i, *_: (0, i)
    return pl.pallas_call(
        _slot_kernel,
        grid_spec=pltpu.PrefetchScalarGridSpec(
            num_scalar_prefetch=1,
            grid=(n // tm,),
            in_specs=[pl.BlockSpec((TOP_K, tm), col), pl.BlockSpec((TOP_K, tm), col)],
            out_specs=pl.BlockSpec((TOP_K, tm), col),
        ),
        out_shape=jax.ShapeDtypeStruct((TOP_K, n), jnp.int32),
        compiler_params=_cparams(("parallel",)),
        name="slots",
    )(pad_start, idx_t, rank_t)


def _dispatch_kernel(tail_ref, has_tail_ref, dest_ref, x_ref, *rest, zero_tails):
    if zero_tails:
        xs_ref, zeros_ref, sem = rest
    else:
        _, xs_ref, sem = rest
    tm = x_ref.shape[0] // TOK_ROWS

    if zero_tails:
        @pl.when(pl.program_id(0) == 0)
        def _():
            zeros_ref[...] = jnp.zeros_like(zeros_ref)

            def tail_copy(e):
                start = pl.multiple_of(tail_ref[e], EXP_BLK * TOK_ROWS)
                return pltpu.make_async_copy(zeros_ref, xs_ref.at[pl.ds(start, EXP_BLK * TOK_ROWS), :], sem)

            for e in range(N_EXPERTS):
                @pl.when(has_tail_ref[e] != 0)
                def _():
                    tail_copy(e).start()
            for e in range(N_EXPERTS):
                @pl.when(has_tail_ref[e] != 0)
                def _():
                    tail_copy(e).wait()

    def row_copy(i, k):
        src = pl.multiple_of(i * TOK_ROWS, TOK_ROWS)
        dst = pl.multiple_of(dest_ref[0, 0, k * tm + i], TOK_ROWS)
        return pltpu.make_async_copy(x_ref.at[pl.ds(src, TOK_ROWS), :], xs_ref.at[pl.ds(dst, TOK_ROWS), :], sem)

    def issue(i, c):
        for k in range(TOP_K):
            row_copy(i, k).start(priority=k % 2)
        return c

    lax.fori_loop(0, tm, issue, 0, unroll=4)
    for k in range(TOP_K):
        pltpu.make_async_copy(x_ref, xs_ref.at[pl.ds(0, tm * TOK_ROWS), :], sem).wait()


def _dispatch(tail, has_tail, dest_tiles, x1, xs_prev, n_slots):
    n = x1.shape[0] // TOK_ROWS
    tm = dest_tiles.shape[2] // TOP_K
    assert n % tm == 0
    zero_tails = xs_prev is None
    in_specs = [
        pl.BlockSpec((1, 1, TOP_K * tm), lambda i, *_: (i, 0, 0), memory_space=pltpu.SMEM),
        pl.BlockSpec((tm * TOK_ROWS, LANES), lambda i, *_: (i, 0)),
    ]
    args = [dest_tiles, x1]
    scratch = []
    aliases = {}
    if zero_tails:
        scratch.append(pltpu.VMEM((EXP_BLK * TOK_ROWS, LANES), F32))
    else:
        in_specs.append(pl.BlockSpec(memory_space=pl.ANY))
        args.append(xs_prev)
        aliases = {4: 0}
    scratch.append(pltpu.SemaphoreType.DMA(()))
    return pl.pallas_call(
        functools.partial(_dispatch_kernel, zero_tails=zero_tails),
        grid_spec=pltpu.PrefetchScalarGridSpec(
            num_scalar_prefetch=2,
            grid=(n // tm,),
            in_specs=in_specs,
            out_specs=pl.BlockSpec(memory_space=pl.ANY),
            scratch_shapes=scratch,
        ),
        out_shape=jax.ShapeDtypeStruct((n_slots * TOK_ROWS, LANES), F32),
        input_output_aliases=aliases,
        compiler_params=pltpu.CompilerParams(dimension_semantics=("arbitrary",), vmem_limit_bytes=VMEM_LIMIT,
                                             has_side_effects=True),
        name="dispatch",
    )(tail, has_tail, *args)


def _w1_prep_kernel(w_ref, perm_ref, o_ref):
    for c in range(w_ref.shape[2] // W1_GROUP):
        cols = slice(c * W1_GROUP, (c + 1) * W1_GROUP)
        w = w_ref[0, :, cols].astype(BF16)
        o_ref[0, :, cols] = jnp.dot(w, perm_ref[...], preferred_element_type=F32).astype(BF16)


def _w1_prep(w1):
    e, d, f2 = w1.shape
    half = f2 // 2
    j = jnp.arange(W1_GROUP)
    dst = jnp.where(j % 2 == 0, j // 2, LANES + j // 2)
    perm = (dst[:, None] == jnp.arange(W1_GROUP)[None, :]).astype(BF16)
    return pl.pallas_call(
        _w1_prep_kernel,
        grid=(e, 2),
        in_specs=[pl.BlockSpec((1, d, half), lambda i, j: (i, 0, j)),
                  pl.BlockSpec((W1_GROUP, W1_GROUP), lambda i, j: (0, 0))],
        out_specs=pl.BlockSpec((1, d, half), lambda i, j: (i, 0, j)),
        out_shape=jax.ShapeDtypeStruct((e, d, f2), BF16),
        compiler_params=_cparams(("parallel", "parallel")),
        name="w1prep",
    )(w1, perm)


def _expert_kernel(be_ref, nused_ref, xs_ref, w1_ref, w2_ref, b1_ref, b2_ref, o_ref):
    @pl.when(pl.program_id(0) < nused_ref[0])
    def _():
        x = _tok_load(xs_ref, EXP_BLK).astype(BF16)
        h = jnp.dot(x, w1_ref[0], preferred_element_type=F32) + b1_ref[0]
        groups = range(h.shape[1] // W1_GROUP)
        glu = jnp.concatenate([h[:, c * W1_GROUP: c * W1_GROUP + LANES] for c in groups], axis=1)
        lin = jnp.concatenate([h[:, c * W1_GROUP + LANES: (c + 1) * W1_GROUP] for c in groups], axis=1)
        glu = jnp.minimum(glu, SWIGLU_LIMIT)
        lin = jnp.clip(lin, -SWIGLU_LIMIT, SWIGLU_LIMIT)
        act = glu * jax.nn.sigmoid(SWIGLU_ALPHA * glu) * (lin + 1.0)
        _tok_store(o_ref, jnp.dot(act.astype(BF16), w2_ref[0], preferred_element_type=F32) + b2_ref[0])


def _experts(block_expert, n_used, xs, w1p, w2, b1p, b2):
    n_slots = xs.shape[0] // TOK_ROWS
    n_blocks = n_slots // EXP_BLK
    slot = lambda i, be, nu: (jnp.minimum(i, nu[0] - 1), 0)
    wsel = lambda i, be, nu: (be[i], 0, 0)
    return pl.pallas_call(
        _expert_kernel,
        grid_spec=pltpu.PrefetchScalarGridSpec(
            num_scalar_prefetch=2,
            grid=(n_blocks,),
            in_specs=[
                pl.BlockSpec((EXP_BLK * TOK_ROWS, LANES), slot),
                pl.BlockSpec((1, D_MODEL, 2 * D_FF), wsel),
                pl.BlockSpec((1, D_FF, D_MODEL), wsel),
                pl.BlockSpec((1, 1, 2 * D_FF), wsel),
                pl.BlockSpec((1, 1, D_MODEL), wsel),
            ],
            out_specs=pl.BlockSpec((EXP_BLK * TOK_ROWS, LANES), slot),
        ),
        out_shape=jax.ShapeDtypeStruct((n_slots * TOK_ROWS, LANES), F32),
        compiler_params=_cparams(("arbitrary",)),
        name="experts",
    )(block_expert, n_used, xs, w1p, w2, b1p, b2)


def _combine_kernel(dest_ref, dest_next_ref, x1_ref, gate_ref, lng_ref, lnb_ref, ys_ref, o_ref, buf_ref, sems):
    tm = x1_ref.shape[0] // TOK_ROWS
    step = pl.program_id(0)
    slot = step % 2

    def gather(idx_ref, into):
        def issue(i, c):
            for k in range(TOP_K):
                src = pl.multiple_of(idx_ref[0, 0, k * tm + i], TOK_ROWS)
                dst = pl.multiple_of(i * TOK_ROWS, TOK_ROWS)
                pltpu.make_async_copy(ys_ref.at[pl.ds(src, TOK_ROWS), :],
                                      buf_ref.at[into, k, pl.ds(dst, TOK_ROWS), :], sems.at[into]
                                      ).start(priority=k % 2)
            return c

        lax.fori_loop(0, tm, issue, 0, unroll=4)

    @pl.when(step == 0)
    def _():
        gather(dest_ref, slot)

    @pl.when(step + 1 < pl.num_programs(0))
    def _():
        gather(dest_next_ref, 1 - slot)

    for k in range(TOP_K):
        pltpu.make_async_copy(ys_ref.at[pl.ds(0, tm * TOK_ROWS), :], buf_ref.at[slot, k], sems.at[slot]).wait()
    gate = gate_ref[...]
    y = _tok_load(buf_ref, tm, (slot, 0)) * gate[:, 0:1]
    for k in range(1, TOP_K):
        y = y + _tok_load(buf_ref, tm, (slot, k)) * gate[:, k:k + 1]
    o_ref[...] = _layer_norm(DN_ALPHA * _tok_load(x1_ref, tm) + y, lng_ref[...], lnb_ref[...])


def _combine(dest_tiles, x1, gate, lng, lnb, ys):
    n = x1.shape[0] // TOK_ROWS
    tm = COMB_TM
    assert n % tm == 0
    n_tiles = n // tm
    return pl.pallas_call(
        _combine_kernel,
        grid=(n_tiles,),
        in_specs=[
            pl.BlockSpec((1, 1, TOP_K * tm), lambda i: (i, 0, 0), memory_space=pltpu.SMEM),
            pl.BlockSpec((1, 1, TOP_K * tm), lambda i: (jnp.minimum(i + 1, n_tiles - 1), 0, 0),
                         memory_space=pltpu.SMEM),
            pl.BlockSpec((tm * TOK_ROWS, LANES), lambda i: (i, 0)),
            pl.BlockSpec((tm, TOP_K), lambda i: (i, 0)),
            pl.BlockSpec((1, D_MODEL), lambda i: (0, 0)),
            pl.BlockSpec((1, D_MODEL), lambda i: (0, 0)),
            pl.BlockSpec(memory_space=pl.ANY),
        ],
        out_specs=pl.BlockSpec((tm, D_MODEL), lambda i: (i, 0)),
        out_shape=jax.ShapeDtypeStruct((n, D_MODEL), F32),
        scratch_shapes=[pltpu.VMEM((2, TOP_K, tm * TOK_ROWS, LANES), F32), pltpu.SemaphoreType.DMA((2,))],
        compiler_params=_cparams(("arbitrary",)),
        name="combine",
    )(dest_tiles, dest_tiles, x1, gate, lng, lnb, ys)


def _rope_tables(t):
    pos = jnp.arange(t)
    row = (pos // GRID_W).astype(F32)
    col = (pos % GRID_W).astype(F32)
    inv = ROPE_THETA ** (-jnp.arange(0, ROPE_AXIS_DIM, 2, dtype=F32) / ROPE_AXIS_DIM)
    ar = row[:, None] * inv[None, :]
    ac = col[:, None] * inv[None, :]
    cos = jnp.concatenate([jnp.cos(ar), jnp.cos(ar), jnp.cos(ac), jnp.cos(ac)], axis=1)
    sin = jnp.concatenate([-jnp.sin(ar), jnp.sin(ar), -jnp.sin(ac), jnp.sin(ac)], axis=1)
    return jnp.tile(cos, (1, 2)), jnp.tile(sin, (1, 2))


def _prep_w_in(w):
    qa = w[:, :NA_WIDTH] * SCORE_SCALE
    kva = w[:, NA_WIDTH:3 * NA_WIDTH]
    qb = _gqa_out_order(w[:, 3 * NA_WIDTH:3 * NA_WIDTH + GQA_WIDTH].T).T
    rest = w[:, 3 * NA_WIDTH + GQA_WIDTH:]
    return jnp.concatenate([qa, kva, qb, rest], axis=1).astype(BF16)


def _gqa_out_order(a):
    rest = a.shape[1:]
    a = a.reshape((GQA_KV_HEADS, GQA_GROUP, HEAD_DIM) + rest)
    return jnp.swapaxes(a, 0, 1).reshape((GQA_WIDTH,) + rest)


def _dest_tiles(dest_t, tm):
    n = dest_t.shape[1]
    return dest_t.reshape(TOP_K, n // tm, tm).transpose(1, 0, 2).reshape(n // tm, 1, TOP_K * tm)


def kernel(x_prompt, x_sample, w_in, rpb, q_norm_g, k_norm_g, g_out_na, g_out_gqa, w_o, ln1_g, ln1_b,
           router_w, router_b, w1, b1, w2, b2, ln2_g, ln2_b):
    assert GQA_KV_HEADS == 2 and KV_WIDTH == LANES
    xs_in = [x_prompt, x_sample]
    l = 0
    w_proj = _prep_w_in(w_in[l])
    gq = jnp.tile(q_norm_g[l] * SCORE_SCALE, 2).reshape(1, LANES)
    gk = jnp.tile(k_norm_g[l], 2).reshape(1, LANES)
    bias = _na_bias_tables(rpb[l])
    woa = w_o[l][:NA_WIDTH].astype(BF16)
    wob = _gqa_out_order(w_o[l][NA_WIDTH:]).astype(BF16)
    ga = g_out_na[l].reshape(1, NA_WIDTH)
    gb = _gqa_out_order(g_out_gqa[l]).reshape(1, GQA_WIDTH)
    ln1g, ln1b = ln1_g[l].reshape(1, D_MODEL), ln1_b[l].reshape(1, D_MODEL)
    ln2g, ln2b = ln2_g[l].reshape(1, D_MODEL), ln2_b[l].reshape(1, D_MODEL)
    rw_t = router_w[l].T.astype(BF16)
    rb = router_b[l].reshape(N_EXPERTS, 1)
    tri = (jnp.arange(MIX_TM)[:, None] < jnp.arange(MIX_TM)[None, :]).astype(BF16)
    w1p = _w1_prep(w1[l])
    w2b = w2[l].astype(BF16)
    b1p = jnp.swapaxes(b1[l].reshape(N_EXPERTS, 2 * D_FF // W1_GROUP, LANES, 2), 2, 3).reshape(N_EXPERTS, 1, 2 * D_FF)
    b2r = b2[l].reshape(N_EXPERTS, 1, D_MODEL)

    x1s, idxs, gates, ranks = [], [], [], []
    cnt = jnp.zeros((N_EXPERTS, LANES), F32)
    for x in xs_in:
        b, t, _ = x.shape
        x2d = x.reshape(b * t, D_MODEL)
        cos, sin = _rope_tables(t)
        qa, ka, va, qb, kb, vlo, vhi = _proj(x2d, t, w_proj, cos, sin, gq, gk)
        sh = lambda a: a.reshape(b, t, a.shape[-1])
        oa = _na(sh(qa), sh(ka), sh(va), bias).reshape(b * t, NA_WIDTH)
        ob = _gqa(sh(qb), sh(kb), sh(vlo), sh(vhi)).reshape(b * t, GQA_WIDTH)
        x1, idx_t, gate_t, rank_t, cnt = _mix(oa, ob, x2d, woa, wob, ga, gb, ln1g, ln1b, rw_t, rb, tri, cnt)
        x1s.append(x1)
        idxs.append(idx_t)
        gates.append(gate_t)
        ranks.append(rank_t)

    counts = cnt[:, 0].astype(jnp.int32)
    nblk = (counts + EXP_BLK - 1) // EXP_BLK
    blk_end = jnp.cumsum(nblk)
    pad_start = (blk_end - nblk) * EXP_BLK
    n_assign = TOP_K * sum(x.shape[0] * x.shape[1] for x in xs_in)
    n_blocks = -(-n_assign // EXP_BLK) + N_EXPERTS
    n_slots = n_blocks * EXP_BLK
    blocks = jnp.arange(n_blocks, dtype=jnp.int32)
    block_expert = jnp.minimum(jnp.sum((blk_end[None, :] <= blocks[:, None]).astype(jnp.int32), axis=1),
                               N_EXPERTS - 1)
    n_used = blk_end[-1:].astype(jnp.int32)
    tail = ((blk_end - 1) * (EXP_BLK * TOK_ROWS)).astype(jnp.int32)
    has_tail = (counts % EXP_BLK != 0).astype(jnp.int32)
    dests = [_slots(pad_start.astype(jnp.int32), idx_t, rank_t) for idx_t, rank_t in zip(idxs, ranks)]

    slots = None
    for x1, dest_t in zip(x1s, dests):
        slots = _dispatch(tail, has_tail, _dest_tiles(dest_t, min(DISP_TM, dest_t.shape[1])), x1, slots, n_slots)
    ys = _experts(block_expert, n_used, slots, w1p, w2b, b1p, b2r)
    outs = []
    for x, x1, dest_t, gate_t in zip(xs_in, x1s, dests, gates):
        y = _combine(_dest_tiles(dest_t, COMB_TM), x1, gate_t.T, ln2g, ln2b, ys)
        outs.append(y.reshape(x.shape))
    return tuple(outs)
```

```python
import functools

import jax
import jax.numpy as jnp
from jax import lax
from jax.experimental import pallas as pl
from jax.experimental.pallas import tpu as pltpu

D_MODEL = 1024
GRID_W = 64
HEAD_DIM = 64
NA_HEADS = 8
GQA_HEADS = 8
GQA_KV_HEADS = 2
GQA_GROUP = GQA_HEADS // GQA_KV_HEADS
NA_WIDTH = NA_HEADS * HEAD_DIM
GQA_WIDTH = GQA_HEADS * HEAD_DIM
KV_WIDTH = GQA_KV_HEADS * HEAD_DIM
NA_WIN_R = 8
NA_WIN_C = 16
ROPE_AXIS_DIM = HEAD_DIM // 2
ROPE_THETA = 10000.0
N_EXPERTS = 32
TOP_K = 4
D_FF = D_MODEL
SWIGLU_ALPHA = 1.702
SWIGLU_LIMIT = 7.0
DEPTH = 1
DN_ALPHA = (2.0 * DEPTH) ** 0.25
NEG_INF = -1e30
LOG2_E = 1.4426950408889634
SCORE_SCALE = HEAD_DIM ** -0.5 * LOG2_E
RMS_EPS = 1e-6
LN_EPS = 1e-5

LANES = 128
PROJ_COLS = 3 * NA_WIDTH + GQA_WIDTH + 2 * KV_WIDTH

PROJ_TM = 512
NA_ROWS = 8
NA_PASS_HEADS = 4
GQA_SCORE_ELEMS = 1024 * 2048
MIX_TM = 512
DISP_TM = 2048
COMB_TM = 256
EXP_BLK = 1024
W1_GROUP = 2 * LANES
SLOT_TM = 2048
V7X_VMEM_BYTES = 64 * 1024 * 1024
VMEM_LIMIT = V7X_VMEM_BYTES - 8 * 1024 * 1024

F32 = jnp.float32
BF16 = jnp.bfloat16

TOK_ROWS = D_MODEL // LANES


def _tok_load(ref, n, lead=()):
    return jnp.concatenate([ref[lead + (pl.ds(s, n, stride=TOK_ROWS), slice(None))] for s in range(TOK_ROWS)],
                           axis=1)


def _tok_store(ref, val):
    n = val.shape[0]
    for s in range(TOK_ROWS):
        ref[pl.ds(s, n, stride=TOK_ROWS), :] = val[:, s * LANES:(s + 1) * LANES]


def _cparams(sem):
    return pltpu.CompilerParams(dimension_semantics=sem, vmem_limit_bytes=VMEM_LIMIT)


def _rope(y, cos, sin, first_half):
    half = ROPE_AXIS_DIM // 2
    partner = jnp.where(first_half, pltpu.roll(y, LANES - half, 1), pltpu.roll(y, half, 1))
    return y * cos + partner * sin


def _proj_kernel(x_ref, w_ref, cos_ref, sin_ref, gq_ref, gk_ref,
                 qa_ref, ka_ref, va_ref, qb_ref, kb_ref, vlo_ref, vhi_ref):
    x = x_ref[...].astype(BF16)

    def cols(c0, width):
        return jnp.dot(x, w_ref[:, c0:c0 + width], preferred_element_type=F32)

    cos = cos_ref[...]
    sin = sin_ref[...]
    lane = lax.broadcasted_iota(jnp.int32, cos.shape, 1)
    first_half = (lane % ROPE_AXIS_DIM) < ROPE_AXIS_DIM // 2
    lo = lane < HEAD_DIM

    def norm_rope(y, g):
        y2 = y * y
        ms_lo = jnp.sum(jnp.where(lo, y2, 0.0), axis=-1, keepdims=True)
        ms_hi = jnp.sum(jnp.where(lo, 0.0, y2), axis=-1, keepdims=True)
        ms = jnp.where(lo, ms_lo, ms_hi) * (1.0 / HEAD_DIM)
        return _rope(y * lax.rsqrt(ms + RMS_EPS) * g, cos, sin, first_half).astype(BF16)

    base = 3 * NA_WIDTH
    q = cols(base, GQA_WIDTH)
    gq = gq_ref[...]
    for g in range(GQA_GROUP):
        qb_ref[:, g * LANES:(g + 1) * LANES] = norm_rope(q[:, g * LANES:(g + 1) * LANES], gq)
    base += GQA_WIDTH
    kv = cols(base, 2 * KV_WIDTH)
    kb_ref[...] = norm_rope(kv[:, :KV_WIDTH], gk_ref[...])
    v = kv[:, KV_WIDTH:]
    vlo_ref[...] = jnp.where(lo, v, 1.0).astype(BF16)
    vhi_ref[...] = jnp.where(lo, 1.0, v).astype(BF16)
    qa_ref[...] = cols(0, NA_WIDTH).astype(BF16)
    ka_ref[...] = cols(NA_WIDTH, NA_WIDTH).astype(BF16)
    va_ref[...] = cols(2 * NA_WIDTH, NA_WIDTH).astype(BF16)


def _proj(x2d, seq_len, w, cos, sin, gq, gk):
    n = x2d.shape[0]
    tm = min(PROJ_TM, seq_len)
    assert n % tm == 0 and seq_len % tm == 0
    pos_blocks = seq_len // tm
    row = lambda i: (i, 0)
    const = lambda i: (0, 0)
    outs = [(NA_WIDTH, BF16)] * 3 + [(GQA_WIDTH, BF16)] + [(KV_WIDTH, BF16)] * 3
    return pl.pallas_call(
        _proj_kernel,
        grid=(n // tm,),
        in_specs=[
            pl.BlockSpec((tm, D_MODEL), row),
            pl.BlockSpec((D_MODEL, PROJ_COLS), const),
            pl.BlockSpec((tm, LANES), lambda i: (i % pos_blocks, 0)),
            pl.BlockSpec((tm, LANES), lambda i: (i % pos_blocks, 0)),
            pl.BlockSpec((1, LANES), const),
            pl.BlockSpec((1, LANES), const),
        ],
        out_specs=[pl.BlockSpec((tm, c), row) for c, _ in outs],
        out_shape=[jax.ShapeDtypeStruct((n, c), dt) for c, dt in outs],
        compiler_params=_cparams(("parallel",)),
        name="proj",
    )(x2d, w, cos, sin, gq, gk)


def _na_kernel(q_ref, k_ref, v_ref, bias_ref, o_ref, *, rows):
    j = pl.program_id(1)
    width = NA_PASS_HEADS * HEAD_DIM
    lane_head = lax.broadcasted_iota(jnp.int32, (GRID_W, width), 1) // HEAD_DIM

    def one_row(rr, carry):
        r = j * NA_ROWS + rr
        rs = jnp.clip(r - NA_WIN_R // 2, 0, rows - NA_WIN_R)
        var = r - rs
        k0 = pl.multiple_of(rs * GRID_W, GRID_W)
        q0 = pl.multiple_of(rr * GRID_W, GRID_W)
        for g in range(NA_HEADS // NA_PASS_HEADS):
            cols = slice(g * width, (g + 1) * width)
            heads = range(g * NA_PASS_HEADS, (g + 1) * NA_PASS_HEADS)
            q = q_ref[0, pl.ds(q0, GRID_W), cols]
            zero = jnp.zeros_like(q)
            qs = jnp.concatenate([jnp.where(lane_head == a, q, zero) for a in range(NA_PASS_HEADS)], axis=0)
            k = k_ref[0, pl.ds(k0, NA_WIN_R * GRID_W), cols]
            v = v_ref[0, pl.ds(k0, NA_WIN_R * GRID_W), cols]
            s = lax.dot_general(qs, k, (((1,), (1,)), ((), ())), preferred_element_type=F32)
            s = s + jnp.concatenate([bias_ref[var, h] for h in heads], axis=0)
            m = jnp.max(s, axis=-1, keepdims=True)
            p = jnp.exp2(s - m)
            l = jnp.sum(p, axis=-1, keepdims=True)
            os = jnp.dot(p.astype(BF16), v, preferred_element_type=F32) / l
            o = os[:GRID_W]
            for a in range(1, NA_PASS_HEADS):
                o = jnp.where(lane_head == a, os[a * GRID_W:(a + 1) * GRID_W], o)
            o_ref[0, pl.ds(q0, GRID_W), cols] = o.astype(BF16)
        return carry

    lax.fori_loop(0, NA_ROWS, one_row, 0, unroll=True)


def _na(qa, ka, va, bias):
    b, t, _ = qa.shape
    rows = t // GRID_W
    assert rows >= NA_WIN_R and rows % NA_ROWS == 0
    tq = NA_ROWS * GRID_W
    return pl.pallas_call(
        functools.partial(_na_kernel, rows=rows),
        grid=(b, rows // NA_ROWS),
        in_specs=[
            pl.BlockSpec((1, tq, NA_WIDTH), lambda i, j: (i, j, 0)),
            pl.BlockSpec((1, t, NA_WIDTH), lambda i, j: (i, 0, 0)),
            pl.BlockSpec((1, t, NA_WIDTH), lambda i, j: (i, 0, 0)),
            pl.BlockSpec(bias.shape, lambda i, j: (0, 0, 0, 0), pipeline_mode=pl.Buffered(1)),
        ],
        out_specs=pl.BlockSpec((1, tq, NA_WIDTH), lambda i, j: (i, j, 0)),
        out_shape=jax.ShapeDtypeStruct((b, t, NA_WIDTH), BF16),
        compiler_params=_cparams(("parallel", "parallel")),
        name="na",
    )(qa, ka, va, bias)


def _na_bias_tables(rpb):
    off = jnp.arange(NA_WIN_R)
    jrow = jnp.arange(NA_WIN_R)
    dr = jrow[None, :] - off[:, None] + (NA_WIN_R - 1)
    c = jnp.arange(GRID_W)
    cs = jnp.clip(c - NA_WIN_C // 2, 0, GRID_W - NA_WIN_C)
    col_ok = (c[None, :] >= cs[:, None]) & (c[None, :] < cs[:, None] + NA_WIN_C)
    dc = jnp.clip(c[None, :] - c[:, None], -(NA_WIN_C - 1), NA_WIN_C - 1) + (NA_WIN_C - 1)
    sel_r = (dr[:, :, None] == jnp.arange(2 * NA_WIN_R - 1)[None, None, :]).astype(F32)
    sel_c = (dc[:, :, None] == jnp.arange(2 * NA_WIN_C - 1)[None, None, :]).astype(F32)
    bias = jnp.einsum('vja,hab,qkb->hvqjk', sel_r, rpb.astype(F32), sel_c, precision=lax.Precision.HIGHEST)
    bias = jnp.where(col_ok[None, None, :, None, :], bias * LOG2_E, NEG_INF)
    return jnp.moveaxis(bias, 0, 1).reshape(NA_WIN_R, NA_HEADS, GRID_W, NA_WIN_R * GRID_W)


def _gqa_kernel(q_ref, k_ref, vlo_ref, vhi_ref, o_ref):
    k = k_ref[0]
    lane = lax.broadcasted_iota(jnp.int32, (q_ref.shape[1], LANES), 1)
    lo = lane < HEAD_DIM

    def scores(h):
        g, hi = divmod(h, 2)
        q = q_ref[0, :, g * LANES:(g + 1) * LANES]
        zero = jnp.zeros_like(q)
        q = jnp.where(lo, zero, q) if hi else jnp.where(lo, q, zero)
        return lax.dot_general(q, k, (((1,), (1,)), ((), ())), preferred_element_type=F32)

    def attend(s, v):
        p = jnp.exp2(s - jnp.max(s, axis=-1, keepdims=True)).astype(BF16)
        ov = jnp.dot(p, v, preferred_element_type=F32)
        return ov / pltpu.roll(ov, HEAD_DIM, 1)

    n_heads = 2 * GQA_GROUP
    outs = []
    s_next = scores(0)
    for h in range(n_heads):
        s = s_next
        if h + 1 < n_heads:
            s_next = scores(h + 1)
        outs.append(attend(s, vhi_ref[0] if h % 2 else vlo_ref[0]))
        if h % 2:
            g = h // 2
            o_ref[0, :, g * LANES:(g + 1) * LANES] = jnp.where(lo, outs[h - 1], outs[h]).astype(BF16)


def _gqa(qb, kb, vlo, vhi):
    b, t, _ = qb.shape
    tq = min(t, GQA_SCORE_ELEMS // t)
    assert t % tq == 0 and tq % 8 == 0
    return pl.pallas_call(
        _gqa_kernel,
        grid=(b, t // tq),
        in_specs=[
            pl.BlockSpec((1, tq, GQA_WIDTH), lambda i, j: (i, j, 0)),
            pl.BlockSpec((1, t, KV_WIDTH), lambda i, j: (i, 0, 0)),
            pl.BlockSpec((1, t, KV_WIDTH), lambda i, j: (i, 0, 0)),
            pl.BlockSpec((1, t, KV_WIDTH), lambda i, j: (i, 0, 0)),
        ],
        out_specs=pl.BlockSpec((1, tq, GQA_WIDTH), lambda i, j: (i, j, 0)),
        out_shape=jax.ShapeDtypeStruct((b, t, GQA_WIDTH), BF16),
        compiler_params=_cparams(("parallel", "parallel")),
        name="gqa",
    )(qb, kb, vlo, vhi)


def _layer_norm(z, g, b):
    mu = jnp.mean(z, axis=-1, keepdims=True)
    zc = z - mu
    var = jnp.mean(zc * zc, axis=-1, keepdims=True)
    return zc * lax.rsqrt(var + LN_EPS) * g + b


def _rms(o, g):
    return o * lax.rsqrt(jnp.mean(o * o, axis=-1, keepdims=True) + RMS_EPS) * g


def _mix_kernel(oa_ref, ob_ref, x_ref, woa_ref, wob_ref, ga_ref, gb_ref, lng_ref, lnb_ref,
                rw_ref, rb_ref, tri_ref, cnt_in_ref,
                x1_ref, idx_ref, gate_ref, rank_ref, cnt_ref, carry_ref):
    @pl.when(pl.program_id(0) == 0)
    def _():
        carry_ref[...] = cnt_in_ref[...]

    na = _rms(oa_ref[...].astype(F32), ga_ref[...]).astype(BF16)
    nb = _rms(ob_ref[...].astype(F32), gb_ref[...]).astype(BF16)
    mixed = (jnp.dot(na, woa_ref[...], preferred_element_type=F32)
             + jnp.dot(nb, wob_ref[...], preferred_element_type=F32))
    x1 = _layer_norm(DN_ALPHA * x_ref[...] + mixed, lng_ref[...], lnb_ref[...])
    _tok_store(x1_ref, x1)

    logits = lax.dot_general(rw_ref[...], x1.astype(BF16), (((1,), (1,)), ((), ())),
                             preferred_element_type=F32) + rb_ref[...]
    tm = logits.shape[1]
    eidx = lax.broadcasted_iota(jnp.int32, (N_EXPERTS, tm), 0).astype(F32)
    work = logits
    vals, idxs, hots = [], [], []
    for _ in range(TOP_K):
        m = jnp.max(work, axis=0, keepdims=True)
        sel = jnp.min(jnp.where(work == m, eidx, float(N_EXPERTS)), axis=0, keepdims=True)
        hot = eidx == sel
        vals.append(m)
        idxs.append(sel)
        hots.append(hot)
        work = jnp.where(hot, -jnp.inf, work)
    es = [jnp.exp(v - vals[0]) for v in vals]
    den = es[0] + es[1] + es[2] + es[3]
    gate_ref[...] = jnp.concatenate([e / den for e in es], axis=0)
    idx_ref[...] = jnp.concatenate(idxs, axis=0).astype(jnp.int32)

    hot_all = hots[0] | hots[1] | hots[2] | hots[3]
    onehot = jnp.where(hot_all, 1.0, 0.0)
    before = jnp.dot(onehot.astype(BF16), tri_ref[...], preferred_element_type=F32)
    before = before + carry_ref[:, 0:1]
    ranks = [jnp.sum(jnp.where(hot, before, 0.0), axis=0, keepdims=True) for hot in hots]
    rank_ref[...] = jnp.concatenate(ranks, axis=0).astype(jnp.int32)
    carry_ref[...] = carry_ref[...] + jnp.sum(onehot, axis=1, keepdims=True)
    cnt_ref[...] = carry_ref[...]


def _mix(oa, ob, x2d, woa, wob, ga, gb, lng, lnb, rw_t, rb, tri, cnt_in):
    n = x2d.shape[0]
    tm = MIX_TM
    assert n % tm == 0
    row = lambda i: (i, 0)
    col = lambda i: (0, i)
    const = lambda i: (0, 0)
    full = lambda a: pl.BlockSpec(a.shape, const)
    return pl.pallas_call(
        _mix_kernel,
        grid=(n // tm,),
        in_specs=[
            pl.BlockSpec((tm, NA_WIDTH), row),
            pl.BlockSpec((tm, GQA_WIDTH), row),
            pl.BlockSpec((tm, D_MODEL), row),
            full(woa), full(wob), full(ga), full(gb), full(lng), full(lnb),
            full(rw_t), full(rb), full(tri), full(cnt_in),
        ],
        out_specs=[
            pl.BlockSpec((tm * TOK_ROWS, LANES), row),
            pl.BlockSpec((TOP_K, tm), col),
            pl.BlockSpec((TOP_K, tm), col),
            pl.BlockSpec((TOP_K, tm), col),
            pl.BlockSpec((N_EXPERTS, LANES), const),
        ],
        out_shape=[
            jax.ShapeDtypeStruct((n * TOK_ROWS, LANES), F32),
            jax.ShapeDtypeStruct((TOP_K, n), jnp.int32),
            jax.ShapeDtypeStruct((TOP_K, n), F32),
            jax.ShapeDtypeStruct((TOP_K, n), jnp.int32),
            jax.ShapeDtypeStruct((N_EXPERTS, LANES), F32),
        ],
        scratch_shapes=[pltpu.VMEM((N_EXPERTS, LANES), F32)],
        compiler_params=_cparams(("arbitrary",)),
        name="mix",
    )(oa, ob, x2d, woa, wob, ga, gb, lng, lnb, rw_t, rb, tri, cnt_in)


def _slot_kernel(pad_ref, idx_ref, rank_ref, o_ref):
    idx = idx_ref[...]
    start = jnp.zeros_like(idx)
    for e in range(N_EXPERTS):
        start = jnp.where(idx == e, pad_ref[e], start)
    o_ref[...] = (start + rank_ref[...]) * TOK_ROWS


def _slots(pad_start, idx_t, rank_t):
    n = idx_t.shape[1]
    tm = min(SLOT_TM, n)
    assert n % tm == 0
    col = lambda i, *_: (0, i)
    return pl.pallas_call(
        _slot_kernel,
        grid_spec=pltpu.PrefetchScalarGridSpec(
            num_scalar_prefetch=1,
            grid=(n // tm,),
            in_specs=[pl.BlockSpec((TOP_K, tm), col), pl.BlockSpec((TOP_K, tm), col)],
            out_specs=pl.BlockSpec((TOP_K, tm), col),
        ),
        out_shape=jax.ShapeDtypeStruct((TOP_K, n), jnp.int32),
        compiler_params=_cparams(("parallel",)),
        name="slots",
    )(pad_start, idx_t, rank_t)


def _dispatch_kernel(tail_ref, has_tail_ref, dest_ref, x_ref, *rest, zero_tails):
    if zero_tails:
        xs_ref, zeros_ref, sem = rest
    else:
        _, xs_ref, sem = rest
    tm = x_ref.shape[0] // TOK_ROWS

    if zero_tails:
        @pl.when(pl.program_id(0) == 0)
        def _():
            zeros_ref[...] = jnp.zeros_like(zeros_ref)

            def tail_copy(e):
                start = pl.multiple_of(tail_ref[e], EXP_BLK * TOK_ROWS)
                return pltpu.make_async_copy(zeros_ref, xs_ref.at[pl.ds(start, EXP_BLK * TOK_ROWS), :], sem)

            for e in range(N_EXPERTS):
                @pl.when(has_tail_ref[e] != 0)
                def _():
                    tail_copy(e).start()
            for e in range(N_EXPERTS):
                @pl.when(has_tail_ref[e] != 0)
                def _():
                    tail_copy(e).wait()

    def row_copy(i, k):
        src = pl.multiple_of(i * TOK_ROWS, TOK_ROWS)
        dst = pl.multiple_of(dest_ref[0, 0, k * tm + i], TOK_ROWS)
        return pltpu.make_async_copy(x_ref.at[pl.ds(src, TOK_ROWS), :], xs_ref.at[pl.ds(dst, TOK_ROWS), :], sem)

    def issue(i, c):
        for k in range(TOP_K):
            row_copy(i, k).start(priority=k % 2)
        return c

    lax.fori_loop(0, tm, issue, 0, unroll=4)
    for k in range(TOP_K):
        pltpu.make_async_copy(x_ref, xs_ref.at[pl.ds(0, tm * TOK_ROWS), :], sem).wait()


def _dispatch(tail, has_tail, dest_tiles, x1, xs_prev, n_slots):
    n = x1.shape[0] // TOK_ROWS
    tm = dest_tiles.shape[2] // TOP_K
    assert n % tm == 0
    zero_tails = xs_prev is None
    in_specs = [
        pl.BlockSpec((1, 1, TOP_K * tm), lambda i, *_: (i, 0, 0), memory_space=pltpu.SMEM),
        pl.BlockSpec((tm * TOK_ROWS, LANES), lambda i, *_: (i, 0)),
    ]
    args = [dest_tiles, x1]
    scratch = []
    aliases = {}
    if zero_tails:
        scratch.append(pltpu.VMEM((EXP_BLK * TOK_ROWS, LANES), F32))
    else:
        in_specs.append(pl.BlockSpec(memory_space=pl.ANY))
        args.append(xs_prev)
        aliases = {4: 0}
    scratch.append(pltpu.SemaphoreType.DMA(()))
    return pl.pallas_call(
        functools.partial(_dispatch_kernel, zero_tails=zero_tails),
        grid_spec=pltpu.PrefetchScalarGridSpec(
            num_scalar_prefetch=2,
            grid=(n // tm,),
            in_specs=in_specs,
            out_specs=pl.BlockSpec(memory_space=pl.ANY),
            scratch_shapes=scratch,
        ),
        out_shape=jax.ShapeDtypeStruct((n_slots * TOK_ROWS, LANES), F32),
        input_output_aliases=aliases,
        compiler_params=pltpu.CompilerParams(dimension_semantics=("arbitrary",), vmem_limit_bytes=VMEM_LIMIT,
                                             has_side_effects=True),
        name="dispatch",
    )(tail, has_tail, *args)


def _w1_prep_kernel(w_ref, perm_ref, o_ref):
    for c in range(w_ref.shape[2] // W1_GROUP):
        cols = slice(c * W1_GROUP, (c + 1) * W1_GROUP)
        w = w_ref[0, :, cols].astype(BF16)
        o_ref[0, :, cols] = jnp.dot(w, perm_ref[...], preferred_element_type=F32).astype(BF16)


def _w1_prep(w1):
    e, d, f2 = w1.shape
    half = f2 // 2
    j = jnp.arange(W1_GROUP)
    dst = jnp.where(j % 2 == 0, j // 2, LANES + j // 2)
    perm = (dst[:, None] == jnp.arange(W1_GROUP)[None, :]).astype(BF16)
    return pl.pallas_call(
        _w1_prep_kernel,
        grid=(e, 2),
        in_specs=[pl.BlockSpec((1, d, half), lambda i, j: (i, 0, j)),
                  pl.BlockSpec((W1_GROUP, W1_GROUP), lambda i, j: (0, 0))],
        out_specs=pl.BlockSpec((1, d, half), lambda i, j: (i, 0, j)),
        out_shape=jax.ShapeDtypeStruct((e, d, f2), BF16),
        compiler_params=_cparams(("parallel", "parallel")),
        name="w1prep",
    )(w1, perm)


def _expert_kernel(be_ref, nused_ref, xs_ref, w1_ref, w2_ref, b1_ref, b2_ref, o_ref):
    @pl.when(pl.program_id(0) < nused_ref[0])
    def _():
        x = _tok_load(xs_ref, EXP_BLK).astype(BF16)
        h = jnp.dot(x, w1_ref[0], preferred_element_type=F32) + b1_ref[0]
        groups = range(h.shape[1] // W1_GROUP)
        glu = jnp.concatenate([h[:, c * W1_GROUP: c * W1_GROUP + LANES] for c in groups], axis=1)
        lin = jnp.concatenate([h[:, c * W1_GROUP + LANES: (c + 1) * W1_GROUP] for c in groups], axis=1)
        glu = jnp.minimum(glu, SWIGLU_LIMIT)
        lin = jnp.clip(lin, -SWIGLU_LIMIT, SWIGLU_LIMIT)
        act = glu * jax.nn.sigmoid(SWIGLU_ALPHA * glu) * (lin + 1.0)
        w2 = w2_ref[0].astype(BF16)
        _tok_store(o_ref, jnp.dot(act.astype(BF16), w2, preferred_element_type=F32) + b2_ref[0])


def _experts(block_expert, n_used, xs, w1p, w2, b1p, b2):
    n_slots = xs.shape[0] // TOK_ROWS
    n_blocks = n_slots // EXP_BLK
    slot = lambda i, be, nu: (jnp.minimum(i, nu[0] - 1), 0)
    wsel = lambda i, be, nu: (be[i], 0, 0)
    return pl.pallas_call(
        _expert_kernel,
        grid_spec=pltpu.PrefetchScalarGridSpec(
            num_scalar_prefetch=2,
            grid=(n_blocks,),
            in_specs=[
                pl.BlockSpec((EXP_BLK * TOK_ROWS, LANES), slot),
                pl.BlockSpec((1, D_MODEL, 2 * D_FF), wsel),
                pl.BlockSpec((1, D_FF, D_MODEL), wsel),
                pl.BlockSpec((1, 1, 2 * D_FF), wsel),
                pl.BlockSpec((1, 1, D_MODEL), wsel),
            ],
            out_specs=pl.BlockSpec((EXP_BLK * TOK_ROWS, LANES), slot),
        ),
        out_shape=jax.ShapeDtypeStruct((n_slots * TOK_ROWS, LANES), F32),
        compiler_params=_cparams(("arbitrary",)),
        name="experts",
    )(block_expert, n_used, xs, w1p, w2, b1p, b2)


def _combine_kernel(dest_ref, dest_next_ref, x1_ref, gate_ref, lng_ref, lnb_ref, ys_ref, o_ref, buf_ref, sems):
    tm = x1_ref.shape[0] // TOK_ROWS
    step = pl.program_id(0)
    slot = step % 2

    def gather(idx_ref, into):
        def issue(i, c):
            for k in range(TOP_K):
                src = pl.multiple_of(idx_ref[0, 0, k * tm + i], TOK_ROWS)
                dst = pl.multiple_of(i * TOK_ROWS, TOK_ROWS)
                pltpu.make_async_copy(ys_ref.at[pl.ds(src, TOK_ROWS), :],
                                      buf_ref.at[into, k, pl.ds(dst, TOK_ROWS), :], sems.at[into]
                                      ).start(priority=k % 2)
            return c

        lax.fori_loop(0, tm, issue, 0, unroll=4)

    @pl.when(step == 0)
    def _():
        gather(dest_ref, slot)

    @pl.when(step + 1 < pl.num_programs(0))
    def _():
        gather(dest_next_ref, 1 - slot)

    for k in range(TOP_K):
        pltpu.make_async_copy(ys_ref.at[pl.ds(0, tm * TOK_ROWS), :], buf_ref.at[slot, k], sems.at[slot]).wait()
    gate = gate_ref[...]
    y = _tok_load(buf_ref, tm, (slot, 0)) * gate[:, 0:1]
    for k in range(1, TOP_K):
        y = y + _tok_load(buf_ref, tm, (slot, k)) * gate[:, k:k + 1]
    o_ref[...] = _layer_norm(DN_ALPHA * _tok_load(x1_ref, tm) + y, lng_ref[...], lnb_ref[...])


def _combine(dest_tiles, x1, gate, lng, lnb, ys):
    n = x1.shape[0] // TOK_ROWS
    tm = COMB_TM
    assert n % tm == 0
    n_tiles = n // tm
    return pl.pallas_call(
        _combine_kernel,
        grid=(n_tiles,),
        in_specs=[
            pl.BlockSpec((1, 1, TOP_K * tm), lambda i: (i, 0, 0), memory_space=pltpu.SMEM),
            pl.BlockSpec((1, 1, TOP_K * tm), lambda i: (jnp.minimum(i + 1, n_tiles - 1), 0, 0),
                         memory_space=pltpu.SMEM),
            pl.BlockSpec((tm * TOK_ROWS, LANES), lambda i: (i, 0)),
            pl.BlockSpec((tm, TOP_K), lambda i: (i, 0)),
            pl.BlockSpec((1, D_MODEL), lambda i: (0, 0)),
            pl.BlockSpec((1, D_MODEL), lambda i: (0, 0)),
            pl.BlockSpec(memory_space=pl.ANY),
        ],
        out_specs=pl.BlockSpec((tm, D_MODEL), lambda i: (i, 0)),
        out_shape=jax.ShapeDtypeStruct((n, D_MODEL), F32),
        scratch_shapes=[pltpu.VMEM((2, TOP_K, tm * TOK_ROWS, LANES), F32), pltpu.SemaphoreType.DMA((2,))],
        compiler_params=_cparams(("arbitrary",)),
        name="combine",
    )(dest_tiles, dest_tiles, x1, gate, lng, lnb, ys)


def _rope_tables(t):
    pos = jnp.arange(t)
    row = (pos // GRID_W).astype(F32)
    col = (pos % GRID_W).astype(F32)
    inv = ROPE_THETA ** (-jnp.arange(0, ROPE_AXIS_DIM, 2, dtype=F32) / ROPE_AXIS_DIM)
    ar = row[:, None] * inv[None, :]
    ac = col[:, None] * inv[None, :]
    cos = jnp.concatenate([jnp.cos(ar), jnp.cos(ar), jnp.cos(ac), jnp.cos(ac)], axis=1)
    sin = jnp.concatenate([-jnp.sin(ar), jnp.sin(ar), -jnp.sin(ac), jnp.sin(ac)], axis=1)
    return jnp.tile(cos, (1, 2)), jnp.tile(sin, (1, 2))


def _prep_w_in(w):
    qa = w[:, :NA_WIDTH] * SCORE_SCALE
    kva = w[:, NA_WIDTH:3 * NA_WIDTH]
    qb = _gqa_out_order(w[:, 3 * NA_WIDTH:3 * NA_WIDTH + GQA_WIDTH].T).T
    rest = w[:, 3 * NA_WIDTH + GQA_WIDTH:]
    return jnp.concatenate([qa, kva, qb, rest], axis=1).astype(BF16)


def _gqa_out_order(a):
    rest = a.shape[1:]
    a = a.reshape((GQA_KV_HEADS, GQA_GROUP, HEAD_DIM) + rest)
    return jnp.swapaxes(a, 0, 1).reshape((GQA_WIDTH,) + rest)


def _dest_tiles(dest_t, tm):
    n = dest_t.shape[1]
    return dest_t.reshape(TOP_K, n // tm, tm).transpose(1, 0, 2).reshape(n // tm, 1, TOP_K * tm)


def kernel(x_prompt, x_sample, w_in, rpb, q_norm_g, k_norm_g, g_out_na, g_out_gqa, w_o, ln1_g, ln1_b,
           router_w, router_b, w1, b1, w2, b2, ln2_g, ln2_b):
    assert GQA_KV_HEADS == 2 and KV_WIDTH == LANES
    xs_in = [x_prompt, x_sample]
    l = 0
    w_proj = _prep_w_in(w_in[l])
    gq = jnp.tile(q_norm_g[l] * SCORE_SCALE, 2).reshape(1, LANES)
    gk = jnp.tile(k_norm_g[l], 2).reshape(1, LANES)
    bias = _na_bias_tables(rpb[l])
    woa = w_o[l][:NA_WIDTH].astype(BF16)
    wob = _gqa_out_order(w_o[l][NA_WIDTH:]).astype(BF16)
    ga = g_out_na[l].reshape(1, NA_WIDTH)
    gb = _gqa_out_order(g_out_gqa[l]).reshape(1, GQA_WIDTH)
    ln1g, ln1b = ln1_g[l].reshape(1, D_MODEL), ln1_b[l].reshape(1, D_MODEL)
    ln2g, ln2b = ln2_g[l].reshape(1, D_MODEL), ln2_b[l].reshape(1, D_MODEL)
    rw_t = router_w[l].T.astype(BF16)
    rb = router_b[l].reshape(N_EXPERTS, 1)
    tri = (jnp.arange(MIX_TM)[:, None] < jnp.arange(MIX_TM)[None, :]).astype(BF16)
    w1p = _w1_prep(w1[l])
    w2b = w2[l]
    b1p = jnp.swapaxes(b1[l].reshape(N_EXPERTS, 2 * D_FF // W1_GROUP, LANES, 2), 2, 3).reshape(N_EXPERTS, 1, 2 * D_FF)
    b2r = b2[l].reshape(N_EXPERTS, 1, D_MODEL)

    x1s, idxs, gates, ranks = [], [], [], []
    cnt = jnp.zeros((N_EXPERTS, LANES), F32)
    for x in xs_in:
        b, t, _ = x.shape
        x2d = x.reshape(b * t, D_MODEL)
        cos, sin = _rope_tables(t)
        qa, ka, va, qb, kb, vlo, vhi = _proj(x2d, t, w_proj, cos, sin, gq, gk)
        sh = lambda a: a.reshape(b, t, a.shape[-1])
        oa = _na(sh(qa), sh(ka), sh(va), bias).reshape(b * t, NA_WIDTH)
        ob = _gqa(sh(qb), sh(kb), sh(vlo), sh(vhi)).reshape(b * t, GQA_WIDTH)
        x1, idx_t, gate_t, rank_t, cnt = _mix(oa, ob, x2d, woa, wob, ga, gb, ln1g, ln1b, rw_t, rb, tri, cnt)
        x1s.append(x1)
        idxs.append(idx_t)
        gates.append(gate_t)
        ranks.append(rank_t)

    counts = cnt[:, 0].astype(jnp.int32)
    nblk = (counts + EXP_BLK - 1) // EXP_BLK
    blk_end = jnp.cumsum(nblk)
    pad_start = (blk_end - nblk) * EXP_BLK
    n_assign = TOP_K * sum(x.shape[0] * x.shape[1] for x in xs_in)
    n_blocks = -(-n_assign // EXP_BLK) + N_EXPERTS
    n_slots = n_blocks * EXP_BLK
    blocks = jnp.arange(n_blocks, dtype=jnp.int32)
    block_expert = jnp.minimum(jnp.sum((blk_end[None, :] <= blocks[:, None]).astype(jnp.int32), axis=1),
                               N_EXPERTS - 1)
    n_used = blk_end[-1:].astype(jnp.int32)
    tail = ((blk_end - 1) * (EXP_BLK * TOK_ROWS)).astype(jnp.int32)
    has_tail = (counts % EXP_BLK != 0).astype(jnp.int32)
    dests = [_slots(pad_start.astype(jnp.int32), idx_t, rank_t) for idx_t, rank_t in zip(idxs, ranks)]

    slots = None
    for x1, dest_t in zip(x1s, dests):
        slots = _dispatch(tail, has_tail, _dest_tiles(dest_t, min(DISP_TM, dest_t.shape[1])), x1, slots, n_slots)
    ys = _experts(block_expert, n_used, slots, w1p, w2b, b1p, b2r)
    outs = []
    for x, x1, dest_t, gate_t in zip(xs_in, x1s, dests, gates):
        y = _combine(_dest_tiles(dest_t, COMB_TM), x1, gate_t.T, ln2g, ln2b, ys)
        outs.append(y.reshape(x.shape))
    return tuple(outs)
```

```python
import functools

import jax
import jax.numpy as jnp
from jax import lax
from jax.experimental import pallas as pl
from jax.experimental.pallas import tpu as pltpu

D_MODEL = 1024
GRID_W = 64
HEAD_DIM = 64
NA_HEADS = 8
GQA_HEADS = 8
GQA_KV_HEADS = 2
GQA_GROUP = GQA_HEADS // GQA_KV_HEADS
NA_WIDTH = NA_HEADS * HEAD_DIM
GQA_WIDTH = GQA_HEADS * HEAD_DIM
KV_WIDTH = GQA_KV_HEADS * HEAD_DIM
NA_WIN_R = 8
NA_WIN_C = 16
ROPE_AXIS_DIM = HEAD_DIM // 2
ROPE_THETA = 10000.0
N_EXPERTS = 32
TOP_K = 4
D_FF = D_MODEL
SWIGLU_ALPHA = 1.702
SWIGLU_LIMIT = 7.0
DEPTH = 1
DN_ALPHA = (2.0 * DEPTH) ** 0.25
NEG_INF = -1e30
LOG2_E = 1.4426950408889634
SCORE_SCALE = HEAD_DIM ** -0.5 * LOG2_E
RMS_EPS = 1e-6
LN_EPS = 1e-5

LANES = 128
PROJ_COLS = 3 * NA_WIDTH + GQA_WIDTH + 2 * KV_WIDTH

PROJ_TM = 512
NA_ROWS = 16
NA_PASS_HEADS = 4
GQA_SCORE_ELEMS = 1024 * 2048
MIX_TM = 1024
DISP_TM = 2048
COMB_TM = 256
EXP_BLK = 1024
W1_GROUP = 2 * LANES
SLOT_TM = 2048
V7X_VMEM_BYTES = 64 * 1024 * 1024
VMEM_LIMIT = V7X_VMEM_BYTES - 8 * 1024 * 1024

F32 = jnp.float32
BF16 = jnp.bfloat16

TOK_ROWS = D_MODEL // LANES


def _tok_load(ref, n, lead=()):
    return jnp.concatenate([ref[lead + (pl.ds(s, n, stride=TOK_ROWS), slice(None))] for s in range(TOK_ROWS)],
                           axis=1)


def _tok_store(ref, val):
    n = val.shape[0]
    for s in range(TOK_ROWS):
        ref[pl.ds(s, n, stride=TOK_ROWS), :] = val[:, s * LANES:(s + 1) * LANES]


def _cparams(sem):
    return pltpu.CompilerParams(dimension_semantics=sem, vmem_limit_bytes=VMEM_LIMIT)


def _rope(y, cos, sin, first_half):
    half = ROPE_AXIS_DIM // 2
    partner = jnp.where(first_half, pltpu.roll(y, LANES - half, 1), pltpu.roll(y, half, 1))
    return y * cos + partner * sin


def _proj_kernel(x_ref, w_ref, cos_ref, sin_ref, gq_ref, gk_ref,
                 qa_ref, ka_ref, va_ref, qb_ref, kb_ref, vlo_ref, vhi_ref):
    x = x_ref[...].astype(BF16)

    def cols(c0, width):
        return jnp.dot(x, w_ref[:, c0:c0 + width], preferred_element_type=F32)

    cos = cos_ref[...]
    sin = sin_ref[...]
    lane = lax.broadcasted_iota(jnp.int32, cos.shape, 1)
    first_half = (lane % ROPE_AXIS_DIM) < ROPE_AXIS_DIM // 2
    lo = lane < HEAD_DIM

    def norm_rope(y, g):
        y2 = y * y
        ms_lo = jnp.sum(jnp.where(lo, y2, 0.0), axis=-1, keepdims=True)
        ms_hi = jnp.sum(jnp.where(lo, 0.0, y2), axis=-1, keepdims=True)
        ms = jnp.where(lo, ms_lo, ms_hi) * (1.0 / HEAD_DIM)
        return _rope(y * lax.rsqrt(ms + RMS_EPS) * g, cos, sin, first_half).astype(BF16)

    base = 3 * NA_WIDTH
    q = cols(base, GQA_WIDTH)
    gq = gq_ref[...]
    for g in range(GQA_GROUP):
        qb_ref[:, g * LANES:(g + 1) * LANES] = norm_rope(q[:, g * LANES:(g + 1) * LANES], gq)
    base += GQA_WIDTH
    kv = cols(base, 2 * KV_WIDTH)
    kb_ref[...] = norm_rope(kv[:, :KV_WIDTH], gk_ref[...])
    v = kv[:, KV_WIDTH:]
    vlo_ref[...] = jnp.where(lo, v, 1.0).astype(BF16)
    vhi_ref[...] = jnp.where(lo, 1.0, v).astype(BF16)
    qa_ref[...] = cols(0, NA_WIDTH).astype(BF16)
    ka_ref[...] = cols(NA_WIDTH, NA_WIDTH).astype(BF16)
    va_ref[...] = cols(2 * NA_WIDTH, NA_WIDTH).astype(BF16)


def _proj(x2d, seq_len, w, cos, sin, gq, gk):
    n = x2d.shape[0]
    tm = min(PROJ_TM, seq_len)
    assert n % tm == 0 and seq_len % tm == 0
    pos_blocks = seq_len // tm
    row = lambda i: (i, 0)
    const = lambda i: (0, 0)
    outs = [(NA_WIDTH, BF16)] * 3 + [(GQA_WIDTH, BF16)] + [(KV_WIDTH, BF16)] * 3
    return pl.pallas_call(
        _proj_kernel,
        grid=(n // tm,),
        in_specs=[
            pl.BlockSpec((tm, D_MODEL), row),
            pl.BlockSpec((D_MODEL, PROJ_COLS), const),
            pl.BlockSpec((tm, LANES), lambda i: (i % pos_blocks, 0)),
            pl.BlockSpec((tm, LANES), lambda i: (i % pos_blocks, 0)),
            pl.BlockSpec((1, LANES), const),
            pl.BlockSpec((1, LANES), const),
        ],
        out_specs=[pl.BlockSpec((tm, c), row) for c, _ in outs],
        out_shape=[jax.ShapeDtypeStruct((n, c), dt) for c, dt in outs],
        compiler_params=_cparams(("parallel",)),
        name="proj",
    )(x2d, w, cos, sin, gq, gk)


def _na_kernel(q_ref, k_ref, v_ref, bias_ref, o_ref, *, rows):
    j = pl.program_id(1)
    step_rows = q_ref.shape[1] // GRID_W
    width = NA_PASS_HEADS * HEAD_DIM
    lane_head = lax.broadcasted_iota(jnp.int32, (GRID_W, width), 1) // HEAD_DIM

    def one_row(rr, carry):
        r = j * step_rows + rr
        rs = jnp.clip(r - NA_WIN_R // 2, 0, rows - NA_WIN_R)
        var = r - rs
        k0 = pl.multiple_of(rs * GRID_W, GRID_W)
        q0 = pl.multiple_of(rr * GRID_W, GRID_W)
        for g in range(NA_HEADS // NA_PASS_HEADS):
            cols = slice(g * width, (g + 1) * width)
            heads = range(g * NA_PASS_HEADS, (g + 1) * NA_PASS_HEADS)
            q = q_ref[0, pl.ds(q0, GRID_W), cols]
            zero = jnp.zeros_like(q)
            qs = jnp.concatenate([jnp.where(lane_head == a, q, zero) for a in range(NA_PASS_HEADS)], axis=0)
            k = k_ref[0, pl.ds(k0, NA_WIN_R * GRID_W), cols]
            v = v_ref[0, pl.ds(k0, NA_WIN_R * GRID_W), cols]
            s = lax.dot_general(qs, k, (((1,), (1,)), ((), ())), preferred_element_type=F32)
            s = s + jnp.concatenate([bias_ref[var, h] for h in heads], axis=0)
            m = jnp.max(s, axis=-1, keepdims=True)
            p = jnp.exp2(s - m)
            l = jnp.sum(p, axis=-1, keepdims=True)
            os = jnp.dot(p.astype(BF16), v, preferred_element_type=F32) / l
            o = os[:GRID_W]
            for a in range(1, NA_PASS_HEADS):
                o = jnp.where(lane_head == a, os[a * GRID_W:(a + 1) * GRID_W], o)
            o_ref[0, pl.ds(q0, GRID_W), cols] = o.astype(BF16)
        return carry

    lax.fori_loop(0, step_rows, one_row, 0, unroll=True)


def _na(qa, ka, va, bias):
    b, t, _ = qa.shape
    rows = t // GRID_W
    step_rows = min(NA_ROWS, rows)
    assert rows >= NA_WIN_R and rows % step_rows == 0
    tq = step_rows * GRID_W
    return pl.pallas_call(
        functools.partial(_na_kernel, rows=rows),
        grid=(b, rows // step_rows),
        in_specs=[
            pl.BlockSpec((1, tq, NA_WIDTH), lambda i, j: (i, j, 0)),
            pl.BlockSpec((1, t, NA_WIDTH), lambda i, j: (i, 0, 0)),
            pl.BlockSpec((1, t, NA_WIDTH), lambda i, j: (i, 0, 0)),
            pl.BlockSpec(bias.shape, lambda i, j: (0, 0, 0, 0), pipeline_mode=pl.Buffered(1)),
        ],
        out_specs=pl.BlockSpec((1, tq, NA_WIDTH), lambda i, j: (i, j, 0)),
        out_shape=jax.ShapeDtypeStruct((b, t, NA_WIDTH), BF16),
        compiler_params=_cparams(("parallel", "parallel")),
        name="na",
    )(qa, ka, va, bias)


def _na_bias_tables(rpb):
    off = jnp.arange(NA_WIN_R)
    jrow = jnp.arange(NA_WIN_R)
    dr = jrow[None, :] - off[:, None] + (NA_WIN_R - 1)
    c = jnp.arange(GRID_W)
    cs = jnp.clip(c - NA_WIN_C // 2, 0, GRID_W - NA_WIN_C)
    col_ok = (c[None, :] >= cs[:, None]) & (c[None, :] < cs[:, None] + NA_WIN_C)
    dc = jnp.clip(c[None, :] - c[:, None], -(NA_WIN_C - 1), NA_WIN_C - 1) + (NA_WIN_C - 1)
    sel_r = (dr[:, :, None] == jnp.arange(2 * NA_WIN_R - 1)[None, None, :]).astype(F32)
    sel_c = (dc[:, :, None] == jnp.arange(2 * NA_WIN_C - 1)[None, None, :]).astype(F32)
    bias = jnp.einsum('vja,hab,qkb->hvqjk', sel_r, rpb.astype(F32), sel_c, precision=lax.Precision.HIGHEST)
    bias = jnp.where(col_ok[None, None, :, None, :], bias * LOG2_E, NEG_INF)
    return jnp.moveaxis(bias, 0, 1).reshape(NA_WIN_R, NA_HEADS, GRID_W, NA_WIN_R * GRID_W)


def _gqa_kernel(q_ref, k_ref, vlo_ref, vhi_ref, o_ref):
    k = k_ref[0]
    lane = lax.broadcasted_iota(jnp.int32, (q_ref.shape[1], LANES), 1)
    lo = lane < HEAD_DIM

    def scores(h):
        g, hi = divmod(h, 2)
        q = q_ref[0, :, g * LANES:(g + 1) * LANES]
        zero = jnp.zeros_like(q)
        q = jnp.where(lo, zero, q) if hi else jnp.where(lo, q, zero)
        return lax.dot_general(q, k, (((1,), (1,)), ((), ())), preferred_element_type=F32)

    def attend(s, v):
        p = jnp.exp2(s - jnp.max(s, axis=-1, keepdims=True)).astype(BF16)
        ov = jnp.dot(p, v, preferred_element_type=F32)
        return ov / pltpu.roll(ov, HEAD_DIM, 1)

    n_heads = 2 * GQA_GROUP
    outs = []
    s_next = scores(0)
    for h in range(n_heads):
        s = s_next
        if h + 1 < n_heads:
            s_next = scores(h + 1)
        outs.append(attend(s, vhi_ref[0] if h % 2 else vlo_ref[0]))
        if h % 2:
            g = h // 2
            o_ref[0, :, g * LANES:(g + 1) * LANES] = jnp.where(lo, outs[h - 1], outs[h]).astype(BF16)


def _gqa(qb, kb, vlo, vhi):
    b, t, _ = qb.shape
    tq = min(t, GQA_SCORE_ELEMS // t)
    assert t % tq == 0 and tq % 8 == 0
    return pl.pallas_call(
        _gqa_kernel,
        grid=(b, t // tq),
        in_specs=[
            pl.BlockSpec((1, tq, GQA_WIDTH), lambda i, j: (i, j, 0)),
            pl.BlockSpec((1, t, KV_WIDTH), lambda i, j: (i, 0, 0)),
            pl.BlockSpec((1, t, KV_WIDTH), lambda i, j: (i, 0, 0)),
            pl.BlockSpec((1, t, KV_WIDTH), lambda i, j: (i, 0, 0)),
        ],
        out_specs=pl.BlockSpec((1, tq, GQA_WIDTH), lambda i, j: (i, j, 0)),
        out_shape=jax.ShapeDtypeStruct((b, t, GQA_WIDTH), BF16),
        compiler_params=_cparams(("parallel", "parallel")),
        name="gqa",
    )(qb, kb, vlo, vhi)


def _layer_norm(z, g, b):
    mu = jnp.mean(z, axis=-1, keepdims=True)
    zc = z - mu
    var = jnp.mean(zc * zc, axis=-1, keepdims=True)
    return zc * lax.rsqrt(var + LN_EPS) * g + b


def _rms(o, g):
    return o * lax.rsqrt(jnp.mean(o * o, axis=-1, keepdims=True) + RMS_EPS) * g


def _mix_kernel(oa_ref, ob_ref, x_ref, woa_ref, wob_ref, ga_ref, gb_ref, lng_ref, lnb_ref,
                rw_ref, rb_ref, tri_ref, cnt_in_ref,
                x1_ref, idx_ref, gate_ref, rank_ref, cnt_ref, carry_ref):
    @pl.when(pl.program_id(0) == 0)
    def _():
        carry_ref[...] = cnt_in_ref[...]

    na = _rms(oa_ref[...].astype(F32), ga_ref[...]).astype(BF16)
    nb = _rms(ob_ref[...].astype(F32), gb_ref[...]).astype(BF16)
    mixed = (jnp.dot(na, woa_ref[...], preferred_element_type=F32)
             + jnp.dot(nb, wob_ref[...], preferred_element_type=F32))
    x1 = _layer_norm(DN_ALPHA * x_ref[...] + mixed, lng_ref[...], lnb_ref[...])
    _tok_store(x1_ref, x1)

    logits = lax.dot_general(rw_ref[...], x1.astype(BF16), (((1,), (1,)), ((), ())),
                             preferred_element_type=F32) + rb_ref[...]
    tm = logits.shape[1]
    eidx = lax.broadcasted_iota(jnp.int32, (N_EXPERTS, tm), 0).astype(F32)
    work = logits
    vals, idxs, hots = [], [], []
    for _ in range(TOP_K):
        m = jnp.max(work, axis=0, keepdims=True)
        sel = jnp.min(jnp.where(work == m, eidx, float(N_EXPERTS)), axis=0, keepdims=True)
        hot = eidx == sel
        vals.append(m)
        idxs.append(sel)
        hots.append(hot)
        work = jnp.where(hot, -jnp.inf, work)
    es = [jnp.exp(v - vals[0]) for v in vals]
    den = es[0] + es[1] + es[2] + es[3]
    gate_ref[...] = jnp.concatenate([e / den for e in es], axis=0)
    idx_ref[...] = jnp.concatenate(idxs, axis=0).astype(jnp.int32)

    hot_all = hots[0] | hots[1] | hots[2] | hots[3]
    onehot = jnp.where(hot_all, 1.0, 0.0)
    before = jnp.dot(onehot.astype(BF16), tri_ref[...], preferred_element_type=F32)
    before = before + carry_ref[:, 0:1]
    ranks = [jnp.sum(jnp.where(hot, before, 0.0), axis=0, keepdims=True) for hot in hots]
    rank_ref[...] = jnp.concatenate(ranks, axis=0).astype(jnp.int32)
    carry_ref[...] = carry_ref[...] + jnp.sum(onehot, axis=1, keepdims=True)
    cnt_ref[...] = carry_ref[...]


def _mix(oa, ob, x2d, woa, wob, ga, gb, lng, lnb, rw_t, rb, cnt_in):
    n = x2d.shape[0]
    tm = min(MIX_TM, n)
    assert n % tm == 0
    tri = (jnp.arange(tm)[:, None] < jnp.arange(tm)[None, :]).astype(BF16)
    row = lambda i: (i, 0)
    col = lambda i: (0, i)
    const = lambda i: (0, 0)
    full = lambda a: pl.BlockSpec(a.shape, const)
    return pl.pallas_call(
        _mix_kernel,
        grid=(n // tm,),
        in_specs=[
            pl.BlockSpec((tm, NA_WIDTH), row),
            pl.BlockSpec((tm, GQA_WIDTH), row),
            pl.BlockSpec((tm, D_MODEL), row),
            full(woa), full(wob), full(ga), full(gb), full(lng), full(lnb),
            full(rw_t), full(rb), full(tri), full(cnt_in),
        ],
        out_specs=[
            pl.BlockSpec((tm * TOK_ROWS, LANES), row),
            pl.BlockSpec((TOP_K, tm), col),
            pl.BlockSpec((TOP_K, tm), col),
            pl.BlockSpec((TOP_K, tm), col),
            pl.BlockSpec((N_EXPERTS, LANES), const),
        ],
        out_shape=[
            jax.ShapeDtypeStruct((n * TOK_ROWS, LANES), F32),
            jax.ShapeDtypeStruct((TOP_K, n), jnp.int32),
            jax.ShapeDtypeStruct((TOP_K, n), F32),
            jax.ShapeDtypeStruct((TOP_K, n), jnp.int32),
            jax.ShapeDtypeStruct((N_EXPERTS, LANES), F32),
        ],
        scratch_shapes=[pltpu.VMEM((N_EXPERTS, LANES), F32)],
        compiler_params=_cparams(("arbitrary",)),
        name="mix",
    )(oa, ob, x2d, woa, wob, ga, gb, lng, lnb, rw_t, rb, tri, cnt_in)


def _slot_kernel(pad_ref, idx_ref, rank_ref, o_ref):
    idx = idx_ref[...]
    start = jnp.zeros_like(idx)
    for e in range(N_EXPERTS):
        start = jnp.where(idx == e, pad_ref[e], start)
    o_ref[...] = (start + rank_ref[...]) * TOK_ROWS


def _slots(pad_start, idx_t, rank_t):
    n = idx_t.shape[1]
    tm = min(SLOT_TM, n)
    assert n % tm == 0
    col = lambda i, *_: (0, i)
    return pl.pallas_call(
        _slot_kernel,
        grid_spec=pltpu.PrefetchScalarGridSpec(
            num_scalar_prefetch=1,
            grid=(n // tm,),
            in_specs=[pl.BlockSpec((TOP_K, tm), col), pl.BlockSpec((TOP_K, tm), col)],
            out_specs=pl.BlockSpec((TOP_K, tm), col),
        ),
        out_shape=jax.ShapeDtypeStruct((TOP_K, n), jnp.int32),
        compiler_params=_cparams(("parallel",)),
        name="slots",
    )(pad_start, idx_t, rank_t)


def _dispatch_kernel(tail_ref, has_tail_ref, dest_ref, x_ref, *rest, zero_tails):
    if zero_tails:
        xs_ref, zeros_ref, sem = rest
    else:
        _, xs_ref, sem = rest
    tm = x_ref.shape[0] // TOK_ROWS

    if zero_tails:
        @pl.when(pl.program_id(0) == 0)
        def _():
            zeros_ref[...] = jnp.zeros_like(zeros_ref)

            def tail_copy(e):
                start = pl.multiple_of(tail_ref[e], EXP_BLK * TOK_ROWS)
                return pltpu.make_async_copy(zeros_ref, xs_ref.at[pl.ds(start, EXP_BLK * TOK_ROWS), :], sem)

            for e in range(N_EXPERTS):
                @pl.when(has_tail_ref[e] != 0)
                def _():
                    tail_copy(e).start()
            for e in range(N_EXPERTS):
                @pl.when(has_tail_ref[e] != 0)
                def _():
                    tail_copy(e).wait()

    def row_copy(i, k):
        src = pl.multiple_of(i * TOK_ROWS, TOK_ROWS)
        dst = pl.multiple_of(dest_ref[0, 0, k * tm + i], TOK_ROWS)
        return pltpu.make_async_copy(x_ref.at[pl.ds(src, TOK_ROWS), :], xs_ref.at[pl.ds(dst, TOK_ROWS), :], sem)

    def issue(i, c):
        for k in range(TOP_K):
            row_copy(i, k).start(priority=k % 2)
        return c

    lax.fori_loop(0, tm, issue, 0, unroll=4)
    for k in range(TOP_K):
        pltpu.make_async_copy(x_ref, xs_ref.at[pl.ds(0, tm * TOK_ROWS), :], sem).wait()


def _dispatch(tail, has_tail, dest_tiles, x1, xs_prev, n_slots):
    n = x1.shape[0] // TOK_ROWS
    tm = dest_tiles.shape[2] // TOP_K
    assert n % tm == 0
    zero_tails = xs_prev is None
    in_specs = [
        pl.BlockSpec((1, 1, TOP_K * tm), lambda i, *_: (i, 0, 0), memory_space=pltpu.SMEM),
        pl.BlockSpec((tm * TOK_ROWS, LANES), lambda i, *_: (i, 0)),
    ]
    args = [dest_tiles, x1]
    scratch = []
    aliases = {}
    if zero_tails:
        scratch.append(pltpu.VMEM((EXP_BLK * TOK_ROWS, LANES), F32))
    else:
        in_specs.append(pl.BlockSpec(memory_space=pl.ANY))
        args.append(xs_prev)
        aliases = {4: 0}
    scratch.append(pltpu.SemaphoreType.DMA(()))
    return pl.pallas_call(
        functools.partial(_dispatch_kernel, zero_tails=zero_tails),
        grid_spec=pltpu.PrefetchScalarGridSpec(
            num_scalar_prefetch=2,
            grid=(n // tm,),
            in_specs=in_specs,
            out_specs=pl.BlockSpec(memory_space=pl.ANY),
            scratch_shapes=scratch,
        ),
        out_shape=jax.ShapeDtypeStruct((n_slots * TOK_ROWS, LANES), F32),
        input_output_aliases=aliases,
        compiler_params=pltpu.CompilerParams(dimension_semantics=("arbitrary",), vmem_limit_bytes=VMEM_LIMIT,
                                             has_side_effects=True),
        name="dispatch",
    )(tail, has_tail, *args)


def _w1_prep_kernel(w_ref, perm_ref, o_ref):
    for c in range(w_ref.shape[2] // W1_GROUP):
        cols = slice(c * W1_GROUP, (c + 1) * W1_GROUP)
        w = w_ref[0, :, cols].astype(BF16)
        o_ref[0, :, cols] = jnp.dot(w, perm_ref[...], preferred_element_type=F32).astype(BF16)


def _w1_prep(w1):
    e, d, f2 = w1.shape
    half = f2 // 2
    j = jnp.arange(W1_GROUP)
    dst = jnp.where(j % 2 == 0, j // 2, LANES + j // 2)
    perm = (dst[:, None] == jnp.arange(W1_GROUP)[None, :]).astype(BF16)
    return pl.pallas_call(
        _w1_prep_kernel,
        grid=(e, 2),
        in_specs=[pl.BlockSpec((1, d, half), lambda i, j: (i, 0, j)),
                  pl.BlockSpec((W1_GROUP, W1_GROUP), lambda i, j: (0, 0))],
        out_specs=pl.BlockSpec((1, d, half), lambda i, j: (i, 0, j)),
        out_shape=jax.ShapeDtypeStruct((e, d, f2), BF16),
        compiler_params=_cparams(("parallel", "parallel")),
        name="w1prep",
    )(w1, perm)


def _expert_kernel(be_ref, nused_ref, xs_ref, w1_ref, w2_ref, b1_ref, b2_ref, o_ref):
    @pl.when(pl.program_id(0) < nused_ref[0])
    def _():
        x = _tok_load(xs_ref, EXP_BLK).astype(BF16)
        h = jnp.dot(x, w1_ref[0], preferred_element_type=F32) + b1_ref[0]
        groups = range(h.shape[1] // W1_GROUP)
        glu = jnp.concatenate([h[:, c * W1_GROUP: c * W1_GROUP + LANES] for c in groups], axis=1)
        lin = jnp.concatenate([h[:, c * W1_GROUP + LANES: (c + 1) * W1_GROUP] for c in groups], axis=1)
        glu = jnp.minimum(glu, SWIGLU_LIMIT)
        lin = jnp.clip(lin, -SWIGLU_LIMIT, SWIGLU_LIMIT)
        act = glu * jax.nn.sigmoid(SWIGLU_ALPHA * glu) * (lin + 1.0)
        w2 = w2_ref[0].astype(BF16)
        _tok_store(o_ref, jnp.dot(act.astype(BF16), w2, preferred_element_type=F32) + b2_ref[0])


def _experts(block_expert, n_used, xs, w1p, w2, b1p, b2):
    n_slots = xs.shape[0] // TOK_ROWS
    n_blocks = n_slots // EXP_BLK
    slot = lambda i, be, nu: (jnp.minimum(i, nu[0] - 1), 0)
    wsel = lambda i, be, nu: (be[i], 0, 0)
    return pl.pallas_call(
        _expert_kernel,
        grid_spec=pltpu.PrefetchScalarGridSpec(
            num_scalar_prefetch=2,
            grid=(n_blocks,),
            in_specs=[
                pl.BlockSpec((EXP_BLK * TOK_ROWS, LANES), slot),
                pl.BlockSpec((1, D_MODEL, 2 * D_FF), wsel),
                pl.BlockSpec((1, D_FF, D_MODEL), wsel),
                pl.BlockSpec((1, 1, 2 * D_FF), wsel),
                pl.BlockSpec((1, 1, D_MODEL), wsel),
            ],
            out_specs=pl.BlockSpec((EXP_BLK * TOK_ROWS, LANES), slot),
        ),
        out_shape=jax.ShapeDtypeStruct((n_slots * TOK_ROWS, LANES), F32),
        compiler_params=_cparams(("arbitrary",)),
        name="experts",
    )(block_expert, n_used, xs, w1p, w2, b1p, b2)


def _combine_kernel(dest_ref, dest_next_ref, x1_ref, gate_ref, lng_ref, lnb_ref, ys_ref, o_ref, buf_ref, sems):
    tm = x1_ref.shape[0] // TOK_ROWS
    step = pl.program_id(0)
    slot = step % 2

    def gather(idx_ref, into):
        def issue(i, c):
            for k in range(TOP_K):
                src = pl.multiple_of(idx_ref[0, 0, k * tm + i], TOK_ROWS)
                dst = pl.multiple_of(i * TOK_ROWS, TOK_ROWS)
                pltpu.make_async_copy(ys_ref.at[pl.ds(src, TOK_ROWS), :],
                                      buf_ref.at[into, k, pl.ds(dst, TOK_ROWS), :], sems.at[into]
                                      ).start(priority=k % 2)
            return c

        lax.fori_loop(0, tm, issue, 0, unroll=4)

    @pl.when(step == 0)
    def _():
        gather(dest_ref, slot)

    @pl.when(step + 1 < pl.num_programs(0))
    def _():
        gather(dest_next_ref, 1 - slot)

    for k in range(TOP_K):
        pltpu.make_async_copy(ys_ref.at[pl.ds(0, tm * TOK_ROWS), :], buf_ref.at[slot, k], sems.at[slot]).wait()
    gate = gate_ref[...]
    y = _tok_load(buf_ref, tm, (slot, 0)) * gate[:, 0:1]
    for k in range(1, TOP_K):
        y = y + _tok_load(buf_ref, tm, (slot, k)) * gate[:, k:k + 1]
    o_ref[...] = _layer_norm(DN_ALPHA * _tok_load(x1_ref, tm) + y, lng_ref[...], lnb_ref[...])


def _combine(dest_tiles, x1, gate, lng, lnb, ys):
    n = x1.shape[0] // TOK_ROWS
    tm = COMB_TM
    assert n % tm == 0
    n_tiles = n // tm
    return pl.pallas_call(
        _combine_kernel,
        grid=(n_tiles,),
        in_specs=[
            pl.BlockSpec((1, 1, TOP_K * tm), lambda i: (i, 0, 0), memory_space=pltpu.SMEM),
            pl.BlockSpec((1, 1, TOP_K * tm), lambda i: (jnp.minimum(i + 1, n_tiles - 1), 0, 0),
                         memory_space=pltpu.SMEM),
            pl.BlockSpec((tm * TOK_ROWS, LANES), lambda i: (i, 0)),
            pl.BlockSpec((tm, TOP_K), lambda i: (i, 0)),
            pl.BlockSpec((1, D_MODEL), lambda i: (0, 0)),
            pl.BlockSpec((1, D_MODEL), lambda i: (0, 0)),
            pl.BlockSpec(memory_space=pl.ANY),
        ],
        out_specs=pl.BlockSpec((tm, D_MODEL), lambda i: (i, 0)),
        out_shape=jax.ShapeDtypeStruct((n, D_MODEL), F32),
        scratch_shapes=[pltpu.VMEM((2, TOP_K, tm * TOK_ROWS, LANES), F32), pltpu.SemaphoreType.DMA((2,))],
        compiler_params=_cparams(("arbitrary",)),
        name="combine",
    )(dest_tiles, dest_tiles, x1, gate, lng, lnb, ys)


def _rope_tables(t):
    pos = jnp.arange(t)
    row = (pos // GRID_W).astype(F32)
    col = (pos % GRID_W).astype(F32)
    inv = ROPE_THETA ** (-jnp.arange(0, ROPE_AXIS_DIM, 2, dtype=F32) / ROPE_AXIS_DIM)
    ar = row[:, None] * inv[None, :]
    ac = col[:, None] * inv[None, :]
    cos = jnp.concatenate([jnp.cos(ar), jnp.cos(ar), jnp.cos(ac), jnp.cos(ac)], axis=1)
    sin = jnp.concatenate([-jnp.sin(ar), jnp.sin(ar), -jnp.sin(ac), jnp.sin(ac)], axis=1)
    return jnp.tile(cos, (1, 2)), jnp.tile(sin, (1, 2))


def _prep_w_in(w):
    qa = w[:, :NA_WIDTH] * SCORE_SCALE
    kva = w[:, NA_WIDTH:3 * NA_WIDTH]
    qb = _gqa_out_order(w[:, 3 * NA_WIDTH:3 * NA_WIDTH + GQA_WIDTH].T).T
    rest = w[:, 3 * NA_WIDTH + GQA_WIDTH:]
    return jnp.concatenate([qa, kva, qb, rest], axis=1).astype(BF16)


def _gqa_out_order(a):
    rest = a.shape[1:]
    a = a.reshape((GQA_KV_HEADS, GQA_GROUP, HEAD_DIM) + rest)
    return jnp.swapaxes(a, 0, 1).reshape((GQA_WIDTH,) + rest)


def _dest_tiles(dest_t, tm):
    n = dest_t.shape[1]
    return dest_t.reshape(TOP_K, n // tm, tm).transpose(1, 0, 2).reshape(n // tm, 1, TOP_K * tm)


def kernel(x_prompt, x_sample, w_in, rpb, q_norm_g, k_norm_g, g_out_na, g_out_gqa, w_o, ln1_g, ln1_b,
           router_w, router_b, w1, b1, w2, b2, ln2_g, ln2_b):
    assert GQA_KV_HEADS == 2 and KV_WIDTH == LANES
    xs_in = [x_prompt, x_sample]
    l = 0
    w_proj = _prep_w_in(w_in[l])
    gq = jnp.tile(q_norm_g[l] * SCORE_SCALE, 2).reshape(1, LANES)
    gk = jnp.tile(k_norm_g[l], 2).reshape(1, LANES)
    bias = _na_bias_tables(rpb[l])
    woa = w_o[l][:NA_WIDTH].astype(BF16)
    wob = _gqa_out_order(w_o[l][NA_WIDTH:]).astype(BF16)
    ga = g_out_na[l].reshape(1, NA_WIDTH)
    gb = _gqa_out_order(g_out_gqa[l]).reshape(1, GQA_WIDTH)
    ln1g, ln1b = ln1_g[l].reshape(1, D_MODEL), ln1_b[l].reshape(1, D_MODEL)
    ln2g, ln2b = ln2_g[l].reshape(1, D_MODEL), ln2_b[l].reshape(1, D_MODEL)
    rw_t = router_w[l].T.astype(BF16)
    rb = router_b[l].reshape(N_EXPERTS, 1)
    w1p = _w1_prep(w1[l])
    w2b = w2[l]
    b1p = jnp.swapaxes(b1[l].reshape(N_EXPERTS, 2 * D_FF // W1_GROUP, LANES, 2), 2, 3).reshape(N_EXPERTS, 1, 2 * D_FF)
    b2r = b2[l].reshape(N_EXPERTS, 1, D_MODEL)

    x1s, idxs, gates, ranks = [], [], [], []
    cnt = jnp.zeros((N_EXPERTS, LANES), F32)
    for x in xs_in:
        b, t, _ = x.shape
        x2d = x.reshape(b * t, D_MODEL)
        cos, sin = _rope_tables(t)
        qa, ka, va, qb, kb, vlo, vhi = _proj(x2d, t, w_proj, cos, sin, gq, gk)
        sh = lambda a: a.reshape(b, t, a.shape[-1])
        oa = _na(sh(qa), sh(ka), sh(va), bias).reshape(b * t, NA_WIDTH)
        ob = _gqa(sh(qb), sh(kb), sh(vlo), sh(vhi)).reshape(b * t, GQA_WIDTH)
        x1, idx_t, gate_t, rank_t, cnt = _mix(oa, ob, x2d, woa, wob, ga, gb, ln1g, ln1b, rw_t, rb, cnt)
        x1s.append(x1)
        idxs.append(idx_t)
        gates.append(gate_t)
        ranks.append(rank_t)

    counts = cnt[:, 0].astype(jnp.int32)
    nblk = (counts + EXP_BLK - 1) // EXP_BLK
    blk_end = jnp.cumsum(nblk)
    pad_start = (blk_end - nblk) * EXP_BLK
    n_assign = TOP_K * sum(x.shape[0] * x.shape[1] for x in xs_in)
    n_blocks = -(-n_assign // EXP_BLK) + N_EXPERTS
    n_slots = n_blocks * EXP_BLK
    blocks = jnp.arange(n_blocks, dtype=jnp.int32)
    block_expert = jnp.minimum(jnp.sum((blk_end[None, :] <= blocks[:, None]).astype(jnp.int32), axis=1),
                               N_EXPERTS - 1)
    n_used = blk_end[-1:].astype(jnp.int32)
    tail = ((blk_end - 1) * (EXP_BLK * TOK_ROWS)).astype(jnp.int32)
    has_tail = (counts % EXP_BLK != 0).astype(jnp.int32)
    dests = [_slots(pad_start.astype(jnp.int32), idx_t, rank_t) for idx_t, rank_t in zip(idxs, ranks)]

    slots = None
    for x1, dest_t in zip(x1s, dests):
        slots = _dispatch(tail, has_tail, _dest_tiles(dest_t, min(DISP_TM, dest_t.shape[1])), x1, slots, n_slots)
    ys = _experts(block_expert, n_used, slots, w1p, w2b, b1p, b2r)
    outs = []
    for x, x1, dest_t, gate_t in zip(xs_in, x1s, dests, gates):
        y = _combine(_dest_tiles(dest_t, COMB_TM), x1, gate_t.T, ln2g, ln2b, ys)
        outs.append(y.reshape(x.shape))
    return tuple(outs)
```

```python
import functools

import jax
import jax.numpy as jnp
from jax import lax
from jax.experimental import pallas as pl
from jax.experimental.pallas import tpu as pltpu

D_MODEL = 1024
GRID_W = 64
HEAD_DIM = 64
NA_HEADS = 8
GQA_HEADS = 8
GQA_KV_HEADS = 2
GQA_GROUP = GQA_HEADS // GQA_KV_HEADS
NA_WIDTH = NA_HEADS * HEAD_DIM
GQA_WIDTH = GQA_HEADS * HEAD_DIM
KV_WIDTH = GQA_KV_HEADS * HEAD_DIM
NA_WIN_R = 8
NA_WIN_C = 16
ROPE_AXIS_DIM = HEAD_DIM // 2
ROPE_THETA = 10000.0
N_EXPERTS = 32
TOP_K = 4
D_FF = D_MODEL
SWIGLU_ALPHA = 1.702
SWIGLU_LIMIT = 7.0
DEPTH = 1
DN_ALPHA = (2.0 * DEPTH) ** 0.25
NEG_INF = -1e30
LOG2_E = 1.4426950408889634
SCORE_SCALE = HEAD_DIM ** -0.5 * LOG2_E
RMS_EPS = 1e-6
LN_EPS = 1e-5

LANES = 128
PROJ_COLS = 3 * NA_WIDTH + GQA_WIDTH + 2 * KV_WIDTH

PROJ_TM = 512
NA_ROWS = 16
NA_PASS_HEADS = 4
GQA_SCORE_ELEMS = 1024 * 2048
MIX_TM = 1024
DISP_TM = 2048
COMB_TM = 256
EXP_BLK = 1024
W1_GROUP = 2 * LANES
V7X_VMEM_BYTES = 64 * 1024 * 1024
VMEM_LIMIT = V7X_VMEM_BYTES - 8 * 1024 * 1024

F32 = jnp.float32
BF16 = jnp.bfloat16

TOK_ROWS = D_MODEL // LANES


def _tok_load(ref, n, lead=()):
    return jnp.concatenate([ref[lead + (pl.ds(s, n, stride=TOK_ROWS), slice(None))] for s in range(TOK_ROWS)],
                           axis=1)


def _tok_store(ref, val):
    n = val.shape[0]
    for s in range(TOK_ROWS):
        ref[pl.ds(s, n, stride=TOK_ROWS), :] = val[:, s * LANES:(s + 1) * LANES]


def _cparams(sem):
    return pltpu.CompilerParams(dimension_semantics=sem, vmem_limit_bytes=VMEM_LIMIT)


def _rope(y, cos, sin, first_half):
    half = ROPE_AXIS_DIM // 2
    partner = jnp.where(first_half, pltpu.roll(y, LANES - half, 1), pltpu.roll(y, half, 1))
    return y * cos + partner * sin


def _proj_kernel(x_ref, w_ref, cos_ref, sin_ref, gq_ref, gk_ref,
                 qa_ref, ka_ref, va_ref, qb_ref, kb_ref, vlo_ref, vhi_ref):
    x = x_ref[...].astype(BF16)

    def cols(c0, width):
        return jnp.dot(x, w_ref[:, c0:c0 + width], preferred_element_type=F32)

    cos = cos_ref[...]
    sin = sin_ref[...]
    lane = lax.broadcasted_iota(jnp.int32, cos.shape, 1)
    first_half = (lane % ROPE_AXIS_DIM) < ROPE_AXIS_DIM // 2
    lo = lane < HEAD_DIM

    def norm_rope(y, g):
        y2 = y * y
        ms_lo = jnp.sum(jnp.where(lo, y2, 0.0), axis=-1, keepdims=True)
        ms_hi = jnp.sum(jnp.where(lo, 0.0, y2), axis=-1, keepdims=True)
        ms = jnp.where(lo, ms_lo, ms_hi) * (1.0 / HEAD_DIM)
        return _rope(y * lax.rsqrt(ms + RMS_EPS) * g, cos, sin, first_half).astype(BF16)

    base = 3 * NA_WIDTH
    q = cols(base, GQA_WIDTH)
    gq = gq_ref[...]
    for g in range(GQA_GROUP):
        qb_ref[:, g * LANES:(g + 1) * LANES] = norm_rope(q[:, g * LANES:(g + 1) * LANES], gq)
    base += GQA_WIDTH
    kv = cols(base, 2 * KV_WIDTH)
    kb_ref[...] = norm_rope(kv[:, :KV_WIDTH], gk_ref[...])
    v = kv[:, KV_WIDTH:]
    vlo_ref[...] = jnp.where(lo, v, 1.0).astype(BF16)
    vhi_ref[...] = jnp.where(lo, 1.0, v).astype(BF16)
    qa_ref[...] = cols(0, NA_WIDTH).astype(BF16)
    ka_ref[...] = cols(NA_WIDTH, NA_WIDTH).astype(BF16)
    va_ref[...] = cols(2 * NA_WIDTH, NA_WIDTH).astype(BF16)


def _proj(x2d, seq_len, w, cos, sin, gq, gk):
    n = x2d.shape[0]
    tm = min(PROJ_TM, seq_len)
    assert n % tm == 0 and seq_len % tm == 0
    pos_blocks = seq_len // tm
    row = lambda i: (i, 0)
    const = lambda i: (0, 0)
    outs = [(NA_WIDTH, BF16)] * 3 + [(GQA_WIDTH, BF16)] + [(KV_WIDTH, BF16)] * 3
    return pl.pallas_call(
        _proj_kernel,
        grid=(n // tm,),
        in_specs=[
            pl.BlockSpec((tm, D_MODEL), row),
            pl.BlockSpec((D_MODEL, PROJ_COLS), const),
            pl.BlockSpec((tm, LANES), lambda i: (i % pos_blocks, 0)),
            pl.BlockSpec((tm, LANES), lambda i: (i % pos_blocks, 0)),
            pl.BlockSpec((1, LANES), const),
            pl.BlockSpec((1, LANES), const),
        ],
        out_specs=[pl.BlockSpec((tm, c), row) for c, _ in outs],
        out_shape=[jax.ShapeDtypeStruct((n, c), dt) for c, dt in outs],
        compiler_params=_cparams(("parallel",)),
        name="proj",
    )(x2d, w, cos, sin, gq, gk)


def _na_kernel(q_ref, k_ref, v_ref, bias_ref, o_ref, *, rows):
    j = pl.program_id(1)
    step_rows = q_ref.shape[1] // GRID_W
    width = NA_PASS_HEADS * HEAD_DIM
    lane_head = lax.broadcasted_iota(jnp.int32, (GRID_W, width), 1) // HEAD_DIM

    def one_row(rr, carry):
        r = j * step_rows + rr
        rs = jnp.clip(r - NA_WIN_R // 2, 0, rows - NA_WIN_R)
        var = r - rs
        k0 = pl.multiple_of(rs * GRID_W, GRID_W)
        q0 = pl.multiple_of(rr * GRID_W, GRID_W)
        for g in range(NA_HEADS // NA_PASS_HEADS):
            cols = slice(g * width, (g + 1) * width)
            heads = range(g * NA_PASS_HEADS, (g + 1) * NA_PASS_HEADS)
            q = q_ref[0, pl.ds(q0, GRID_W), cols]
            zero = jnp.zeros_like(q)
            qs = jnp.concatenate([jnp.where(lane_head == a, q, zero) for a in range(NA_PASS_HEADS)], axis=0)
            k = k_ref[0, pl.ds(k0, NA_WIN_R * GRID_W), cols]
            v = v_ref[0, pl.ds(k0, NA_WIN_R * GRID_W), cols]
            s = lax.dot_general(qs, k, (((1,), (1,)), ((), ())), preferred_element_type=F32)
            s = s + jnp.concatenate([bias_ref[var, h] for h in heads], axis=0)
            m = jnp.max(s, axis=-1, keepdims=True)
            p = jnp.exp2(s - m)
            l = jnp.sum(p, axis=-1, keepdims=True)
            os = jnp.dot(p.astype(BF16), v, preferred_element_type=F32) / l
            o = os[:GRID_W]
            for a in range(1, NA_PASS_HEADS):
                o = jnp.where(lane_head == a, os[a * GRID_W:(a + 1) * GRID_W], o)
            o_ref[0, pl.ds(q0, GRID_W), cols] = o.astype(BF16)
        return carry

    lax.fori_loop(0, step_rows, one_row, 0, unroll=True)


def _na(qa, ka, va, bias):
    b, t, _ = qa.shape
    rows = t // GRID_W
    step_rows = min(NA_ROWS, rows)
    assert rows >= NA_WIN_R and rows % step_rows == 0
    tq = step_rows * GRID_W
    return pl.pallas_call(
        functools.partial(_na_kernel, rows=rows),
        grid=(b, rows // step_rows),
        in_specs=[
            pl.BlockSpec((1, tq, NA_WIDTH), lambda i, j: (i, j, 0)),
            pl.BlockSpec((1, t, NA_WIDTH), lambda i, j: (i, 0, 0)),
            pl.BlockSpec((1, t, NA_WIDTH), lambda i, j: (i, 0, 0)),
            pl.BlockSpec(bias.shape, lambda i, j: (0, 0, 0, 0), pipeline_mode=pl.Buffered(1)),
        ],
        out_specs=pl.BlockSpec((1, tq, NA_WIDTH), lambda i, j: (i, j, 0)),
        out_shape=jax.ShapeDtypeStruct((b, t, NA_WIDTH), BF16),
        compiler_params=_cparams(("parallel", "parallel")),
        name="na",
    )(qa, ka, va, bias)


def _na_bias_tables(rpb):
    off = jnp.arange(NA_WIN_R)
    jrow = jnp.arange(NA_WIN_R)
    dr = jrow[None, :] - off[:, None] + (NA_WIN_R - 1)
    c = jnp.arange(GRID_W)
    cs = jnp.clip(c - NA_WIN_C // 2, 0, GRID_W - NA_WIN_C)
    col_ok = (c[None, :] >= cs[:, None]) & (c[None, :] < cs[:, None] + NA_WIN_C)
    dc = jnp.clip(c[None, :] - c[:, None], -(NA_WIN_C - 1), NA_WIN_C - 1) + (NA_WIN_C - 1)
    sel_r = (dr[:, :, None] == jnp.arange(2 * NA_WIN_R - 1)[None, None, :]).astype(F32)
    sel_c = (dc[:, :, None] == jnp.arange(2 * NA_WIN_C - 1)[None, None, :]).astype(F32)
    bias = jnp.einsum('vja,hab,qkb->hvqjk', sel_r, rpb.astype(F32), sel_c, precision=lax.Precision.HIGHEST)
    bias = jnp.where(col_ok[None, None, :, None, :], bias * LOG2_E, NEG_INF)
    return jnp.moveaxis(bias, 0, 1).reshape(NA_WIN_R, NA_HEADS, GRID_W, NA_WIN_R * GRID_W)


def _gqa_kernel(q_ref, k_ref, vlo_ref, vhi_ref, o_ref):
    k = k_ref[0]
    lane = lax.broadcasted_iota(jnp.int32, (q_ref.shape[1], LANES), 1)
    lo = lane < HEAD_DIM

    def scores(h):
        g, hi = divmod(h, 2)
        q = q_ref[0, :, g * LANES:(g + 1) * LANES]
        zero = jnp.zeros_like(q)
        q = jnp.where(lo, zero, q) if hi else jnp.where(lo, q, zero)
        return lax.dot_general(q, k, (((1,), (1,)), ((), ())), preferred_element_type=F32)

    def attend(s, v):
        p = jnp.exp2(s - jnp.max(s, axis=-1, keepdims=True)).astype(BF16)
        ov = jnp.dot(p, v, preferred_element_type=F32)
        return ov / pltpu.roll(ov, HEAD_DIM, 1)

    n_heads = 2 * GQA_GROUP
    outs = []
    s_next = scores(0)
    for h in range(n_heads):
        s = s_next
        if h + 1 < n_heads:
            s_next = scores(h + 1)
        outs.append(attend(s, vhi_ref[0] if h % 2 else vlo_ref[0]))
        if h % 2:
            g = h // 2
            o_ref[0, :, g * LANES:(g + 1) * LANES] = jnp.where(lo, outs[h - 1], outs[h]).astype(BF16)


def _gqa(qb, kb, vlo, vhi):
    b, t, _ = qb.shape
    tq = min(t, GQA_SCORE_ELEMS // t)
    assert t % tq == 0 and tq % 8 == 0
    return pl.pallas_call(
        _gqa_kernel,
        grid=(b, t // tq),
        in_specs=[
            pl.BlockSpec((1, tq, GQA_WIDTH), lambda i, j: (i, j, 0)),
            pl.BlockSpec((1, t, KV_WIDTH), lambda i, j: (i, 0, 0)),
            pl.BlockSpec((1, t, KV_WIDTH), lambda i, j: (i, 0, 0)),
            pl.BlockSpec((1, t, KV_WIDTH), lambda i, j: (i, 0, 0)),
        ],
        out_specs=pl.BlockSpec((1, tq, GQA_WIDTH), lambda i, j: (i, j, 0)),
        out_shape=jax.ShapeDtypeStruct((b, t, GQA_WIDTH), BF16),
        compiler_params=_cparams(("parallel", "parallel")),
        name="gqa",
    )(qb, kb, vlo, vhi)


def _layer_norm(z, g, b):
    mu = jnp.mean(z, axis=-1, keepdims=True)
    zc = z - mu
    var = jnp.mean(zc * zc, axis=-1, keepdims=True)
    return zc * lax.rsqrt(var + LN_EPS) * g + b


def _rms(o, g):
    return o * lax.rsqrt(jnp.mean(o * o, axis=-1, keepdims=True) + RMS_EPS) * g


def _mix_kernel(oa_ref, ob_ref, x_ref, woa_ref, wob_ref, ga_ref, gb_ref, lng_ref, lnb_ref,
                rw_ref, rb_ref, tri_ref, cnt_in_ref,
                x1_ref, idx_ref, gate_ref, rank_ref, cnt_ref, carry_ref):
    @pl.when(pl.program_id(0) == 0)
    def _():
        carry_ref[...] = cnt_in_ref[...]

    na = _rms(oa_ref[...].astype(F32), ga_ref[...]).astype(BF16)
    nb = _rms(ob_ref[...].astype(F32), gb_ref[...]).astype(BF16)
    mixed = (jnp.dot(na, woa_ref[...], preferred_element_type=F32)
             + jnp.dot(nb, wob_ref[...], preferred_element_type=F32))
    x1 = _layer_norm(DN_ALPHA * x_ref[...] + mixed, lng_ref[...], lnb_ref[...])
    _tok_store(x1_ref, x1)

    logits = lax.dot_general(rw_ref[...], x1.astype(BF16), (((1,), (1,)), ((), ())),
                             preferred_element_type=F32) + rb_ref[...]
    tm = logits.shape[1]
    eidx = lax.broadcasted_iota(jnp.int32, (N_EXPERTS, tm), 0).astype(F32)
    work = logits
    vals, idxs, hots = [], [], []
    for _ in range(TOP_K):
        m = jnp.max(work, axis=0, keepdims=True)
        sel = jnp.min(jnp.where(work == m, eidx, float(N_EXPERTS)), axis=0, keepdims=True)
        hot = eidx == sel
        vals.append(m)
        idxs.append(sel)
        hots.append(hot)
        work = jnp.where(hot, -jnp.inf, work)
    es = [jnp.exp(v - vals[0]) for v in vals]
    den = es[0] + es[1] + es[2] + es[3]
    gate_ref[...] = jnp.concatenate([e / den for e in es], axis=0)
    idx_ref[...] = jnp.concatenate(idxs, axis=0).astype(jnp.int32)

    hot_all = hots[0] | hots[1] | hots[2] | hots[3]
    onehot = jnp.where(hot_all, 1.0, 0.0)
    before = jnp.dot(onehot.astype(BF16), tri_ref[...], preferred_element_type=F32)
    before = before + carry_ref[:, 0:1]
    ranks = [jnp.sum(jnp.where(hot, before, 0.0), axis=0, keepdims=True) for hot in hots]
    rank_ref[...] = jnp.concatenate(ranks, axis=0).astype(jnp.int32)
    carry_ref[...] = carry_ref[...] + jnp.sum(onehot, axis=1, keepdims=True)
    cnt_ref[...] = carry_ref[...]


def _mix(oa, ob, x2d, woa, wob, ga, gb, lng, lnb, rw_t, rb, cnt_in):
    n = x2d.shape[0]
    tm = min(MIX_TM, n)
    assert n % tm == 0
    tri = (jnp.arange(tm)[:, None] < jnp.arange(tm)[None, :]).astype(BF16)
    row = lambda i: (i, 0)
    col = lambda i: (0, i)
    const = lambda i: (0, 0)
    full = lambda a: pl.BlockSpec(a.shape, const)
    return pl.pallas_call(
        _mix_kernel,
        grid=(n // tm,),
        in_specs=[
            pl.BlockSpec((tm, NA_WIDTH), row),
            pl.BlockSpec((tm, GQA_WIDTH), row),
            pl.BlockSpec((tm, D_MODEL), row),
            full(woa), full(wob), full(ga), full(gb), full(lng), full(lnb),
            full(rw_t), full(rb), full(tri), full(cnt_in),
        ],
        out_specs=[
            pl.BlockSpec((tm * TOK_ROWS, LANES), row),
            pl.BlockSpec((TOP_K, tm), col),
            pl.BlockSpec((TOP_K, tm), col),
            pl.BlockSpec((TOP_K, tm), col),
            pl.BlockSpec((N_EXPERTS, LANES), const),
        ],
        out_shape=[
            jax.ShapeDtypeStruct((n * TOK_ROWS, LANES), F32),
            jax.ShapeDtypeStruct((TOP_K, n), jnp.int32),
            jax.ShapeDtypeStruct((TOP_K, n), F32),
            jax.ShapeDtypeStruct((TOP_K, n), jnp.int32),
            jax.ShapeDtypeStruct((N_EXPERTS, LANES), F32),
        ],
        scratch_shapes=[pltpu.VMEM((N_EXPERTS, LANES), F32)],
        compiler_params=_cparams(("arbitrary",)),
        name="mix",
    )(oa, ob, x2d, woa, wob, ga, gb, lng, lnb, rw_t, rb, tri, cnt_in)


def _slot_kernel(pad_ref, idx_ref, rank_ref, disp_ref, comb_ref):
    idx = idx_ref[...]
    start = jnp.zeros_like(idx)
    for e in range(N_EXPERTS):
        start = jnp.where(idx == e, pad_ref[e], start)
    dest = (start + rank_ref[...]) * TOK_ROWS
    tm = dest.shape[1]
    for k in range(TOP_K):
        disp_ref[0, :, k * tm:(k + 1) * tm] = dest[k:k + 1, :]
        for j in range(tm // COMB_TM):
            comb_ref[j, :, k * COMB_TM:(k + 1) * COMB_TM] = dest[k:k + 1, j * COMB_TM:(j + 1) * COMB_TM]


def _slots(pad_start, idx_t, rank_t):
    n = idx_t.shape[1]
    tm = min(DISP_TM, n)
    assert n % tm == 0 and tm % COMB_TM == 0
    col = lambda i, *_: (0, i)
    sub = tm // COMB_TM
    return pl.pallas_call(
        _slot_kernel,
        grid_spec=pltpu.PrefetchScalarGridSpec(
            num_scalar_prefetch=1,
            grid=(n // tm,),
            in_specs=[pl.BlockSpec((TOP_K, tm), col), pl.BlockSpec((TOP_K, tm), col)],
            out_specs=[pl.BlockSpec((1, 1, TOP_K * tm), lambda i, *_: (i, 0, 0)),
                       pl.BlockSpec((sub, 1, TOP_K * COMB_TM), lambda i, *_: (i, 0, 0))],
        ),
        out_shape=[jax.ShapeDtypeStruct((n // tm, 1, TOP_K * tm), jnp.int32),
                   jax.ShapeDtypeStruct((n // COMB_TM, 1, TOP_K * COMB_TM), jnp.int32)],
        compiler_params=_cparams(("parallel",)),
        name="slots",
    )(pad_start, idx_t, rank_t)


def _dispatch_kernel(tail_ref, has_tail_ref, dest_ref, x_ref, *rest, zero_tails):
    if zero_tails:
        xs_ref, zeros_ref, sem = rest
    else:
        _, xs_ref, sem = rest
    tm = x_ref.shape[0] // TOK_ROWS

    if zero_tails:
        @pl.when(pl.program_id(0) == 0)
        def _():
            zeros_ref[...] = jnp.zeros_like(zeros_ref)

            def tail_copy(e):
                start = pl.multiple_of(tail_ref[e], EXP_BLK * TOK_ROWS)
                return pltpu.make_async_copy(zeros_ref, xs_ref.at[pl.ds(start, EXP_BLK * TOK_ROWS), :], sem)

            for e in range(N_EXPERTS):
                @pl.when(has_tail_ref[e] != 0)
                def _():
                    tail_copy(e).start()
            for e in range(N_EXPERTS):
                @pl.when(has_tail_ref[e] != 0)
                def _():
                    tail_copy(e).wait()

    def row_copy(i, k):
        src = pl.multiple_of(i * TOK_ROWS, TOK_ROWS)
        dst = pl.multiple_of(dest_ref[0, 0, k * tm + i], TOK_ROWS)
        return pltpu.make_async_copy(x_ref.at[pl.ds(src, TOK_ROWS), :], xs_ref.at[pl.ds(dst, TOK_ROWS), :], sem)

    def issue(i, c):
        for k in range(TOP_K):
            row_copy(i, k).start(priority=k % 2)
        return c

    lax.fori_loop(0, tm, issue, 0, unroll=4)
    for k in range(TOP_K):
        pltpu.make_async_copy(x_ref, xs_ref.at[pl.ds(0, tm * TOK_ROWS), :], sem).wait()


def _dispatch(tail, has_tail, dest_tiles, x1, xs_prev, n_slots):
    n = x1.shape[0] // TOK_ROWS
    tm = dest_tiles.shape[2] // TOP_K
    assert n % tm == 0
    zero_tails = xs_prev is None
    in_specs = [
        pl.BlockSpec((1, 1, TOP_K * tm), lambda i, *_: (i, 0, 0), memory_space=pltpu.SMEM),
        pl.BlockSpec((tm * TOK_ROWS, LANES), lambda i, *_: (i, 0)),
    ]
    args = [dest_tiles, x1]
    scratch = []
    aliases = {}
    if zero_tails:
        scratch.append(pltpu.VMEM((EXP_BLK * TOK_ROWS, LANES), F32))
    else:
        in_specs.append(pl.BlockSpec(memory_space=pl.ANY))
        args.append(xs_prev)
        aliases = {4: 0}
    scratch.append(pltpu.SemaphoreType.DMA(()))
    return pl.pallas_call(
        functools.partial(_dispatch_kernel, zero_tails=zero_tails),
        grid_spec=pltpu.PrefetchScalarGridSpec(
            num_scalar_prefetch=2,
            grid=(n // tm,),
            in_specs=in_specs,
            out_specs=pl.BlockSpec(memory_space=pl.ANY),
            scratch_shapes=scratch,
        ),
        out_shape=jax.ShapeDtypeStruct((n_slots * TOK_ROWS, LANES), F32),
        input_output_aliases=aliases,
        compiler_params=pltpu.CompilerParams(dimension_semantics=("arbitrary",), vmem_limit_bytes=VMEM_LIMIT,
                                             has_side_effects=True),
        name="dispatch",
    )(tail, has_tail, *args)


def _w1_prep_kernel(w_ref, perm_ref, o_ref):
    for c in range(w_ref.shape[2] // W1_GROUP):
        cols = slice(c * W1_GROUP, (c + 1) * W1_GROUP)
        w = w_ref[0, :, cols].astype(BF16)
        o_ref[0, :, cols] = jnp.dot(w, perm_ref[...], preferred_element_type=F32).astype(BF16)


def _w1_prep(w1):
    e, d, f2 = w1.shape
    half = f2 // 2
    j = jnp.arange(W1_GROUP)
    dst = jnp.where(j % 2 == 0, j // 2, LANES + j // 2)
    perm = (dst[:, None] == jnp.arange(W1_GROUP)[None, :]).astype(BF16)
    return pl.pallas_call(
        _w1_prep_kernel,
        grid=(e, 2),
        in_specs=[pl.BlockSpec((1, d, half), lambda i, j: (i, 0, j)),
                  pl.BlockSpec((W1_GROUP, W1_GROUP), lambda i, j: (0, 0))],
        out_specs=pl.BlockSpec((1, d, half), lambda i, j: (i, 0, j)),
        out_shape=jax.ShapeDtypeStruct((e, d, f2), BF16),
        compiler_params=_cparams(("parallel", "parallel")),
        name="w1prep",
    )(w1, perm)


def _expert_kernel(be_ref, nused_ref, xs_ref, w1_ref, w2_ref, b1_ref, b2_ref, o_ref):
    @pl.when(pl.program_id(0) < nused_ref[0])
    def _():
        x = _tok_load(xs_ref, EXP_BLK).astype(BF16)
        h = jnp.dot(x, w1_ref[0], preferred_element_type=F32) + b1_ref[0]
        groups = range(h.shape[1] // W1_GROUP)
        glu = jnp.concatenate([h[:, c * W1_GROUP: c * W1_GROUP + LANES] for c in groups], axis=1)
        lin = jnp.concatenate([h[:, c * W1_GROUP + LANES: (c + 1) * W1_GROUP] for c in groups], axis=1)
        glu = jnp.minimum(glu, SWIGLU_LIMIT)
        lin = jnp.clip(lin, -SWIGLU_LIMIT, SWIGLU_LIMIT)
        act = glu * jax.nn.sigmoid(SWIGLU_ALPHA * glu) * (lin + 1.0)
        w2 = w2_ref[0].astype(BF16)
        _tok_store(o_ref, jnp.dot(act.astype(BF16), w2, preferred_element_type=F32) + b2_ref[0])


def _experts(block_expert, n_used, xs, w1p, w2, b1p, b2):
    n_slots = xs.shape[0] // TOK_ROWS
    n_blocks = n_slots // EXP_BLK
    slot = lambda i, be, nu: (jnp.minimum(i, nu[0] - 1), 0)
    wsel = lambda i, be, nu: (be[i], 0, 0)
    return pl.pallas_call(
        _expert_kernel,
        grid_spec=pltpu.PrefetchScalarGridSpec(
            num_scalar_prefetch=2,
            grid=(n_blocks,),
            in_specs=[
                pl.BlockSpec((EXP_BLK * TOK_ROWS, LANES), slot),
                pl.BlockSpec((1, D_MODEL, 2 * D_FF), wsel),
                pl.BlockSpec((1, D_FF, D_MODEL), wsel),
                pl.BlockSpec((1, 1, 2 * D_FF), wsel),
                pl.BlockSpec((1, 1, D_MODEL), wsel),
            ],
            out_specs=pl.BlockSpec((EXP_BLK * TOK_ROWS, LANES), slot),
        ),
        out_shape=jax.ShapeDtypeStruct((n_slots * TOK_ROWS, LANES), F32),
        compiler_params=_cparams(("arbitrary",)),
        name="experts",
    )(block_expert, n_used, xs, w1p, w2, b1p, b2)


def _combine_kernel(dest_ref, dest_next_ref, x1_ref, gate_ref, lng_ref, lnb_ref, ys_ref, o_ref, buf_ref, sems):
    tm = x1_ref.shape[0] // TOK_ROWS
    step = pl.program_id(0)
    slot = step % 2

    def gather(idx_ref, into):
        def issue(i, c):
            for k in range(TOP_K):
                src = pl.multiple_of(idx_ref[0, 0, k * tm + i], TOK_ROWS)
                dst = pl.multiple_of(i * TOK_ROWS, TOK_ROWS)
                pltpu.make_async_copy(ys_ref.at[pl.ds(src, TOK_ROWS), :],
                                      buf_ref.at[into, k, pl.ds(dst, TOK_ROWS), :], sems.at[into]
                                      ).start(priority=k % 2)
            return c

        lax.fori_loop(0, tm, issue, 0, unroll=4)

    @pl.when(step == 0)
    def _():
        gather(dest_ref, slot)

    @pl.when(step + 1 < pl.num_programs(0))
    def _():
        gather(dest_next_ref, 1 - slot)

    for k in range(TOP_K):
        pltpu.make_async_copy(ys_ref.at[pl.ds(0, tm * TOK_ROWS), :], buf_ref.at[slot, k], sems.at[slot]).wait()
    gate = gate_ref[...]
    y = _tok_load(buf_ref, tm, (slot, 0)) * gate[:, 0:1]
    for k in range(1, TOP_K):
        y = y + _tok_load(buf_ref, tm, (slot, k)) * gate[:, k:k + 1]
    o_ref[...] = _layer_norm(DN_ALPHA * _tok_load(x1_ref, tm) + y, lng_ref[...], lnb_ref[...])


def _combine(dest_tiles, x1, gate, lng, lnb, ys):
    n = x1.shape[0] // TOK_ROWS
    tm = COMB_TM
    assert n % tm == 0
    n_tiles = n // tm
    return pl.pallas_call(
        _combine_kernel,
        grid=(n_tiles,),
        in_specs=[
            pl.BlockSpec((1, 1, TOP_K * tm), lambda i: (i, 0, 0), memory_space=pltpu.SMEM),
            pl.BlockSpec((1, 1, TOP_K * tm), lambda i: (jnp.minimum(i + 1, n_tiles - 1), 0, 0),
                         memory_space=pltpu.SMEM),
            pl.BlockSpec((tm * TOK_ROWS, LANES), lambda i: (i, 0)),
            pl.BlockSpec((tm, TOP_K), lambda i: (i, 0)),
            pl.BlockSpec((1, D_MODEL), lambda i: (0, 0)),
            pl.BlockSpec((1, D_MODEL), lambda i: (0, 0)),
            pl.BlockSpec(memory_space=pl.ANY),
        ],
        out_specs=pl.BlockSpec((tm, D_MODEL), lambda i: (i, 0)),
        out_shape=jax.ShapeDtypeStruct((n, D_MODEL), F32),
        scratch_shapes=[pltpu.VMEM((2, TOP_K, tm * TOK_ROWS, LANES), F32), pltpu.SemaphoreType.DMA((2,))],
        compiler_params=_cparams(("arbitrary",)),
        name="combine",
    )(dest_tiles, dest_tiles, x1, gate, lng, lnb, ys)


def _rope_tables(t):
    pos = jnp.arange(t)
    row = (pos // GRID_W).astype(F32)
    col = (pos % GRID_W).astype(F32)
    inv = ROPE_THETA ** (-jnp.arange(0, ROPE_AXIS_DIM, 2, dtype=F32) / ROPE_AXIS_DIM)
    ar = row[:, None] * inv[None, :]
    ac = col[:, None] * inv[None, :]
    cos = jnp.concatenate([jnp.cos(ar), jnp.cos(ar), jnp.cos(ac), jnp.cos(ac)], axis=1)
    sin = jnp.concatenate([-jnp.sin(ar), jnp.sin(ar), -jnp.sin(ac), jnp.sin(ac)], axis=1)
    return jnp.tile(cos, (1, 2)), jnp.tile(sin, (1, 2))


def _prep_w_in(w):
    qa = w[:, :NA_WIDTH] * SCORE_SCALE
    kva = w[:, NA_WIDTH:3 * NA_WIDTH]
    qb = _gqa_out_order(w[:, 3 * NA_WIDTH:3 * NA_WIDTH + GQA_WIDTH].T).T
    rest = w[:, 3 * NA_WIDTH + GQA_WIDTH:]
    return jnp.concatenate([qa, kva, qb, rest], axis=1).astype(BF16)


def _gqa_out_order(a):
    rest = a.shape[1:]
    a = a.reshape((GQA_KV_HEADS, GQA_GROUP, HEAD_DIM) + rest)
    return jnp.swapaxes(a, 0, 1).reshape((GQA_WIDTH,) + rest)


def kernel(x_prompt, x_sample, w_in, rpb, q_norm_g, k_norm_g, g_out_na, g_out_gqa, w_o, ln1_g, ln1_b,
           router_w, router_b, w1, b1, w2, b2, ln2_g, ln2_b):
    assert GQA_KV_HEADS == 2 and KV_WIDTH == LANES
    xs_in = [x_prompt, x_sample]
    l = 0
    w_proj = _prep_w_in(w_in[l])
    gq = jnp.tile(q_norm_g[l] * SCORE_SCALE, 2).reshape(1, LANES)
    gk = jnp.tile(k_norm_g[l], 2).reshape(1, LANES)
    bias = _na_bias_tables(rpb[l])
    woa = w_o[l][:NA_WIDTH].astype(BF16)
    wob = _gqa_out_order(w_o[l][NA_WIDTH:]).astype(BF16)
    ga = g_out_na[l].reshape(1, NA_WIDTH)
    gb = _gqa_out_order(g_out_gqa[l]).reshape(1, GQA_WIDTH)
    ln1g, ln1b = ln1_g[l].reshape(1, D_MODEL), ln1_b[l].reshape(1, D_MODEL)
    ln2g, ln2b = ln2_g[l].reshape(1, D_MODEL), ln2_b[l].reshape(1, D_MODEL)
    rw_t = router_w[l].T.astype(BF16)
    rb = router_b[l].reshape(N_EXPERTS, 1)
    w1p = _w1_prep(w1[l])
    w2b = w2[l]
    b1p = jnp.swapaxes(b1[l].reshape(N_EXPERTS, 2 * D_FF // W1_GROUP, LANES, 2), 2, 3).reshape(N_EXPERTS, 1, 2 * D_FF)
    b2r = b2[l].reshape(N_EXPERTS, 1, D_MODEL)

    x1s, idxs, gates, ranks = [], [], [], []
    cnt = jnp.zeros((N_EXPERTS, LANES), F32)
    for x in xs_in:
        b, t, _ = x.shape
        x2d = x.reshape(b * t, D_MODEL)
        cos, sin = _rope_tables(t)
        qa, ka, va, qb, kb, vlo, vhi = _proj(x2d, t, w_proj, cos, sin, gq, gk)
        sh = lambda a: a.reshape(b, t, a.shape[-1])
        oa = _na(sh(qa), sh(ka), sh(va), bias).reshape(b * t, NA_WIDTH)
        ob = _gqa(sh(qb), sh(kb), sh(vlo), sh(vhi)).reshape(b * t, GQA_WIDTH)
        x1, idx_t, gate_t, rank_t, cnt = _mix(oa, ob, x2d, woa, wob, ga, gb, ln1g, ln1b, rw_t, rb, cnt)
        x1s.append(x1)
        idxs.append(idx_t)
        gates.append(gate_t)
        ranks.append(rank_t)

    counts = cnt[:, 0].astype(jnp.int32)
    nblk = (counts + EXP_BLK - 1) // EXP_BLK
    blk_end = jnp.cumsum(nblk)
    pad_start = (blk_end - nblk) * EXP_BLK
    n_assign = TOP_K * sum(x.shape[0] * x.shape[1] for x in xs_in)
    n_blocks = -(-n_assign // EXP_BLK) + N_EXPERTS
    n_slots = n_blocks * EXP_BLK
    blocks = jnp.arange(n_blocks, dtype=jnp.int32)
    block_expert = jnp.minimum(jnp.sum((blk_end[None, :] <= blocks[:, None]).astype(jnp.int32), axis=1),
                               N_EXPERTS - 1)
    n_used = blk_end[-1:].astype(jnp.int32)
    tail = ((blk_end - 1) * (EXP_BLK * TOK_ROWS)).astype(jnp.int32)
    has_tail = (counts % EXP_BLK != 0).astype(jnp.int32)
    dests = [_slots(pad_start.astype(jnp.int32), idx_t, rank_t) for idx_t, rank_t in zip(idxs, ranks)]

    slots = None
    for x1, (dest_disp, _) in zip(x1s, dests):
        slots = _dispatch(tail, has_tail, dest_disp, x1, slots, n_slots)
    ys = _experts(block_expert, n_used, slots, w1p, w2b, b1p, b2r)
    outs = []
    for x, x1, (_, dest_comb), gate_t in zip(xs_in, x1s, dests, gates):
        y = _combine(dest_comb, x1, gate_t.T, ln2g, ln2b, ys)
        outs.append(y.reshape(x.shape))
    return tuple(outs)
```

```python
import functools

import jax
import jax.numpy as jnp
from jax import lax
from jax.experimental import pallas as pl
from jax.experimental.pallas import tpu as pltpu

D_MODEL = 1024
GRID_W = 64
HEAD_DIM = 64
NA_HEADS = 8
GQA_HEADS = 8
GQA_KV_HEADS = 2
GQA_GROUP = GQA_HEADS // GQA_KV_HEADS
NA_WIDTH = NA_HEADS * HEAD_DIM
GQA_WIDTH = GQA_HEADS * HEAD_DIM
KV_WIDTH = GQA_KV_HEADS * HEAD_DIM
NA_WIN_R = 8
NA_WIN_C = 16
ROPE_AXIS_DIM = HEAD_DIM // 2
ROPE_THETA = 10000.0
N_EXPERTS = 32
TOP_K = 4
D_FF = D_MODEL
SWIGLU_ALPHA = 1.702
SWIGLU_LIMIT = 7.0
DEPTH = 1
DN_ALPHA = (2.0 * DEPTH) ** 0.25
NEG_INF = -1e30
LOG2_E = 1.4426950408889634
SCORE_SCALE = HEAD_DIM ** -0.5 * LOG2_E
RMS_EPS = 1e-6
LN_EPS = 1e-5

LANES = 128
PROJ_COLS = 3 * NA_WIDTH + GQA_WIDTH + 2 * KV_WIDTH

PROJ_TM = 512
NA_ROWS = 16
NA_PASS_HEADS = 4
GQA_SCORE_ELEMS = 1024 * 2048
MIX_TM = 1024
DISP_TM = 2048
COMB_TM = 256
EXP_BLK = 1024
W1_GROUP = 2 * LANES
V7X_VMEM_BYTES = 64 * 1024 * 1024
VMEM_LIMIT = V7X_VMEM_BYTES - 8 * 1024 * 1024

F32 = jnp.float32
BF16 = jnp.bfloat16

TOK_ROWS = D_MODEL // LANES


def _tok_load(ref, n, lead=()):
    return jnp.concatenate([ref[lead + (pl.ds(s, n, stride=TOK_ROWS), slice(None))] for s in range(TOK_ROWS)],
                           axis=1)


def _tok_store(ref, val):
    n = val.shape[0]
    for s in range(TOK_ROWS):
        ref[pl.ds(s, n, stride=TOK_ROWS), :] = val[:, s * LANES:(s + 1) * LANES]


def _cparams(sem):
    return pltpu.CompilerParams(dimension_semantics=sem, vmem_limit_bytes=VMEM_LIMIT)


def _rope(y, cos, sin, first_half):
    half = ROPE_AXIS_DIM // 2
    partner = jnp.where(first_half, pltpu.roll(y, LANES - half, 1), pltpu.roll(y, half, 1))
    return y * cos + partner * sin


def _proj_kernel(x_ref, w_ref, cos_ref, sin_ref, gq_ref, gk_ref,
                 qa_ref, ka_ref, va_ref, qb_ref, kb_ref, vlo_ref, vhi_ref):
    x = x_ref[...].astype(BF16)

    def cols(c0, width):
        return jnp.dot(x, w_ref[:, c0:c0 + width], preferred_element_type=F32)

    cos = cos_ref[...]
    sin = sin_ref[...]
    lane = lax.broadcasted_iota(jnp.int32, cos.shape, 1)
    first_half = (lane % ROPE_AXIS_DIM) < ROPE_AXIS_DIM // 2
    lo = lane < HEAD_DIM

    def norm_rope(y, g):
        y2 = y * y
        ms_lo = jnp.sum(jnp.where(lo, y2, 0.0), axis=-1, keepdims=True)
        ms_hi = jnp.sum(jnp.where(lo, 0.0, y2), axis=-1, keepdims=True)
        ms = jnp.where(lo, ms_lo, ms_hi) * (1.0 / HEAD_DIM)
        return _rope(y * lax.rsqrt(ms + RMS_EPS) * g, cos, sin, first_half).astype(BF16)

    base = 3 * NA_WIDTH
    q = cols(base, GQA_WIDTH)
    gq = gq_ref[...]
    for g in range(GQA_GROUP):
        qb_ref[:, g * LANES:(g + 1) * LANES] = norm_rope(q[:, g * LANES:(g + 1) * LANES], gq)
    base += GQA_WIDTH
    kv = cols(base, 2 * KV_WIDTH)
    kb_ref[...] = norm_rope(kv[:, :KV_WIDTH], gk_ref[...])
    v = kv[:, KV_WIDTH:]
    vlo_ref[...] = jnp.where(lo, v, 1.0).astype(BF16)
    vhi_ref[...] = jnp.where(lo, 1.0, v).astype(BF16)
    qa_ref[...] = cols(0, NA_WIDTH).astype(BF16)
    ka_ref[...] = cols(NA_WIDTH, NA_WIDTH).astype(BF16)
    va_ref[...] = cols(2 * NA_WIDTH, NA_WIDTH).astype(BF16)


def _proj(x2d, seq_len, w, cos, sin, gq, gk):
    n = x2d.shape[0]
    tm = min(PROJ_TM, seq_len)
    assert n % tm == 0 and seq_len % tm == 0
    pos_blocks = seq_len // tm
    row = lambda i: (i, 0)
    const = lambda i: (0, 0)
    outs = [(NA_WIDTH, BF16)] * 3 + [(GQA_WIDTH, BF16)] + [(KV_WIDTH, BF16)] * 3
    return pl.pallas_call(
        _proj_kernel,
        grid=(n // tm,),
        in_specs=[
            pl.BlockSpec((tm, D_MODEL), row),
            pl.BlockSpec((D_MODEL, PROJ_COLS), const),
            pl.BlockSpec((tm, LANES), lambda i: (i % pos_blocks, 0)),
            pl.BlockSpec((tm, LANES), lambda i: (i % pos_blocks, 0)),
            pl.BlockSpec((1, LANES), const),
            pl.BlockSpec((1, LANES), const),
        ],
        out_specs=[pl.BlockSpec((tm, c), row) for c, _ in outs],
        out_shape=[jax.ShapeDtypeStruct((n, c), dt) for c, dt in outs],
        compiler_params=_cparams(("parallel",)),
        name="proj",
    )(x2d, w, cos, sin, gq, gk)


def _na_kernel(q_ref, k_ref, v_ref, bias_ref, o_ref, *, rows):
    j = pl.program_id(1)
    step_rows = q_ref.shape[1] // GRID_W
    width = NA_PASS_HEADS * HEAD_DIM
    lane_head = lax.broadcasted_iota(jnp.int32, (GRID_W, width), 1) // HEAD_DIM

    def one_row(rr, carry):
        r = j * step_rows + rr
        rs = jnp.clip(r - NA_WIN_R // 2, 0, rows - NA_WIN_R)
        var = r - rs
        k0 = pl.multiple_of(rs * GRID_W, GRID_W)
        q0 = pl.multiple_of(rr * GRID_W, GRID_W)
        for g in range(NA_HEADS // NA_PASS_HEADS):
            cols = slice(g * width, (g + 1) * width)
            heads = range(g * NA_PASS_HEADS, (g + 1) * NA_PASS_HEADS)
            q = q_ref[0, pl.ds(q0, GRID_W), cols]
            zero = jnp.zeros_like(q)
            qs = jnp.concatenate([jnp.where(lane_head == a, q, zero) for a in range(NA_PASS_HEADS)], axis=0)
            k = k_ref[0, pl.ds(k0, NA_WIN_R * GRID_W), cols]
            v = v_ref[0, pl.ds(k0, NA_WIN_R * GRID_W), cols]
            s = lax.dot_general(qs, k, (((1,), (1,)), ((), ())), preferred_element_type=F32)
            s = s + jnp.concatenate([bias_ref[var, h] for h in heads], axis=0)
            m = jnp.max(s, axis=-1, keepdims=True)
            p = jnp.exp2(s - m)
            l = jnp.sum(p, axis=-1, keepdims=True)
            os = jnp.dot(p.astype(BF16), v, preferred_element_type=F32) / l
            o = os[:GRID_W]
            for a in range(1, NA_PASS_HEADS):
                o = jnp.where(lane_head == a, os[a * GRID_W:(a + 1) * GRID_W], o)
            o_ref[0, pl.ds(q0, GRID_W), cols] = o.astype(BF16)
        return carry

    lax.fori_loop(0, step_rows, one_row, 0, unroll=True)


def _na(qa, ka, va, bias):
    b, t, _ = qa.shape
    rows = t // GRID_W
    step_rows = min(NA_ROWS, rows)
    assert rows >= NA_WIN_R and rows % step_rows == 0
    tq = step_rows * GRID_W
    return pl.pallas_call(
        functools.partial(_na_kernel, rows=rows),
        grid=(b, rows // step_rows),
        in_specs=[
            pl.BlockSpec((1, tq, NA_WIDTH), lambda i, j: (i, j, 0)),
            pl.BlockSpec((1, t, NA_WIDTH), lambda i, j: (i, 0, 0)),
            pl.BlockSpec((1, t, NA_WIDTH), lambda i, j: (i, 0, 0)),
            pl.BlockSpec(bias.shape, lambda i, j: (0, 0, 0, 0), pipeline_mode=pl.Buffered(1)),
        ],
        out_specs=pl.BlockSpec((1, tq, NA_WIDTH), lambda i, j: (i, j, 0)),
        out_shape=jax.ShapeDtypeStruct((b, t, NA_WIDTH), BF16),
        compiler_params=_cparams(("parallel", "parallel")),
        name="na",
    )(qa, ka, va, bias)


def _na_bias_tables(rpb):
    off = jnp.arange(NA_WIN_R)
    jrow = jnp.arange(NA_WIN_R)
    dr = jrow[None, :] - off[:, None] + (NA_WIN_R - 1)
    c = jnp.arange(GRID_W)
    cs = jnp.clip(c - NA_WIN_C // 2, 0, GRID_W - NA_WIN_C)
    col_ok = (c[None, :] >= cs[:, None]) & (c[None, :] < cs[:, None] + NA_WIN_C)
    dc = jnp.clip(c[None, :] - c[:, None], -(NA_WIN_C - 1), NA_WIN_C - 1) + (NA_WIN_C - 1)
    sel_r = (dr[:, :, None] == jnp.arange(2 * NA_WIN_R - 1)[None, None, :]).astype(F32)
    sel_c = (dc[:, :, None] == jnp.arange(2 * NA_WIN_C - 1)[None, None, :]).astype(F32)
    bias = jnp.einsum('vja,hab,qkb->hvqjk', sel_r, rpb.astype(F32), sel_c, precision=lax.Precision.HIGHEST)
    bias = jnp.where(col_ok[None, None, :, None, :], bias * LOG2_E, NEG_INF)
    return jnp.moveaxis(bias, 0, 1).reshape(NA_WIN_R, NA_HEADS, GRID_W, NA_WIN_R * GRID_W)


def _gqa_kernel(q_ref, k_ref, vlo_ref, vhi_ref, o_ref):
    k = k_ref[0]
    lane = lax.broadcasted_iota(jnp.int32, (q_ref.shape[1], LANES), 1)
    lo = lane < HEAD_DIM

    def scores(h):
        g, hi = divmod(h, 2)
        q = q_ref[0, :, g * LANES:(g + 1) * LANES]
        zero = jnp.zeros_like(q)
        q = jnp.where(lo, zero, q) if hi else jnp.where(lo, q, zero)
        return lax.dot_general(q, k, (((1,), (1,)), ((), ())), preferred_element_type=F32)

    def attend(s, v):
        p = jnp.exp2(s - jnp.max(s, axis=-1, keepdims=True)).astype(BF16)
        ov = jnp.dot(p, v, preferred_element_type=F32)
        return ov / pltpu.roll(ov, HEAD_DIM, 1)

    n_heads = 2 * GQA_GROUP
    outs = []
    s_next = scores(0)
    for h in range(n_heads):
        s = s_next
        if h + 1 < n_heads:
            s_next = scores(h + 1)
        outs.append(attend(s, vhi_ref[0] if h % 2 else vlo_ref[0]))
        if h % 2:
            g = h // 2
            o_ref[0, :, g * LANES:(g + 1) * LANES] = jnp.where(lo, outs[h - 1], outs[h]).astype(BF16)


def _gqa(qb, kb, vlo, vhi):
    b, t, _ = qb.shape
    tq = min(t, GQA_SCORE_ELEMS // t)
    assert t % tq == 0 and tq % 8 == 0
    return pl.pallas_call(
        _gqa_kernel,
        grid=(b, t // tq),
        in_specs=[
            pl.BlockSpec((1, tq, GQA_WIDTH), lambda i, j: (i, j, 0)),
            pl.BlockSpec((1, t, KV_WIDTH), lambda i, j: (i, 0, 0)),
            pl.BlockSpec((1, t, KV_WIDTH), lambda i, j: (i, 0, 0)),
            pl.BlockSpec((1, t, KV_WIDTH), lambda i, j: (i, 0, 0)),
        ],
        out_specs=pl.BlockSpec((1, tq, GQA_WIDTH), lambda i, j: (i, j, 0)),
        out_shape=jax.ShapeDtypeStruct((b, t, GQA_WIDTH), BF16),
        compiler_params=_cparams(("parallel", "parallel")),
        name="gqa",
    )(qb, kb, vlo, vhi)


def _layer_norm(z, g, b):
    mu = jnp.mean(z, axis=-1, keepdims=True)
    zc = z - mu
    var = jnp.mean(zc * zc, axis=-1, keepdims=True)
    return zc * lax.rsqrt(var + LN_EPS) * g + b


def _rms(o, g):
    return o * lax.rsqrt(jnp.mean(o * o, axis=-1, keepdims=True) + RMS_EPS) * g


def _mix_kernel(oa_ref, ob_ref, x_ref, woa_ref, wob_ref, ga_ref, gb_ref, lng_ref, lnb_ref,
                rw_ref, rb_ref, tri_ref, cnt_in_ref,
                x1_ref, idx_ref, gate_ref, rank_ref, cnt_ref, carry_ref):
    @pl.when(pl.program_id(0) == 0)
    def _():
        carry_ref[...] = cnt_in_ref[...]

    na = _rms(oa_ref[...].astype(F32), ga_ref[...]).astype(BF16)
    nb = _rms(ob_ref[...].astype(F32), gb_ref[...]).astype(BF16)
    mixed = (jnp.dot(na, woa_ref[...], preferred_element_type=F32)
             + jnp.dot(nb, wob_ref[...], preferred_element_type=F32))
    x1 = _layer_norm(DN_ALPHA * x_ref[...] + mixed, lng_ref[...], lnb_ref[...])
    _tok_store(x1_ref, x1)

    logits = lax.dot_general(rw_ref[...], x1.astype(BF16), (((1,), (1,)), ((), ())),
                             preferred_element_type=F32) + rb_ref[...]
    tm = logits.shape[1]
    eidx = lax.broadcasted_iota(jnp.int32, (N_EXPERTS, tm), 0).astype(F32)
    work = logits
    vals, idxs, hots = [], [], []
    for _ in range(TOP_K):
        m = jnp.max(work, axis=0, keepdims=True)
        sel = jnp.min(jnp.where(work == m, eidx, float(N_EXPERTS)), axis=0, keepdims=True)
        hot = eidx == sel
        vals.append(m)
        idxs.append(sel)
        hots.append(hot)
        work = jnp.where(hot, -jnp.inf, work)
    es = [jnp.exp(v - vals[0]) for v in vals]
    den = es[0] + es[1] + es[2] + es[3]
    gate_ref[...] = jnp.concatenate([e / den for e in es], axis=0)
    idx_ref[...] = jnp.concatenate(idxs, axis=0).astype(jnp.int32)

    hot_all = hots[0] | hots[1] | hots[2] | hots[3]
    onehot = jnp.where(hot_all, 1.0, 0.0)
    before = jnp.dot(onehot.astype(BF16), tri_ref[...], preferred_element_type=F32)
    before = before + carry_ref[:, 0:1]
    ranks = [jnp.sum(jnp.where(hot, before, 0.0), axis=0, keepdims=True) for hot in hots]
    rank_ref[...] = jnp.concatenate(ranks, axis=0).astype(jnp.int32)
    carry_ref[...] = carry_ref[...] + jnp.sum(onehot, axis=1, keepdims=True)
    cnt_ref[...] = carry_ref[...]


def _mix(oa, ob, x2d, woa, wob, ga, gb, lng, lnb, rw_t, rb, cnt_in):
    n = x2d.shape[0]
    tm = min(MIX_TM, n)
    assert n % tm == 0
    tri = (jnp.arange(tm)[:, None] < jnp.arange(tm)[None, :]).astype(BF16)
    row = lambda i: (i, 0)
    col = lambda i: (0, i)
    const = lambda i: (0, 0)
    full = lambda a: pl.BlockSpec(a.shape, const)
    return pl.pallas_call(
        _mix_kernel,
        grid=(n // tm,),
        in_specs=[
            pl.BlockSpec((tm, NA_WIDTH), row),
            pl.BlockSpec((tm, GQA_WIDTH), row),
            pl.BlockSpec((tm, D_MODEL), row),
            full(woa), full(wob), full(ga), full(gb), full(lng), full(lnb),
            full(rw_t), full(rb), full(tri), full(cnt_in),
        ],
        out_specs=[
            pl.BlockSpec((tm * TOK_ROWS, LANES), row),
            pl.BlockSpec((TOP_K, tm), col),
            pl.BlockSpec((TOP_K, tm), col),
            pl.BlockSpec((TOP_K, tm), col),
            pl.BlockSpec((N_EXPERTS, LANES), const),
        ],
        out_shape=[
            jax.ShapeDtypeStruct((n * TOK_ROWS, LANES), F32),
            jax.ShapeDtypeStruct((TOP_K, n), jnp.int32),
            jax.ShapeDtypeStruct((TOP_K, n), F32),
            jax.ShapeDtypeStruct((TOP_K, n), jnp.int32),
            jax.ShapeDtypeStruct((N_EXPERTS, LANES), F32),
        ],
        scratch_shapes=[pltpu.VMEM((N_EXPERTS, LANES), F32)],
        compiler_params=_cparams(("arbitrary",)),
        name="mix",
    )(oa, ob, x2d, woa, wob, ga, gb, lng, lnb, rw_t, rb, tri, cnt_in)


def _slot_kernel(pad_ref, idx_ref, rank_ref, disp_ref, comb_ref):
    idx = idx_ref[...]
    start = jnp.zeros_like(idx)
    for e in range(N_EXPERTS):
        start = jnp.where(idx == e, pad_ref[e], start)
    dest = (start + rank_ref[...]) * TOK_ROWS
    tm = dest.shape[1]
    for k in range(TOP_K):
        disp_ref[0, :, k * tm:(k + 1) * tm] = dest[k:k + 1, :]
        for j in range(tm // COMB_TM):
            comb_ref[j, :, k * COMB_TM:(k + 1) * COMB_TM] = dest[k:k + 1, j * COMB_TM:(j + 1) * COMB_TM]


def _slots(pad_start, idx_t, rank_t):
    n = idx_t.shape[1]
    tm = min(DISP_TM, n)
    assert n % tm == 0 and tm % COMB_TM == 0
    col = lambda i, *_: (0, i)
    sub = tm // COMB_TM
    return pl.pallas_call(
        _slot_kernel,
        grid_spec=pltpu.PrefetchScalarGridSpec(
            num_scalar_prefetch=1,
            grid=(n // tm,),
            in_specs=[pl.BlockSpec((TOP_K, tm), col), pl.BlockSpec((TOP_K, tm), col)],
            out_specs=[pl.BlockSpec((1, 1, TOP_K * tm), lambda i, *_: (i, 0, 0)),
                       pl.BlockSpec((sub, 1, TOP_K * COMB_TM), lambda i, *_: (i, 0, 0))],
        ),
        out_shape=[jax.ShapeDtypeStruct((n // tm, 1, TOP_K * tm), jnp.int32),
                   jax.ShapeDtypeStruct((n // COMB_TM, 1, TOP_K * COMB_TM), jnp.int32)],
        compiler_params=_cparams(("parallel",)),
        name="slots",
    )(pad_start, idx_t, rank_t)


def _dispatch_kernel(tail_ref, has_tail_ref, dest_ref, x_ref, *rest, zero_tails):
    if zero_tails:
        xs_ref, zeros_ref, sem = rest
    else:
        _, xs_ref, sem = rest
    tm = dest_ref.shape[2] // TOP_K
    tile0 = pl.program_id(0) * (tm * TOK_ROWS)

    if zero_tails:
        @pl.when(pl.program_id(0) == 0)
        def _():
            zeros_ref[...] = jnp.zeros_like(zeros_ref)

            def tail_copy(e):
                start = pl.multiple_of(tail_ref[e], EXP_BLK * TOK_ROWS)
                return pltpu.make_async_copy(zeros_ref, xs_ref.at[pl.ds(start, EXP_BLK * TOK_ROWS), :], sem)

            for e in range(N_EXPERTS):
                @pl.when(has_tail_ref[e] != 0)
                def _():
                    tail_copy(e).start()
            for e in range(N_EXPERTS):
                @pl.when(has_tail_ref[e] != 0)
                def _():
                    tail_copy(e).wait()

    def row_copy(i, k):
        src = pl.multiple_of(tile0 + i * TOK_ROWS, TOK_ROWS)
        dst = pl.multiple_of(dest_ref[0, 0, k * tm + i], TOK_ROWS)
        return pltpu.make_async_copy(x_ref.at[pl.ds(src, TOK_ROWS), :], xs_ref.at[pl.ds(dst, TOK_ROWS), :], sem)

    def issue(i, c):
        for k in range(TOP_K):
            row_copy(i, k).start(priority=k % 2)
        return c

    lax.fori_loop(0, tm, issue, 0, unroll=4)
    for k in range(TOP_K):
        pltpu.make_async_copy(x_ref.at[pl.ds(0, tm * TOK_ROWS), :], xs_ref.at[pl.ds(0, tm * TOK_ROWS), :], sem).wait()


def _dispatch(tail, has_tail, dest_tiles, x1, xs_prev, n_slots):
    n = x1.shape[0] // TOK_ROWS
    tm = dest_tiles.shape[2] // TOP_K
    assert n % tm == 0
    zero_tails = xs_prev is None
    in_specs = [
        pl.BlockSpec((1, 1, TOP_K * tm), lambda i, *_: (i, 0, 0), memory_space=pltpu.SMEM),
        pl.BlockSpec(memory_space=pl.ANY),
    ]
    args = [dest_tiles, x1]
    scratch = []
    aliases = {}
    if zero_tails:
        scratch.append(pltpu.VMEM((EXP_BLK * TOK_ROWS, LANES), F32))
    else:
        in_specs.append(pl.BlockSpec(memory_space=pl.ANY))
        args.append(xs_prev)
        aliases = {4: 0}
    scratch.append(pltpu.SemaphoreType.DMA(()))
    return pl.pallas_call(
        functools.partial(_dispatch_kernel, zero_tails=zero_tails),
        grid_spec=pltpu.PrefetchScalarGridSpec(
            num_scalar_prefetch=2,
            grid=(n // tm,),
            in_specs=in_specs,
            out_specs=pl.BlockSpec(memory_space=pl.ANY),
            scratch_shapes=scratch,
        ),
        out_shape=jax.ShapeDtypeStruct((n_slots * TOK_ROWS, LANES), F32),
        input_output_aliases=aliases,
        compiler_params=pltpu.CompilerParams(dimension_semantics=("arbitrary",), vmem_limit_bytes=VMEM_LIMIT,
                                             has_side_effects=True),
        name="dispatch",
    )(tail, has_tail, *args)


def _w1_prep_kernel(w_ref, perm_ref, o_ref):
    for c in range(w_ref.shape[2] // W1_GROUP):
        cols = slice(c * W1_GROUP, (c + 1) * W1_GROUP)
        w = w_ref[0, :, cols].astype(BF16)
        o_ref[0, :, cols] = jnp.dot(w, perm_ref[...], preferred_element_type=F32).astype(BF16)


def _w1_prep(w1):
    e, d, f2 = w1.shape
    half = f2 // 2
    j = jnp.arange(W1_GROUP)
    dst = jnp.where(j % 2 == 0, j // 2, LANES + j // 2)
    perm = (dst[:, None] == jnp.arange(W1_GROUP)[None, :]).astype(BF16)
    return pl.pallas_call(
        _w1_prep_kernel,
        grid=(e, 2),
        in_specs=[pl.BlockSpec((1, d, half), lambda i, j: (i, 0, j)),
                  pl.BlockSpec((W1_GROUP, W1_GROUP), lambda i, j: (0, 0))],
        out_specs=pl.BlockSpec((1, d, half), lambda i, j: (i, 0, j)),
        out_shape=jax.ShapeDtypeStruct((e, d, f2), BF16),
        compiler_params=_cparams(("parallel", "parallel")),
        name="w1prep",
    )(w1, perm)


def _expert_kernel(be_ref, nused_ref, xs_ref, w1_ref, w2_ref, b1_ref, b2_ref, o_ref):
    @pl.when(pl.program_id(0) < nused_ref[0])
    def _():
        x = _tok_load(xs_ref, EXP_BLK).astype(BF16)
        h = jnp.dot(x, w1_ref[0], preferred_element_type=F32) + b1_ref[0]
        groups = range(h.shape[1] // W1_GROUP)
        glu = jnp.concatenate([h[:, c * W1_GROUP: c * W1_GROUP + LANES] for c in groups], axis=1)
        lin = jnp.concatenate([h[:, c * W1_GROUP + LANES: (c + 1) * W1_GROUP] for c in groups], axis=1)
        glu = jnp.minimum(glu, SWIGLU_LIMIT)
        lin = jnp.clip(lin, -SWIGLU_LIMIT, SWIGLU_LIMIT)
        act = glu * jax.nn.sigmoid(SWIGLU_ALPHA * glu) * (lin + 1.0)
        w2 = w2_ref[0].astype(BF16)
        _tok_store(o_ref, jnp.dot(act.astype(BF16), w2, preferred_element_type=F32) + b2_ref[0])


def _experts(block_expert, n_used, xs, w1p, w2, b1p, b2):
    n_slots = xs.shape[0] // TOK_ROWS
    n_blocks = n_slots // EXP_BLK
    slot = lambda i, be, nu: (jnp.minimum(i, nu[0] - 1), 0)
    wsel = lambda i, be, nu: (be[i], 0, 0)
    return pl.pallas_call(
        _expert_kernel,
        grid_spec=pltpu.PrefetchScalarGridSpec(
            num_scalar_prefetch=2,
            grid=(n_blocks,),
            in_specs=[
                pl.BlockSpec((EXP_BLK * TOK_ROWS, LANES), slot),
                pl.BlockSpec((1, D_MODEL, 2 * D_FF), wsel),
                pl.BlockSpec((1, D_FF, D_MODEL), wsel),
                pl.BlockSpec((1, 1, 2 * D_FF), wsel),
                pl.BlockSpec((1, 1, D_MODEL), wsel),
            ],
            out_specs=pl.BlockSpec((EXP_BLK * TOK_ROWS, LANES), slot),
        ),
        out_shape=jax.ShapeDtypeStruct((n_slots * TOK_ROWS, LANES), F32),
        compiler_params=_cparams(("arbitrary",)),
        name="experts",
    )(block_expert, n_used, xs, w1p, w2, b1p, b2)


def _combine_kernel(dest_ref, dest_next_ref, x1_ref, gate_ref, lng_ref, lnb_ref, ys_ref, o_ref, buf_ref, sems):
    tm = x1_ref.shape[0] // TOK_ROWS
    step = pl.program_id(0)
    slot = step % 2

    def gather(idx_ref, into):
        def issue(i, c):
            for k in range(TOP_K):
                src = pl.multiple_of(idx_ref[0, 0, k * tm + i], TOK_ROWS)
                dst = pl.multiple_of(i * TOK_ROWS, TOK_ROWS)
                pltpu.make_async_copy(ys_ref.at[pl.ds(src, TOK_ROWS), :],
                                      buf_ref.at[into, k, pl.ds(dst, TOK_ROWS), :], sems.at[into]
                                      ).start(priority=k % 2)
            return c

        lax.fori_loop(0, tm, issue, 0, unroll=4)

    @pl.when(step == 0)
    def _():
        gather(dest_ref, slot)

    @pl.when(step + 1 < pl.num_programs(0))
    def _():
        gather(dest_next_ref, 1 - slot)

    for k in range(TOP_K):
        pltpu.make_async_copy(ys_ref.at[pl.ds(0, tm * TOK_ROWS), :], buf_ref.at[slot, k], sems.at[slot]).wait()
    gate = gate_ref[...]
    y = _tok_load(buf_ref, tm, (slot, 0)) * gate[:, 0:1]
    for k in range(1, TOP_K):
        y = y + _tok_load(buf_ref, tm, (slot, k)) * gate[:, k:k + 1]
    o_ref[...] = _layer_norm(DN_ALPHA * _tok_load(x1_ref, tm) + y, lng_ref[...], lnb_ref[...])


def _combine(dest_tiles, x1, gate, lng, lnb, ys):
    n = x1.shape[0] // TOK_ROWS
    tm = COMB_TM
    assert n % tm == 0
    n_tiles = n // tm
    return pl.pallas_call(
        _combine_kernel,
        grid=(n_tiles,),
        in_specs=[
            pl.BlockSpec((1, 1, TOP_K * tm), lambda i: (i, 0, 0), memory_space=pltpu.SMEM),
            pl.BlockSpec((1, 1, TOP_K * tm), lambda i: (jnp.minimum(i + 1, n_tiles - 1), 0, 0),
                         memory_space=pltpu.SMEM),
            pl.BlockSpec((tm * TOK_ROWS, LANES), lambda i: (i, 0)),
            pl.BlockSpec((tm, TOP_K), lambda i: (i, 0)),
            pl.BlockSpec((1, D_MODEL), lambda i: (0, 0)),
            pl.BlockSpec((1, D_MODEL), lambda i: (0, 0)),
            pl.BlockSpec(memory_space=pl.ANY),
        ],
        out_specs=pl.BlockSpec((tm, D_MODEL), lambda i: (i, 0)),
        out_shape=jax.ShapeDtypeStruct((n, D_MODEL), F32),
        scratch_shapes=[pltpu.VMEM((2, TOP_K, tm * TOK_ROWS, LANES), F32), pltpu.SemaphoreType.DMA((2,))],
        compiler_params=_cparams(("arbitrary",)),
        name="combine",
    )(dest_tiles, dest_tiles, x1, gate, lng, lnb, ys)


def _rope_tables(t):
    pos = jnp.arange(t)
    row = (pos // GRID_W).astype(F32)
    col = (pos % GRID_W).astype(F32)
    inv = ROPE_THETA ** (-jnp.arange(0, ROPE_AXIS_DIM, 2, dtype=F32) / ROPE_AXIS_DIM)
    ar = row[:, None] * inv[None, :]
    ac = col[:, None] * inv[None, :]
    cos = jnp.concatenate([jnp.cos(ar), jnp.cos(ar), jnp.cos(ac), jnp.cos(ac)], axis=1)
    sin = jnp.concatenate([-jnp.sin(ar), jnp.sin(ar), -jnp.sin(ac), jnp.sin(ac)], axis=1)
    return jnp.tile(cos, (1, 2)), jnp.tile(sin, (1, 2))


def _prep_w_in(w):
    qa = w[:, :NA_WIDTH] * SCORE_SCALE
    kva = w[:, NA_WIDTH:3 * NA_WIDTH]
    qb = _gqa_out_order(w[:, 3 * NA_WIDTH:3 * NA_WIDTH + GQA_WIDTH].T).T
    rest = w[:, 3 * NA_WIDTH + GQA_WIDTH:]
    return jnp.concatenate([qa, kva, qb, rest], axis=1).astype(BF16)


def _gqa_out_order(a):
    rest = a.shape[1:]
    a = a.reshape((GQA_KV_HEADS, GQA_GROUP, HEAD_DIM) + rest)
    return jnp.swapaxes(a, 0, 1).reshape((GQA_WIDTH,) + rest)


def kernel(x_prompt, x_sample, w_in, rpb, q_norm_g, k_norm_g, g_out_na, g_out_gqa, w_o, ln1_g, ln1_b,
           router_w, router_b, w1, b1, w2, b2, ln2_g, ln2_b):
    assert GQA_KV_HEADS == 2 and KV_WIDTH == LANES
    xs_in = [x_prompt, x_sample]
    l = 0
    w_proj = _prep_w_in(w_in[l])
    gq = jnp.tile(q_norm_g[l] * SCORE_SCALE, 2).reshape(1, LANES)
    gk = jnp.tile(k_norm_g[l], 2).reshape(1, LANES)
    bias = _na_bias_tables(rpb[l])
    woa = w_o[l][:NA_WIDTH].astype(BF16)
    wob = _gqa_out_order(w_o[l][NA_WIDTH:]).astype(BF16)
    ga = g_out_na[l].reshape(1, NA_WIDTH)
    gb = _gqa_out_order(g_out_gqa[l]).reshape(1, GQA_WIDTH)
    ln1g, ln1b = ln1_g[l].reshape(1, D_MODEL), ln1_b[l].reshape(1, D_MODEL)
    ln2g, ln2b = ln2_g[l].reshape(1, D_MODEL), ln2_b[l].reshape(1, D_MODEL)
    rw_t = router_w[l].T.astype(BF16)
    rb = router_b[l].reshape(N_EXPERTS, 1)
    w1p = _w1_prep(w1[l])
    w2b = w2[l]
    b1p = jnp.swapaxes(b1[l].reshape(N_EXPERTS, 2 * D_FF // W1_GROUP, LANES, 2), 2, 3).reshape(N_EXPERTS, 1, 2 * D_FF)
    b2r = b2[l].reshape(N_EXPERTS, 1, D_MODEL)

    x1s, idxs, gates, ranks = [], [], [], []
    cnt = jnp.zeros((N_EXPERTS, LANES), F32)
    for x in xs_in:
        b, t, _ = x.shape
        x2d = x.reshape(b * t, D_MODEL)
        cos, sin = _rope_tables(t)
        qa, ka, va, qb, kb, vlo, vhi = _proj(x2d, t, w_proj, cos, sin, gq, gk)
        sh = lambda a: a.reshape(b, t, a.shape[-1])
        oa = _na(sh(qa), sh(ka), sh(va), bias).reshape(b * t, NA_WIDTH)
        ob = _gqa(sh(qb), sh(kb), sh(vlo), sh(vhi)).reshape(b * t, GQA_WIDTH)
        x1, idx_t, gate_t, rank_t, cnt = _mix(oa, ob, x2d, woa, wob, ga, gb, ln1g, ln1b, rw_t, rb, cnt)
        x1s.append(x1)
        idxs.append(idx_t)
        gates.append(gate_t)
        ranks.append(rank_t)

    counts = cnt[:, 0].astype(jnp.int32)
    nblk = (counts + EXP_BLK - 1) // EXP_BLK
    blk_end = jnp.cumsum(nblk)
    pad_start = (blk_end - nblk) * EXP_BLK
    n_assign = TOP_K * sum(x.shape[0] * x.shape[1] for x in xs_in)
    n_blocks = -(-n_assign // EXP_BLK) + N_EXPERTS
    n_slots = n_blocks * EXP_BLK
    blocks = jnp.arange(n_blocks, dtype=jnp.int32)
    block_expert = jnp.minimum(jnp.sum((blk_end[None, :] <= blocks[:, None]).astype(jnp.int32), axis=1),
                               N_EXPERTS - 1)
    n_used = blk_end[-1:].astype(jnp.int32)
    tail = ((blk_end - 1) * (EXP_BLK * TOK_ROWS)).astype(jnp.int32)
    has_tail = (counts % EXP_BLK != 0).astype(jnp.int32)
    dests = [_slots(pad_start.astype(jnp.int32), idx_t, rank_t) for idx_t, rank_t in zip(idxs, ranks)]

    slots = None
    for x1, (dest_disp, _) in zip(x1s, dests):
        slots = _dispatch(tail, has_tail, dest_disp, x1, slots, n_slots)
    ys = _experts(block_expert, n_used, slots, w1p, w2b, b1p, b2r)
    outs = []
    for x, x1, (_, dest_comb), gate_t in zip(xs_in, x1s, dests, gates):
        y = _combine(dest_comb, x1, gate_t.T, ln2g, ln2b, ys)
        outs.append(y.reshape(x.shape))
    return tuple(outs)
```
